```python
import math
import jax, jax.numpy as jnp
from jax import lax
import numpy as np

D_MODEL = 2048
BATCH = 2
SEQ = 4096
DEPTH = 1

ML_HEADS = 4
ML_DQK = 128
ML_DV = 256
ML_CHUNK = 64
ML_W = ML_HEADS * ML_DV
DA_HEADS = 8
DA_DQK = 64
DA_DV = 2 * DA_DQK
DA_W = DA_HEADS * DA_DV
Q_BLOCK = 128
ROPE_THETA = 10000.0
N_EXPERTS = 32
TOP_K = 4
D_EXPERT = D_MODEL
SWIGLU_LIMIT = 7.0
SWIGLU_ALPHA = 1.702
MOE_BLOCK = 128
EPS = 1e-6

IN_SPLITS = (
    ML_HEADS * ML_DQK,
    ML_HEADS * ML_DQK,
    ML_W,
    ML_W,
    4 * ML_HEADS,
    DA_HEADS * 2 * DA_DQK,
    DA_HEADS * 2 * DA_DQK,
    DA_W,
    D_MODEL,
    D_MODEL,
)
D_IN = sum(IN_SPLITS)

kernel_name = "hybrid_mlstm_diffattn_moe_encoder"


def rms_norm(x, g):
    xf = x.astype(jnp.float32)
    y = xf * lax.rsqrt(jnp.mean(xf * xf, axis=-1, keepdims=True) + EPS)
    return (y * g.astype(jnp.float32)).astype(x.dtype)


def split_columns(p):
    out = []
    off = 0
    for n in IN_SPLITS:
        out.append(p[..., off:off + n])
        off += n
    return out


def rope_tables(seq, dim):
    inv = ROPE_THETA ** (-jnp.arange(0, dim, 2, dtype=jnp.float32) / dim)
    ang = jnp.arange(seq, dtype=jnp.float32)[:, None] * inv[None, :]
    return jnp.cos(ang), jnp.sin(ang)


def apply_rope(x, cos, sin):
    half = x.shape[-1] // 2
    x1, x2 = x[..., :half], x[..., half:]
    return jnp.concatenate([x1 * cos - x2 * sin, x2 * cos + x1 * sin], axis=-1)


def mlstm_one_direction(q, k, v, ig, fg):
    B, H, S, dk = q.shape
    dv = v.shape[-1]
    L = ML_CHUNK
    NC = S // L
    q = q.reshape(B, H, NC, L, dk)
    k = k.reshape(B, H, NC, L, dk)
    v = v.reshape(B, H, NC, L, dv)
    ig = ig.reshape(B, H, NC, L)
    b = jnp.cumsum(jax.nn.log_sigmoid(fg).reshape(B, H, NC, L), axis=-1)
    g = b[..., -1]
    a = g[..., None] - b + ig

    def step(carry, inp):
        C, n, m = carry
        a_c, g_c, k_c, v_c = inp
        m_new = jnp.maximum(g_c + m, jnp.max(a_c, axis=-1))
        decay = jnp.exp(g_c + m - m_new)
        w = jnp.exp(a_c - m_new[..., None])
        C_new = decay[..., None, None] * C + jnp.einsum('bhsv,bhsk->bhvk', w[..., None] * v_c, k_c)
        n_new = decay[..., None] * n + jnp.einsum('bhs,bhsk->bhk', w, k_c)
        return (C_new, n_new, m_new), (C, n, m)

    init = (jnp.zeros((B, H, dv, dk), jnp.float32),
            jnp.zeros((B, H, dk), jnp.float32),
            jnp.zeros((B, H), jnp.float32))
    xs = (jnp.moveaxis(a, 2, 0), jnp.moveaxis(g, 2, 0),
          jnp.moveaxis(k, 2, 0), jnp.moveaxis(v, 2, 0))
    _, (C_prev, n_prev, m_prev) = lax.scan(step, init, xs)
    C_prev = jnp.moveaxis(C_prev, 0, 2)
    n_prev = jnp.moveaxis(n_prev, 0, 2)
    m_prev = jnp.moveaxis(m_prev, 0, 2)

    lower = jnp.tril(jnp.ones((L, L), dtype=bool))
    logD = b[..., :, None] - b[..., None, :] + ig[..., None, :]
    logD = jnp.where(lower, logD, -jnp.inf)
    inter = b + m_prev[..., None]
    m_t = jnp.maximum(inter, jnp.max(logD, axis=-1))
    s_inter = jnp.exp(inter - m_t)
    Dqk = jnp.exp(logD - m_t[..., None]) * jnp.einsum('bhctk,bhcsk->bhcts', q, k)
    num = (s_inter[..., None] * jnp.einsum('bhcvk,bhctk->bhctv', C_prev, q)
           + jnp.einsum('bhcts,bhcsv->bhctv', Dqk, v))
    den = s_inter * jnp.einsum('bhck,bhctk->bhct', n_prev, q) + jnp.sum(Dqk, axis=-1)
    den = jnp.maximum(jnp.abs(den), jnp.exp(-m_t))
    return (num / den[..., None]).reshape(B, H, S, dv)


def mlstm_branch(mq, mk, mv, mo, mg, gate_bias, norm_g, w_out):
    B, S, _ = mq.shape
    f32 = jnp.float32
    q = mq.astype(f32).reshape(B, S, ML_HEADS, ML_DQK).transpose(0, 2, 1, 3)
    k = mk.astype(f32).reshape(B, S, ML_HEADS, ML_DQK).transpose(0, 2, 1, 3) * (ML_DQK ** -0.5)
    v = mv.astype(f32).reshape(B, S, ML_HEADS, ML_DV).transpose(0, 2, 1, 3)
    gates = mg.astype(f32).reshape(B, S, 4, ML_HEADS) + gate_bias.astype(f32)
    i_f, f_f, i_b, f_b = gates.transpose(2, 0, 3, 1)
    rev = lambda t: jnp.flip(t, axis=2)
    h_f = mlstm_one_direction(q, k, v, i_f, f_f)
    h_b = rev(mlstm_one_direction(rev(q), rev(k), rev(v), rev(i_b), rev(f_b)))
    h = rms_norm((h_f + h_b).transpose(0, 2, 1, 3), norm_g).reshape(B, S, ML_W)
    h = h * jax.nn.sigmoid(mo.astype(f32))
    return h.astype(mq.dtype) @ w_out


def diff_attn_branch(dq, dk, dv, q_norm_g, k_norm_g, lam_params, norm_g, w_out, cos, sin, lam_init):
    B, S, _ = dq.shape
    d = DA_DQK
    cs, sn = cos[None, :, None, None, :], sin[None, :, None, None, :]
    q = apply_rope(rms_norm(dq.reshape(B, S, DA_HEADS, 2, d), q_norm_g), cs, sn)
    k = apply_rope(rms_norm(dk.reshape(B, S, DA_HEADS, 2, d), k_norm_g), cs, sn)
    q = q.transpose(0, 2, 3, 1, 4)
    k = k.transpose(0, 2, 3, 1, 4)
    v = dv.reshape(B, S, DA_HEADS, DA_DV).transpose(0, 2, 1, 3)
    lp = lam_params.astype(jnp.float32)
    lam = jnp.exp(jnp.sum(lp[0] * lp[1])) - jnp.exp(jnp.sum(lp[2] * lp[3])) + lam_init
    nq = S // Q_BLOCK
    qb = q.reshape(B, DA_HEADS, 2, nq, Q_BLOCK, d).transpose(3, 0, 1, 2, 4, 5)
    scale = d ** -0.5

    def attend(q_blk):
        s = jnp.einsum('bhmqd,bhmkd->bhmqk', q_blk, k).astype(jnp.float32) * scale
        p = jax.nn.softmax(s, axis=-1)
        amap = p[:, :, 0] - lam * p[:, :, 1]
        return jnp.einsum('bhqk,bhkv->bhqv', amap, v.astype(jnp.float32))

    o = lax.map(attend, qb)
    o = o.transpose(1, 0, 3, 2, 4).reshape(B, S, DA_HEADS, DA_DV)
    o = rms_norm(o, norm_g) * (1.0 - lam_init)
    return o.reshape(B, S, DA_W).astype(dq.dtype) @ w_out


def moe_ffn(h, w_r, b_r, w_gu, b_gu, w_dn, b_dn):
    T = h.shape[0]
    TK = T * TOP_K
    logits = (h @ w_r).astype(jnp.float32) + b_r.astype(jnp.float32)
    top_v, top_i = lax.top_k(logits, TOP_K)
    wts = jax.nn.softmax(top_v, axis=-1)
    flat_e = top_i.reshape(-1)
    order = jnp.argsort(flat_e)
    e_sorted = flat_e[order]
    tok_sorted = order // TOP_K
    counts = jnp.bincount(flat_e, length=N_EXPERTS)
    padded = (counts + MOE_BLOCK - 1) // MOE_BLOCK * MOE_BLOCK
    pad_end = jnp.cumsum(padded)
    pad_start = pad_end - padded
    start = jnp.cumsum(counts) - counts
    dest = pad_start[e_sorted] + (jnp.arange(TK) - start[e_sorted])
    n_rows = TK + N_EXPERTS * MOE_BLOCK
    n_blk = n_rows // MOE_BLOCK
    row_tok = jnp.zeros((n_rows,), jnp.int32).at[dest].set(tok_sorted.astype(jnp.int32))
    blk_exp = jnp.minimum(jnp.searchsorted(pad_end, jnp.arange(n_blk) * MOE_BLOCK, side='right'),
                          N_EXPERTS - 1)

    def expert_block(args):
        rows, e = args
        xb = h[rows]
        gu = xb @ w_gu[e] + b_gu[e]
        glu = jnp.minimum(gu[:, 0::2], SWIGLU_LIMIT)
        lin = jnp.clip(gu[:, 1::2], -SWIGLU_LIMIT, SWIGLU_LIMIT)
        act = glu * jax.nn.sigmoid(SWIGLU_ALPHA * glu) * (lin + 1.0)
        return act @ w_dn[e] + b_dn[e]

    y_rows = lax.map(expert_block, (row_tok.reshape(n_blk, MOE_BLOCK), blk_exp))
    y_rows = y_rows.reshape(n_rows, h.shape[-1])
    y_assign = y_rows[dest].astype(jnp.float32) * wts.reshape(-1)[order][:, None]
    return jax.ops.segment_sum(y_assign, tok_sorted, num_segments=T).astype(h.dtype)


def setup_inputs(seed: int = 0) -> dict:
    key = jax.random.key(seed)
    ks = jax.random.split(key, 20)
    nrm = lambda k, shape, s: jax.random.normal(k, shape, jnp.float32) * s
    gain = lambda k, shape: 1.0 + 0.02 * jax.random.normal(k, shape, jnp.float32)
    gate_off = jnp.array([0.0, 1.0, 0.0, 1.0], jnp.float32)[:, None] * jnp.linspace(3.0, 6.0, ML_HEADS)[None, :]
    return {
        "x": nrm(ks[0], (BATCH, SEQ, D_MODEL), 1.0),
        "norm1_g": gain(ks[1], (DEPTH, D_MODEL)),
        "w_in": nrm(ks[2], (DEPTH, D_MODEL, D_IN), D_MODEL ** -0.5),
        "ml_gate_bias": gate_off[None] + nrm(ks[3], (DEPTH, 4, ML_HEADS), 0.1),
        "ml_norm_g": gain(ks[4], (DEPTH, ML_DV)),
        "w_ml_out": nrm(ks[5], (DEPTH, ML_W, D_MODEL), ML_W ** -0.5),
        "da_q_norm_g": gain(ks[6], (DEPTH, DA_DQK)),
        "da_k_norm_g": gain(ks[7], (DEPTH, DA_DQK)),
        "da_lambda": nrm(ks[8], (DEPTH, 4, DA_DQK), 0.1),
        "da_norm_g": gain(ks[9], (DEPTH, DA_DV)),
        "w_da_out": nrm(ks[10], (DEPTH, DA_W, D_MODEL), DA_W ** -0.5),
        "w_o": nrm(ks[11], (DEPTH, D_MODEL, D_MODEL), D_MODEL ** -0.5),
        "norm2_g": gain(ks[12], (DEPTH, D_MODEL)),
        "w_router": nrm(ks[13], (DEPTH, D_MODEL, N_EXPERTS), D_MODEL ** -0.5),
        "b_router": nrm(ks[14], (DEPTH, N_EXPERTS), 0.01),
        "w_gate_up": nrm(ks[15], (DEPTH, N_EXPERTS, D_MODEL, 2 * D_EXPERT), D_MODEL ** -0.5),
        "b_gate_up": nrm(ks[16], (DEPTH, N_EXPERTS, 2 * D_EXPERT), 0.01),
        "w_down": nrm(ks[17], (DEPTH, N_EXPERTS, D_EXPERT, D_MODEL), D_EXPERT ** -0.5),
        "b_down": nrm(ks[18], (DEPTH, N_EXPERTS, D_MODEL), 0.01),
    }


def reference(x, norm1_g, w_in, ml_gate_bias, ml_norm_g, w_ml_out, da_q_norm_g, da_k_norm_g,
              da_lambda, da_norm_g, w_da_out, w_o, norm2_g, w_router, b_router,
              w_gate_up, b_gate_up, w_down, b_down):
    B, S, D = x.shape
    cos, sin = rope_tables(S, DA_DQK)
    for l in range(DEPTH):
        lam_init = 0.8 - 0.6 * math.exp(-0.3 * l)
        h = rms_norm(x, norm1_g[l])
        proj = h @ w_in[l]
        mq, mk, mv, mo, mg, dq, dk, dv, gm, gd = split_columns(proj)
        y_m = mlstm_branch(mq, mk, mv, mo, mg, ml_gate_bias[l], ml_norm_g[l], w_ml_out[l])
        y_d = diff_attn_branch(dq, dk, dv, da_q_norm_g[l], da_k_norm_g[l], da_lambda[l],
                               da_norm_g[l], w_da_out[l], cos, sin, lam_init)
        mix = (jax.nn.sigmoid(gm.astype(jnp.float32)) * y_m.astype(jnp.float32)
               + jax.nn.sigmoid(gd.astype(jnp.float32)) * y_d.astype(jnp.float32))
        x = x + (mix.astype(x.dtype) @ w_o[l]).astype(x.dtype)
        h2 = rms_norm(x, norm2_g[l]).reshape(B * S, D)
        y = moe_ffn(h2, w_router[l], b_router[l], w_gate_up[l], b_gate_up[l], w_down[l], b_down[l])
        x = x + y.reshape(B, S, D).astype(x.dtype)
    return x
```

```python
import functools
import math

import jax
import jax.numpy as jnp
from jax import lax
from jax.experimental import pallas as pl
from jax.experimental.pallas import tpu as pltpu

F32 = jnp.float32
BF16 = jnp.bfloat16

D_MODEL = 2048
ML_HEADS = 4
ML_DQK = 128
ML_DV = 256
ML_W = ML_HEADS * ML_DV
DA_HEADS = 8
DA_DQK = 64
DA_DV = 2 * DA_DQK
DA_W = DA_HEADS * DA_DV
ROPE_THETA = 10000.0
N_EXPERTS = 32
TOP_K = 4
D_EXPERT = D_MODEL
SWIGLU_LIMIT = 7.0
SWIGLU_ALPHA = 1.702
EPS = 1e-6
LOG2E = 1.4426950408889634

LANES = 128
VMEM_LIMIT = 48 * 1024 * 1024

_OFF_MG = 2 * ML_HEADS * ML_DQK + 2 * ML_W
_N_MG = 4 * ML_HEADS
_C_MQ, _C_MK, _C_MV, _C_MO = 0, 512, 1024, 2048
_C_DQ, _C_DK, _C_DV, _C_GM, _C_GD = 3072, 4096, 5120, 6144, 8192
_N_MAIN = 10240

ML_CHUNK = 256
AT_QB = 256
AT_KB = 512
MOE_SUB = 256
MOE_RMAX = 1024
MOE_FC = 256


def _cparams(sem, vmem=VMEM_LIMIT):
    return pltpu.CompilerParams(dimension_semantics=sem, vmem_limit_bytes=vmem)


def _inproj_body(x_ref, g_ref, w_ref, wg_ref, gb_ref, o_ref, og_ref, xn_ref):
    @pl.when(pl.program_id(1) == 0)
    def _():
        x = x_ref[...]
        ms = jnp.mean(x * x, axis=-1, keepdims=True)
        xn = (x * lax.rsqrt(ms + EPS) * g_ref[...]).astype(BF16)
        xn_ref[...] = xn
        og_ref[...] = jnp.dot(xn, wg_ref[...], preferred_element_type=F32) + gb_ref[...]

    o_ref[...] = jnp.dot(xn_ref[...], w_ref[...], preferred_element_type=F32).astype(o_ref.dtype)


def _inproj(x2, g1, w_main, w_gate, gate_bias, bm=1024, bn=1024):
    m, d = x2.shape
    n = w_main.shape[1]
    return pl.pallas_call(
        _inproj_body,
        grid=(m // bm, n // bn),
        in_specs=[
            pl.BlockSpec((bm, d), lambda i, j: (i, 0)),
            pl.BlockSpec((1, d), lambda i, j: (0, 0)),
            pl.BlockSpec((d, bn), lambda i, j: (0, j)),
            pl.BlockSpec((d, LANES), lambda i, j: (0, 0)),
            pl.BlockSpec((1, LANES), lambda i, j: (0, 0)),
        ],
        out_specs=[
            pl.BlockSpec((bm, bn), lambda i, j: (i, j)),
            pl.BlockSpec((bm, LANES), lambda i, j: (i, 0)),
        ],
        out_shape=[
            jax.ShapeDtypeStruct((m, n), BF16),
            jax.ShapeDtypeStruct((m, LANES), F32),
        ],
        scratch_shapes=[pltpu.VMEM((bm, d), BF16)],
        compiler_params=_cparams(("parallel", "arbitrary")),
        name="inproj",
    )(x2, g1, w_main, w_gate, gate_bias)


def _gateprep_body(g_ref, o_ref):
    x = g_ref[...]
    c = x.shape[1]
    lf = jnp.minimum(x, 0.0) - jnp.log1p(jnp.exp(-jnp.abs(x)))
    r = lax.broadcasted_iota(jnp.int32, (c, c), 0)
    s = lax.broadcasted_iota(jnp.int32, (c, c), 1)
    upper = (r <= s).astype(F32)
    lower = (r >= s).astype(F32)
    pre = jnp.dot(lf, upper, preferred_element_type=F32, precision=lax.Precision.HIGHEST)
    suf = jnp.dot(lf, lower, preferred_element_type=F32, precision=lax.Precision.HIGHEST)
    row = lax.broadcasted_iota(jnp.int32, x.shape, 0)
    h = ML_HEADS
    out = jnp.where((row >= h) & (row < 2 * h), pre, x)
    out = jnp.where(row >= 3 * h, suf, out)
    o_ref[...] = out


def _gateprep(gt, chunk):
    r, n = gt.shape
    return pl.pallas_call(
        _gateprep_body,
        grid=(n // chunk,),
        in_specs=[pl.BlockSpec((r, chunk), lambda i: (0, i))],
        out_specs=pl.BlockSpec((r, chunk), lambda i: (0, i)),
        out_shape=jax.ShapeDtypeStruct((r, n), F32),
        compiler_params=_cparams(("parallel",)),
        name="gateprep",
    )(gt)


def _mlstm_body(q_ref, kt_ref, v_ref, mo_ref, grow_ref, gcol_ref, ng_ref, o_ref,
                vaug_ref, hf_ref, hb_ref, *, chunk):
    s_len = q_ref.shape[0]
    nc = s_len // chunk
    dv = ML_DV
    wide = dv + LANES
    inv_scale = float(ML_DQK) ** 0.5

    vaug_ref[:, :dv] = v_ref[...]
    lane = lax.broadcasted_iota(jnp.int32, (s_len, LANES), 1)
    vaug_ref[:, dv:] = jnp.where(lane == 0, 1.0, 0.0).astype(BF16)

    rr = lax.broadcasted_iota(jnp.int32, (chunk, chunk), 0)
    cc = lax.broadcasted_iota(jnp.int32, (chunk, chunk), 1)

    def chunk_step(c, state, m_prev, reverse):
        r0 = pl.multiple_of(c * chunk, chunk)
        gi, gb = (2, 3) if reverse else (0, 1)
        qc = q_ref[pl.ds(r0, chunk), :]
        ktc = kt_ref[:, pl.ds(r0, chunk)]
        vac = vaug_ref[pl.ds(r0, chunk), :]
        i_row = grow_ref[gi, :, pl.ds(r0, chunk)]
        b_row = grow_ref[gb, :, pl.ds(r0, chunk)]
        gcol = gcol_ref[0, pl.ds(r0, chunk), :]
        i_col = gcol[:, gi:gi + 1]
        b_col = gcol[:, gb:gb + 1]
        mask = (cc >= rr) if reverse else (cc <= rr)
        log_d = jnp.where(mask, b_col - (b_row - i_row), -jnp.inf)
        inter = b_col + m_prev
        m_t = jnp.maximum(inter, jnp.max(log_d, axis=1, keepdims=True))
        s_inter = jnp.exp(inter - m_t)
        dm = jnp.exp(log_d - m_t)
        sqk = jnp.dot(qc, ktc, preferred_element_type=F32)
        p = (dm * sqk).astype(BF16)
        nd = (jnp.dot(p, vac, preferred_element_type=F32)
              + s_inter * jnp.dot(qc, state.astype(BF16), preferred_element_type=F32))
        den = nd[:, dv:dv + 1]
        floor = jnp.exp(-m_t) * inv_scale
        h = nd[:, :dv] / jnp.maximum(jnp.abs(den), floor)
        g = b_col[0:1, :] if reverse else b_col[chunk - 1:chunk, :]
        a_col = g - b_col + i_col
        m_new = jnp.maximum(g + m_prev, jnp.max(a_col, axis=0, keepdims=True))
        decay = jnp.exp(g + m_prev - m_new)
        w_col = jnp.exp(a_col - m_new)
        wv = (w_col * vac.astype(F32)).astype(BF16)
        new_state = decay * state + jnp.dot(ktc, wv, preferred_element_type=F32)
        return r0, h, new_state, m_new

    def body(c, carry):
        sf, mf, sb, mb = carry
        r0, h, sf, mf = chunk_step(c, sf, mf, False)
        hf_ref[pl.ds(r0, chunk), :] = h
        r1, h2, sb, mb = chunk_step(nc - 1 - c, sb, mb, True)
        hb_ref[pl.ds(r1, chunk), :] = h2
        return sf, mf, sb, mb

    z = jnp.zeros((ML_DQK, wide), F32)
    m0 = jnp.zeros((1, 1), F32)
    lax.fori_loop(0, nc, body, (z, m0, z, m0))

    def epilogue(c, _):
        r0 = pl.multiple_of(c * chunk, chunk)
        h = hf_ref[pl.ds(r0, chunk), :] + hb_ref[pl.ds(r0, chunk), :]
        ms = jnp.mean(h * h, axis=-1, keepdims=True)
        hn = h * lax.rsqrt(ms + EPS) * ng_ref[...]
        gate = jax.nn.sigmoid(mo_ref[pl.ds(r0, chunk), :].astype(F32))
        o_ref[pl.ds(r0, chunk), :] = (hn * gate).astype(o_ref.dtype)
        return 0

    lax.fori_loop(0, nc, epilogue, 0)


def _mlstm(proj, kt, grow, gcol, ng, batch, seq, chunk):
    m = proj.shape[0]
    nsb = 1
    del nsb
    return pl.pallas_call(
        functools.partial(_mlstm_body, chunk=chunk),
        grid=(batch, ML_HEADS),
        in_specs=[
            pl.BlockSpec((seq, ML_DQK), lambda b, h: (b, _C_MQ // ML_DQK + h)),
            pl.BlockSpec((None, None, ML_DQK, seq), lambda b, h: (b, h, 0, 0)),
            pl.BlockSpec((seq, ML_DV), lambda b, h: (b, _C_MV // ML_DV + h)),
            pl.BlockSpec((seq, ML_DV), lambda b, h: (b, _C_MO // ML_DV + h)),
            pl.BlockSpec((4, None, 1, seq), lambda b, h: (0, h, 0, b)),
            pl.BlockSpec((None, 1, seq, 4), lambda b, h: (b, h, 0, 0)),
            pl.BlockSpec((1, ML_DV), lambda b, h: (0, 0)),
        ],
        out_specs=pl.BlockSpec((seq, ML_DV), lambda b, h: (b, h)),
        out_shape=jax.ShapeDtypeStruct((m, ML_W), BF16),
        scratch_shapes=[
            pltpu.VMEM((seq, ML_DV + LANES), BF16),
            pltpu.VMEM((seq, ML_DV), F32),
            pltpu.VMEM((seq, ML_DV), F32),
        ],
        compiler_params=_cparams(("parallel", "parallel")),
        name="mlstm",
    )(proj, kt, proj, proj, grow, gcol, ng)


def _qkprep_body(q_ref, k_ref, cos_ref, sin_ref, gq_ref, gk_ref, gm_ref, oq_ref, ok_ref, *, qscale):
    cosf = cos_ref[...]
    sins = sin_ref[...]
    lane = lax.broadcasted_iota(jnp.int32, cosf.shape, 1)
    first_half = (lane % DA_DQK) < (DA_DQK // 2)
    gmat = gm_ref[...]

    def one(x_ref, g_ref, o_ref, scale):
        for j in range(x_ref.shape[1] // LANES):
            x = x_ref[:, j * LANES:(j + 1) * LANES].astype(F32)
            ms = jnp.dot((x * x).astype(BF16), gmat, preferred_element_type=F32)
            y = x * lax.rsqrt(ms + EPS) * g_ref[...]
            rot = jnp.where(first_half, pltpu.roll(y, LANES - DA_DQK // 2, 1), pltpu.roll(y, DA_DQK // 2, 1))
            o = y * cosf + rot * sins
            if scale != 1.0:
                o = o * scale
            o_ref[:, j * LANES:(j + 1) * LANES] = o.astype(o_ref.dtype)

    one(q_ref, gq_ref, oq_ref, qscale)
    one(k_ref, gk_ref, ok_ref, 1.0)


def _qkprep(proj, cosf, sins, gq, gk, gmat, seq, bm=512):
    m = proj.shape[0]
    nsb = seq // bm
    return pl.pallas_call(
        functools.partial(_qkprep_body, qscale=float(DA_DQK) ** -0.5 * LOG2E),
        grid=(m // bm,),
        in_specs=[
            pl.BlockSpec((bm, DA_W), lambda i: (i, _C_DQ // DA_W)),
            pl.BlockSpec((bm, DA_W), lambda i: (i, _C_DK // DA_W)),
            pl.BlockSpec((bm, LANES), lambda i: (i % nsb, 0)),
            pl.BlockSpec((bm, LANES), lambda i: (i % nsb, 0)),
            pl.BlockSpec((1, LANES), lambda i: (0, 0)),
            pl.BlockSpec((1, LANES), lambda i: (0, 0)),
            pl.BlockSpec((LANES, LANES), lambda i: (0, 0)),
        ],
        out_specs=[
            pl.BlockSpec((bm, DA_W), lambda i: (i, 0)),
            pl.BlockSpec((bm, DA_W), lambda i: (i, 0)),
        ],
        out_shape=[jax.ShapeDtypeStruct((m, DA_W), BF16)] * 2,
        compiler_params=_cparams(("parallel",)),
        name="qkprep",
    )(proj, proj, cosf, sins, gq, gk, gmat)


def _attn_body(qt_ref, k_ref, vt_ref, lam_ref, ng_ref, o_ref, *, kb, lam_init):
    qt = qt_ref[...]
    qb = qt.shape[1]
    s_len = k_ref.shape[0]
    row = lax.broadcasted_iota(jnp.int32, qt.shape, 0)
    zero = jnp.zeros_like(qt)
    q1 = jnp.where(row < DA_DQK, qt, zero)
    q2 = jnp.where(row >= DA_DQK, qt, zero)
    nv = vt_ref.shape[0]

    def upd(s, m_old, acc, vb):
        m_new = jnp.maximum(m_old, jnp.max(s, axis=0, keepdims=True))
        alpha = jnp.exp2(m_old - m_new)
        p = jnp.exp2(s - m_new).astype(BF16)
        return m_new, alpha * acc + jnp.dot(vb, p, preferred_element_type=F32)

    def body(j, carry):
        m1, a1, m2, a2 = carry
        r0 = pl.multiple_of(j * kb, kb)
        kblk = k_ref[pl.ds(r0, kb), :]
        vb = vt_ref[:, pl.ds(r0, kb)]
        m1, a1 = upd(jnp.dot(kblk, q1, preferred_element_type=F32), m1, a1, vb)
        m2, a2 = upd(jnp.dot(kblk, q2, preferred_element_type=F32), m2, a2, vb)
        return m1, a1, m2, a2

    minit = jnp.full((1, qb), -jnp.inf, F32)
    ainit = jnp.zeros((nv, qb), F32)
    _, a1, _, a2 = lax.fori_loop(0, s_len // kb, body, (minit, ainit, minit, ainit))

    lp = lam_ref[...]
    lam = (jnp.exp(jnp.sum(lp[0:1, :] * lp[1:2, :], axis=1, keepdims=True))
           - jnp.exp(jnp.sum(lp[2:3, :] * lp[3:4, :], axis=1, keepdims=True)) + lam_init)
    o = a1[:DA_DV, :] / a1[DA_DV:DA_DV + 1, :] - lam * (a2[:DA_DV, :] / a2[DA_DV:DA_DV + 1, :])
    ms = jnp.mean(o * o, axis=0, keepdims=True)
    on = o * lax.rsqrt(ms + EPS) * ng_ref[...] * (1.0 - lam_init)
    o_ref[...] = on.T.astype(o_ref.dtype)


def _attn(qt, kr, vt, lam_params, ng_col, batch, seq, lam_init, qb=AT_QB, kb=AT_KB):
    nq = seq // qb
    nv = vt.shape[2]
    return pl.pallas_call(
        functools.partial(_attn_body, kb=kb, lam_init=lam_init),
        grid=(batch, DA_HEADS, nq),
        in_specs=[
            pl.BlockSpec((None, None, DA_DV, qb), lambda b, h, i: (b, h, 0, i)),
            pl.BlockSpec((seq, DA_DV), lambda b, h, i: (b, h)),
            pl.BlockSpec((None, None, nv, seq), lambda b, h, i: (b, h, 0, 0)),
            pl.BlockSpec((4, DA_DQK), lambda b, h, i: (0, 0)),
            pl.BlockSpec((DA_DV, 1), lambda b, h, i: (0, 0)),
        ],
        out_specs=pl.BlockSpec((qb, DA_DV), lambda b, h, i: (b * nq + i, h)),
        out_shape=jax.ShapeDtypeStruct((batch * seq, DA_W), BF16),
        compiler_params=_cparams(("parallel", "parallel", "parallel")),
        name="diffattn",
    )(qt, kr, vt, lam_params, ng_col)


def _post_body(hm_ref, hd_ref, gm_ref, gd_ref, x_ref, wm_ref, wd_ref, wo_ref, g2_ref, wr_ref, br_ref,
               x1_ref, h2_ref, lg_ref):
    ym = jnp.dot(hm_ref[...], wm_ref[...], preferred_element_type=F32)
    yd = jnp.dot(hd_ref[...], wd_ref[...], preferred_element_type=F32)
    mix = (jax.nn.sigmoid(gm_ref[...].astype(F32)) * ym
           + jax.nn.sigmoid(gd_ref[...].astype(F32)) * yd)
    x1 = x_ref[...] + jnp.dot(mix.astype(BF16), wo_ref[...], preferred_element_type=F32)
    x1_ref[...] = x1
    ms = jnp.mean(x1 * x1, axis=-1, keepdims=True)
    h2 = x1 * lax.rsqrt(ms + EPS) * g2_ref[...]
    h2_ref[...] = h2
    lg_ref[...] = jnp.dot(h2, wr_ref[...], preferred_element_type=F32,
                          precision=lax.Precision.HIGHEST) + br_ref[...]


def _post(hm, hd, proj, x2, wm, wd, wo, g2, wr, br, bm=256):
    m, d = x2.shape
    const = lambda i: (0, 0)
    return pl.pallas_call(
        _post_body,
        grid=(m // bm,),
        in_specs=[
            pl.BlockSpec((bm, ML_W), lambda i: (i, 0)),
            pl.BlockSpec((bm, DA_W), lambda i: (i, 0)),
            pl.BlockSpec((bm, d), lambda i: (i, _C_GM // D_MODEL)),
            pl.BlockSpec((bm, d), lambda i: (i, _C_GD // D_MODEL)),
            pl.BlockSpec((bm, d), lambda i: (i, 0)),
            pl.BlockSpec((ML_W, d), const, pipeline_mode=pl.Buffered(1)),
            pl.BlockSpec((DA_W, d), const, pipeline_mode=pl.Buffered(1)),
            pl.BlockSpec((d, d), const, pipeline_mode=pl.Buffered(1)),
            pl.BlockSpec((1, d), const),
            pl.BlockSpec((d, LANES), const),
            pl.BlockSpec((1, LANES), const),
        ],
        out_specs=[
            pl.BlockSpec((bm, d), lambda i: (i, 0)),
            pl.BlockSpec((bm, d), lambda i: (i, 0)),
            pl.BlockSpec((bm, LANES), lambda i: (i, 0)),
        ],
        out_shape=[
            jax.ShapeDtypeStruct((m, d), F32),
            jax.ShapeDtypeStruct((m, d), F32),
            jax.ShapeDtypeStruct((m, LANES), F32),
        ],
        compiler_params=_cparams(("parallel",)),
        name="postmix",
    )(hm, hd, proj, proj, x2, wm, wd, wo, g2, wr, br)


def _route_body(lg_ref, ids_ref, rank_ref, wts_ref, cnt_ref, carry_ref):
    @pl.when(pl.program_id(0) == 0)
    def _():
        carry_ref[...] = jnp.zeros_like(carry_ref)

    lg = lg_ref[...]
    bm = lg.shape[0]
    lane = lax.broadcasted_iota(jnp.int32, lg.shape, 1)
    lanef = lane.astype(F32)
    work = jnp.where(lane < N_EXPERTS, lg, -jnp.inf)
    vals, hots = [], []
    ids = jnp.zeros(lg.shape, F32)
    for k in range(TOP_K):
        mx = jnp.max(work, axis=1, keepdims=True)
        idx = jnp.min(jnp.where(work == mx, lanef, float(LANES)), axis=1, keepdims=True)
        hot = lanef == idx
        work = jnp.where(hot, -jnp.inf, work)
        vals.append(mx)
        hots.append(hot)
        ids = jnp.where(lane == k, idx, ids)
    exps = [jnp.exp(v - vals[0]) for v in vals]
    tot = exps[0] + exps[1] + exps[2] + exps[3]
    sel = jnp.zeros(lg.shape, F32)
    for hot in hots:
        sel = jnp.where(hot, 1.0, sel)
    r = lax.broadcasted_iota(jnp.int32, (bm, bm), 0)
    c = lax.broadcasted_iota(jnp.int32, (bm, bm), 1)
    strict = jnp.where(c < r, 1.0, 0.0).astype(BF16)
    cum = jnp.dot(strict, sel.astype(BF16), preferred_element_type=F32) + carry_ref[0:1, :]
    ranks = jnp.zeros(lg.shape, F32)
    wts = jnp.zeros(lg.shape, F32)
    for k in range(TOP_K):
        rk = jnp.sum(jnp.where(hots[k], cum, 0.0), axis=1, keepdims=True)
        ranks = jnp.where(lane == k, rk, ranks)
        wts = jnp.where(lane == k, exps[k] / tot, wts)
    newc = carry_ref[0:1, :] + jnp.sum(sel, axis=0, keepdims=True)
    carry_ref[...] = jnp.broadcast_to(newc, carry_ref.shape)
    ids_ref[...] = ids.astype(jnp.int32)
    rank_ref[...] = ranks.astype(jnp.int32)
    wts_ref[...] = wts
    cnt_ref[...] = jnp.broadcast_to(newc, cnt_ref.shape)


def _route(logits, bm=512):
    m = logits.shape[0]
    blk = pl.BlockSpec((bm, LANES), lambda i: (i, 0))
    return pl.pallas_call(
        _route_body,
        grid=(m // bm,),
        in_specs=[blk],
        out_specs=[blk, blk, blk, pl.BlockSpec((8, LANES), lambda i: (0, 0))],
        out_shape=[
            jax.ShapeDtypeStruct((m, LANES), jnp.int32),
            jax.ShapeDtypeStruct((m, LANES), jnp.int32),
            jax.ShapeDtypeStruct((m, LANES), F32),
            jax.ShapeDtypeStruct((8, LANES), F32),
        ],
        scratch_shapes=[pltpu.VMEM((8, LANES), F32)],
        compiler_params=_cparams(("arbitrary",)),
        name="route",
    )(logits)


def _dispatch_body(dest_ref, h2_ref, xs_in_ref, xs_ref, sem, *, bm):
    del xs_in_ref
    t0 = pl.program_id(0) * bm

    def copy(t, k):
        return pltpu.make_async_copy(h2_ref.at[pl.ds(t0 + t, 1), :],
                                     xs_ref.at[pl.ds(dest_ref[0, 0, t * TOP_K + k], 1), :], sem)

    def start(t, _):
        for k in range(TOP_K):
            copy(t, k).start()
        return 0

    def wait(t, _):
        for k in range(TOP_K):
            copy(t, k).wait()
        return 0

    lax.fori_loop(0, bm, start, 0)
    lax.fori_loop(0, bm, wait, 0)


def _dispatch(dest3, h2, xs0, bm):
    m = h2.shape[0]
    return pl.pallas_call(
        functools.partial(_dispatch_body, bm=bm),
        grid=(m // bm,),
        in_specs=[
            pl.BlockSpec((1, 1, bm * TOP_K), lambda i: (i, 0, 0), memory_space=pltpu.SMEM),
            pl.BlockSpec(memory_space=pl.ANY),
            pl.BlockSpec(memory_space=pl.ANY),
        ],
        out_specs=pl.BlockSpec(memory_space=pl.ANY),
        out_shape=jax.ShapeDtypeStruct(xs0.shape, xs0.dtype),
        scratch_shapes=[pltpu.SemaphoreType.DMA(())],
        input_output_aliases={2: 0},
        compiler_params=_cparams(("arbitrary",)),
        name="dispatch",
    )(dest3, h2, xs0)


def _expert_body(iexp_ref, istart_ref, insub_ref, itail_ref, xs_ref, wgu_ref, bgu_ref, wdn_ref, bdn_ref, pe_ref, po_ref,
                 ys_ref, xstage_ref, xb_ref, acc_ref, sem_in, sem_out):
    del iexp_ref
    i = pl.program_id(0)
    f = pl.program_id(1)
    nf = pl.num_programs(1)
    nsub = insub_ref[i]
    start = pl.multiple_of(istart_ref[i], MOE_SUB)

    @pl.when((f == 0) & (nsub > 0))
    def _():
        cp = pltpu.make_async_copy(xs_ref.at[pl.ds(start, MOE_RMAX), :], xstage_ref, sem_in)
        cp.start()
        cp.wait()
        xb_ref[...] = xstage_ref[...].astype(BF16)

    wg = wgu_ref[...].astype(BF16)
    wd = wdn_ref[...].astype(BF16)
    bgu = bgu_ref[...]

    def sub(sb, _):
        r0 = pl.multiple_of(sb * MOE_SUB, MOE_SUB)
        xsub = xb_ref[pl.ds(r0, MOE_SUB), :]
        gu = (jnp.dot(xsub, wg, preferred_element_type=F32) + bgu).astype(BF16)
        glu = jnp.minimum(jnp.dot(gu, pe_ref[...], preferred_element_type=F32), SWIGLU_LIMIT)
        lin = jnp.clip(jnp.dot(gu, po_ref[...], preferred_element_type=F32), -SWIGLU_LIMIT, SWIGLU_LIMIT)
        act = glu * jax.nn.sigmoid(SWIGLU_ALPHA * glu) * (lin + 1.0)
        contrib = jnp.dot(act.astype(BF16), wd, preferred_element_type=F32)

        @pl.when(f == 0)
        def _():
            acc_ref[pl.ds(r0, MOE_SUB), :] = contrib

        @pl.when(f > 0)
        def _():
            acc_ref[pl.ds(r0, MOE_SUB), :] = acc_ref[pl.ds(r0, MOE_SUB), :] + contrib

        return 0

    lax.fori_loop(0, nsub, sub, 0)

    @pl.when(f == nf - 1)
    def _():
        def out_copy(sb):
            r0 = pl.multiple_of(sb * MOE_SUB, MOE_SUB)
            return pltpu.make_async_copy(acc_ref.at[pl.ds(r0, MOE_SUB), :],
                                         ys_ref.at[pl.ds(start + r0, MOE_SUB), :], sem_out)

        def fin(sb, _):
            r0 = pl.multiple_of(sb * MOE_SUB, MOE_SUB)
            acc_ref[pl.ds(r0, MOE_SUB), :] = acc_ref[pl.ds(r0, MOE_SUB), :] + bdn_ref[...]
            out_copy(sb).start()
            return 0

        def fin_wait(sb, _):
            out_copy(sb).wait()
            return 0

        lax.fori_loop(0, nsub, fin, 0)
        lax.fori_loop(0, nsub, fin_wait, 0)

    @pl.when((i == pl.num_programs(0) - 1) & (f == nf - 1))
    def _():
        first = itail_ref[0] // MOE_SUB
        last = ys_ref.shape[0] // MOE_SUB
        acc_ref[0:MOE_SUB, :] = jnp.zeros((MOE_SUB, acc_ref.shape[1]), F32)

        def tail_copy(sb):
            r0 = pl.multiple_of(sb * MOE_SUB, MOE_SUB)
            return pltpu.make_async_copy(acc_ref.at[0:MOE_SUB, :], ys_ref.at[pl.ds(r0, MOE_SUB), :], sem_out)

        def tail_start(sb, _):
            tail_copy(sb).start()
            return 0

        def tail_wait(sb, _):
            tail_copy(sb).wait()
            return 0

        lax.fori_loop(first, last, tail_start, 0)
        lax.fori_loop(first, last, tail_wait, 0)


def _experts(item_exp, item_start, item_nsub, item_tail, xs, w_gu, b_gu, w_dn, b_dn, pe, po, n_rows):
    n_items = item_exp.shape[0]
    d = D_MODEL
    nf = D_EXPERT // MOE_FC
    grid_spec = pltpu.PrefetchScalarGridSpec(
        num_scalar_prefetch=4,
        grid=(n_items, nf),
        in_specs=[
            pl.BlockSpec(memory_space=pl.ANY),
            pl.BlockSpec((None, d, 2 * MOE_FC), lambda i, f, e, s, n, t: (e[i], 0, f)),
            pl.BlockSpec((None, 1, 2 * MOE_FC), lambda i, f, e, s, n, t: (e[i], 0, f)),
            pl.BlockSpec((None, MOE_FC, d), lambda i, f, e, s, n, t: (e[i], f, 0)),
            pl.BlockSpec((None, 1, d), lambda i, f, e, s, n, t: (e[i], 0, 0)),
            pl.BlockSpec((2 * MOE_FC, MOE_FC), lambda i, f, e, s, n, t: (0, 0)),
            pl.BlockSpec((2 * MOE_FC, MOE_FC), lambda i, f, e, s, n, t: (0, 0)),
        ],
        out_specs=pl.BlockSpec(memory_space=pl.ANY),
        scratch_shapes=[
            pltpu.VMEM((MOE_RMAX, d), F32),
            pltpu.VMEM((MOE_RMAX, d), BF16),
            pltpu.VMEM((MOE_RMAX, d), F32),
            pltpu.SemaphoreType.DMA(()),
            pltpu.SemaphoreType.DMA(()),
        ],
    )
    return pl.pallas_call(
        _expert_body,
        grid_spec=grid_spec,
        out_shape=jax.ShapeDtypeStruct((n_rows, d), F32),
        compiler_params=_cparams(("arbitrary", "arbitrary"), vmem=56 * 1024 * 1024),
        name="experts",
    )(item_exp, item_start, item_nsub, item_tail, xs, w_gu, b_gu, w_dn, b_dn, pe, po)


def _combine_body(dest_ref, ys_ref, x1_ref, wts_ref, o_ref, buf_ref, sem, *, bm):
    def copy(t, k):
        return pltpu.make_async_copy(ys_ref.at[pl.ds(dest_ref[0, 0, t * TOP_K + k], 1), :],
                                     buf_ref.at[k, pl.ds(t, 1), :], sem)

    def start(t, _):
        for k in range(TOP_K):
            copy(t, k).start()
        return 0

    def wait(t, _):
        for k in range(TOP_K):
            copy(t, k).wait()
        return 0

    lax.fori_loop(0, bm, start, 0)
    lax.fori_loop(0, bm, wait, 0)
    w = wts_ref[...]
    acc = x1_ref[...]
    for k in range(TOP_K):
        acc = acc + w[:, k:k + 1] * buf_ref[k]
    o_ref[...] = acc


def _combine(dest3, ys, x1, wts, bm):
    m, d = x1.shape
    return pl.pallas_call(
        functools.partial(_combine_body, bm=bm),
        grid=(m // bm,),
        in_specs=[
            pl.BlockSpec((1, 1, bm * TOP_K), lambda i: (i, 0, 0), memory_space=pltpu.SMEM),
            pl.BlockSpec(memory_space=pl.ANY),
            pl.BlockSpec((bm, d), lambda i: (i, 0)),
            pl.BlockSpec((bm, LANES), lambda i: (i, 0)),
        ],
        out_specs=pl.BlockSpec((bm, d), lambda i: (i, 0)),
        out_shape=jax.ShapeDtypeStruct((m, d), F32),
        scratch_shapes=[pltpu.VMEM((TOP_K, bm, d), F32), pltpu.SemaphoreType.DMA(())],
        compiler_params=_cparams(("arbitrary",)),
        name="combine",
    )(dest3, ys, x1, wts)


def _rope_tables(seq):
    half = DA_DQK // 2
    inv = ROPE_THETA ** (-jnp.arange(0, DA_DQK, 2, dtype=F32) / DA_DQK)
    ang = jnp.arange(seq, dtype=F32)[:, None] * inv[None, :]
    cos, sin = jnp.cos(ang), jnp.sin(ang)
    reps = LANES // DA_DQK
    cosf = jnp.tile(jnp.concatenate([cos, cos], axis=1), (1, reps))
    sins = jnp.tile(jnp.concatenate([-sin, sin], axis=1), (1, reps))
    del half
    return cosf, sins


def _moe_tables(ids, ranks, cnt_row):
    t = ids.shape[0]
    counts = cnt_row.astype(jnp.int32)
    nsb = (counts + MOE_SUB - 1) // MOE_SUB
    padded = nsb * MOE_SUB
    pad_end = jnp.cumsum(padded)
    pad_start = pad_end - padded
    dest = pad_start[ids] + ranks
    per_item = MOE_RMAX // MOE_SUB
    items_e = (nsb + per_item - 1) // per_item
    item_end = jnp.cumsum(items_e)
    n_items = (t * TOP_K // MOE_SUB + N_EXPERTS) // per_item + N_EXPERTS
    idx = jnp.arange(n_items, dtype=jnp.int32)
    e_of = jnp.minimum(jnp.searchsorted(item_end, idx, side="right"), N_EXPERTS - 1).astype(jnp.int32)
    local = idx - (item_end[e_of] - items_e[e_of])
    valid = idx < item_end[-1]
    nsub = jnp.where(valid, jnp.clip(nsb[e_of] - local * per_item, 0, per_item), 0).astype(jnp.int32)
    last_e = e_of[jnp.maximum(item_end[-1] - 1, 0)]
    item_exp = jnp.where(valid, e_of, last_e).astype(jnp.int32)
    item_start = jnp.where(valid, pad_start[e_of] + local * MOE_RMAX, 0).astype(jnp.int32)
    return dest.astype(jnp.int32), item_exp, item_start, nsub, pad_end[-1:].astype(jnp.int32)


def kernel(x, norm1_g, w_in, ml_gate_bias, ml_norm_g, w_ml_out, da_q_norm_g, da_k_norm_g, da_lambda, da_norm_g,
           w_da_out, w_o, norm2_g, w_router, b_router, w_gate_up, b_gate_up, w_down, b_down):
    batch, seq, d = x.shape
    depth = norm1_g.shape[0]
    tokens = batch * seq
    cosf, sins = _rope_tables(seq)
    gmat = jnp.kron(jnp.eye(LANES // DA_DQK, dtype=F32), jnp.full((DA_DQK, DA_DQK), 1.0 / DA_DQK, F32)).astype(BF16)
    lane_even = (jnp.arange(2 * MOE_FC)[:, None] == 2 * jnp.arange(MOE_FC)[None, :])
    lane_odd = (jnp.arange(2 * MOE_FC)[:, None] == 2 * jnp.arange(MOE_FC)[None, :] + 1)
    pe, po = lane_even.astype(BF16), lane_odd.astype(BF16)
    n_rows = tokens * TOP_K + N_EXPERTS * MOE_SUB
    bm_tok = 256

    x2 = x.reshape(tokens, d)
    for l in range(depth):
        lam_init = 0.8 - 0.6 * math.exp(-0.3 * l)
        w = w_in[l]
        w_main = jnp.concatenate([w[:, :_OFF_MG], w[:, _OFF_MG + _N_MG:]], axis=1).astype(BF16)
        w_gate = jnp.pad(w[:, _OFF_MG:_OFF_MG + _N_MG], ((0, 0), (0, LANES - _N_MG))).astype(BF16)
        gbias = jnp.pad(ml_gate_bias[l].reshape(1, _N_MG), ((0, 0), (0, LANES - _N_MG)))
        proj, gates = _inproj(x2, norm1_g[l].reshape(1, d), w_main, w_gate, gbias)

        gt = gates[:, :_N_MG].T
        gp = _gateprep(gt, ML_CHUNK)
        grow = gp.reshape(4, ML_HEADS, 1, tokens)
        gcol = gp.reshape(4, ML_HEADS, batch, seq).transpose(2, 1, 3, 0)
        kt = proj[:, _C_MK:_C_MK + ML_HEADS * ML_DQK].reshape(batch, seq, ML_HEADS, ML_DQK).transpose(0, 2, 3, 1)
        hm = _mlstm(proj, kt, grow, gcol, ml_norm_g[l].reshape(1, ML_DV), batch, seq, ML_CHUNK)

        gq = jnp.tile(da_q_norm_g[l], LANES // DA_DQK).reshape(1, LANES)
        gk = jnp.tile(da_k_norm_g[l], LANES // DA_DQK).reshape(1, LANES)
        qr, kr = _qkprep(proj, cosf, sins, gq, gk, gmat, seq)
        qt = qr.reshape(batch, seq, DA_HEADS, DA_DV).transpose(0, 2, 3, 1)
        vt = proj[:, _C_DV:_C_DV + DA_W].reshape(batch, seq, DA_HEADS, DA_DV).transpose(0, 2, 3, 1)
        vt = jnp.concatenate([vt, jnp.ones((batch, DA_HEADS, 8, seq), BF16)], axis=2)
        hd = _attn(qt, kr, vt, da_lambda[l], da_norm_g[l].reshape(DA_DV, 1), batch, seq, lam_init)

        wr = jnp.pad(w_router[l], ((0, 0), (0, LANES - N_EXPERTS)))
        br = jnp.pad(b_router[l].reshape(1, N_EXPERTS), ((0, 0), (0, LANES - N_EXPERTS)))
        x1, h2, logits = _post(hm, hd, proj, x2, w_ml_out[l].astype(BF16), w_da_out[l].astype(BF16),
                               w_o[l].astype(BF16), norm2_g[l].reshape(1, d), wr, br)

        ids, ranks, wts, cnt = _route(logits)
        dest, item_exp, item_start, item_nsub, item_tail = _moe_tables(ids[:, :TOP_K], ranks[:, :TOP_K], cnt[0, :N_EXPERTS])
        dest3 = dest.reshape(tokens // bm_tok, 1, bm_tok * TOP_K)
        xs0 = jnp.zeros((n_rows + MOE_RMAX, d), F32)
        xs = _dispatch(dest3, h2, xs0, bm_tok)
        ys = _experts(item_exp, item_start, item_nsub, item_tail, xs, w_gate_up[l],
                      b_gate_up[l].reshape(N_EXPERTS, 1, 2 * D_EXPERT), w_down[l],
                      b_down[l].reshape(N_EXPERTS, 1, d), pe, po, n_rows)
        x2 = _combine(dest3, ys, x1, wts, bm_tok)
    return x2.reshape(batch, seq, d)
```

```python
import functools
import math

import jax
import jax.numpy as jnp
from jax import lax
from jax.experimental import pallas as pl
from jax.experimental.pallas import tpu as pltpu

F32 = jnp.float32
BF16 = jnp.bfloat16

D_MODEL = 2048
ML_HEADS = 4
ML_DQK = 128
ML_DV = 256
ML_W = ML_HEADS * ML_DV
DA_HEADS = 8
DA_DQK = 64
DA_DV = 2 * DA_DQK
DA_W = DA_HEADS * DA_DV
ROPE_THETA = 10000.0
N_EXPERTS = 32
TOP_K = 4
D_EXPERT = D_MODEL
SWIGLU_LIMIT = 7.0
SWIGLU_ALPHA = 1.702
EPS = 1e-6
LOG2E = 1.4426950408889634

LANES = 128
VMEM_LIMIT = 48 * 1024 * 1024

_OFF_MG = 2 * ML_HEADS * ML_DQK + 2 * ML_W
_N_MG = 4 * ML_HEADS
_C_MQ, _C_MK, _C_MV, _C_MO = 0, 512, 1024, 2048
_C_DQ, _C_DK, _C_DV, _C_GM, _C_GD = 3072, 4096, 5120, 6144, 8192
_N_MAIN = 10240

ML_CHUNK = 256
AT_QB = 256
AT_KB = 256
MOE_SUB = 256
MOE_RMAX = 1024
MOE_FC = 256


def _cparams(sem, vmem=VMEM_LIMIT):
    return pltpu.CompilerParams(dimension_semantics=sem, vmem_limit_bytes=vmem)


def _inproj_body(x_ref, g_ref, w_ref, wg_ref, gb_ref, o_ref, og_ref, xn_ref):
    @pl.when(pl.program_id(1) == 0)
    def _():
        x = x_ref[...]
        ms = jnp.mean(x * x, axis=-1, keepdims=True)
        xn = (x * lax.rsqrt(ms + EPS) * g_ref[...]).astype(BF16)
        xn_ref[...] = xn
        og_ref[...] = jnp.dot(xn, wg_ref[...], preferred_element_type=F32) + gb_ref[...]

    o_ref[...] = jnp.dot(xn_ref[...], w_ref[...], preferred_element_type=F32).astype(o_ref.dtype)


def _inproj(x2, g1, w_main, w_gate, gate_bias, bm=1024, bn=1024):
    m, d = x2.shape
    n = w_main.shape[1]
    return pl.pallas_call(
        _inproj_body,
        grid=(m // bm, n // bn),
        in_specs=[
            pl.BlockSpec((bm, d), lambda i, j: (i, 0)),
            pl.BlockSpec((1, d), lambda i, j: (0, 0)),
            pl.BlockSpec((d, bn), lambda i, j: (0, j)),
            pl.BlockSpec((d, LANES), lambda i, j: (0, 0)),
            pl.BlockSpec((1, LANES), lambda i, j: (0, 0)),
        ],
        out_specs=[
            pl.BlockSpec((bm, bn), lambda i, j: (i, j)),
            pl.BlockSpec((bm, LANES), lambda i, j: (i, 0)),
        ],
        out_shape=[
            jax.ShapeDtypeStruct((m, n), BF16),
            jax.ShapeDtypeStruct((m, LANES), F32),
        ],
        scratch_shapes=[pltpu.VMEM((bm, d), BF16)],
        compiler_params=_cparams(("parallel", "arbitrary")),
        name="inproj",
    )(x2, g1, w_main, w_gate, gate_bias)


def _gateprep_body(g_ref, o_ref):
    x = g_ref[...]
    c = x.shape[1]
    lf = jnp.minimum(x, 0.0) - jnp.log1p(jnp.exp(-jnp.abs(x)))
    r = lax.broadcasted_iota(jnp.int32, (c, c), 0)
    s = lax.broadcasted_iota(jnp.int32, (c, c), 1)
    upper = (r <= s).astype(F32)
    lower = (r >= s).astype(F32)
    pre = jnp.dot(lf, upper, preferred_element_type=F32, precision=lax.Precision.HIGHEST)
    suf = jnp.dot(lf, lower, preferred_element_type=F32, precision=lax.Precision.HIGHEST)
    row = lax.broadcasted_iota(jnp.int32, x.shape, 0)
    h = ML_HEADS
    out = jnp.where((row >= h) & (row < 2 * h), pre, x)
    out = jnp.where(row >= 3 * h, suf, out)
    o_ref[...] = out


def _gateprep(gt, chunk):
    r, n = gt.shape
    return pl.pallas_call(
        _gateprep_body,
        grid=(n // chunk,),
        in_specs=[pl.BlockSpec((r, chunk), lambda i: (0, i))],
        out_specs=pl.BlockSpec((r, chunk), lambda i: (0, i)),
        out_shape=jax.ShapeDtypeStruct((r, n), F32),
        compiler_params=_cparams(("parallel",)),
        name="gateprep",
    )(gt)


def _mlstm_body(q_ref, kt_ref, v_ref, mo_ref, grow_ref, gcol_ref, ng_ref, o_ref,
                vaug_ref, hf_ref, hb_ref, *, chunk):
    s_len = q_ref.shape[0]
    nc = s_len // chunk
    dv = ML_DV
    wide = dv + LANES
    inv_scale = float(ML_DQK) ** 0.5

    vaug_ref[:, :dv] = v_ref[...]
    lane = lax.broadcasted_iota(jnp.int32, (s_len, LANES), 1)
    vaug_ref[:, dv:] = jnp.where(lane == 0, 1.0, 0.0).astype(BF16)

    rr = lax.broadcasted_iota(jnp.int32, (chunk, chunk), 0)
    cc = lax.broadcasted_iota(jnp.int32, (chunk, chunk), 1)

    def chunk_step(c, state, m_prev, reverse):
        r0 = pl.multiple_of(c * chunk, chunk)
        gi, gb = (2, 3) if reverse else (0, 1)
        qc = q_ref[pl.ds(r0, chunk), :]
        ktc = kt_ref[:, pl.ds(r0, chunk)]
        vac = vaug_ref[pl.ds(r0, chunk), :]
        i_row = grow_ref[gi, :, pl.ds(r0, chunk)]
        b_row = grow_ref[gb, :, pl.ds(r0, chunk)]
        gcol = gcol_ref[0, pl.ds(r0, chunk), :]
        i_col = gcol[:, gi:gi + 1]
        b_col = gcol[:, gb:gb + 1]
        mask = (cc >= rr) if reverse else (cc <= rr)
        log_d = jnp.where(mask, b_col - (b_row - i_row), -jnp.inf)
        inter = b_col + m_prev
        m_t = jnp.maximum(inter, jnp.max(log_d, axis=1, keepdims=True))
        s_inter = jnp.exp(inter - m_t)
        dm = jnp.exp(log_d - m_t)
        sqk = jnp.dot(qc, ktc, preferred_element_type=F32)
        p = (dm * sqk).astype(BF16)
        nd = (jnp.dot(p, vac, preferred_element_type=F32)
              + s_inter * jnp.dot(qc, state.astype(BF16), preferred_element_type=F32))
        den = nd[:, dv:dv + 1]
        floor = jnp.exp(-m_t) * inv_scale
        h = nd[:, :dv] / jnp.maximum(jnp.abs(den), floor)
        g = b_col[0:1, :] if reverse else b_col[chunk - 1:chunk, :]
        a_col = g - b_col + i_col
        m_new = jnp.maximum(g + m_prev, jnp.max(a_col, axis=0, keepdims=True))
        decay = jnp.exp(g + m_prev - m_new)
        w_col = jnp.exp(a_col - m_new)
        wv = (w_col * vac.astype(F32)).astype(BF16)
        new_state = decay * state + jnp.dot(ktc, wv, preferred_element_type=F32)
        return r0, h, new_state, m_new

    def body(c, carry):
        sf, mf, sb, mb = carry
        r0, h, sf, mf = chunk_step(c, sf, mf, False)
        hf_ref[pl.ds(r0, chunk), :] = h
        r1, h2, sb, mb = chunk_step(nc - 1 - c, sb, mb, True)
        hb_ref[pl.ds(r1, chunk), :] = h2
        return sf, mf, sb, mb

    z = jnp.zeros((ML_DQK, wide), F32)
    m0 = jnp.zeros((1, 1), F32)
    lax.fori_loop(0, nc, body, (z, m0, z, m0))

    def epilogue(c, _):
        r0 = pl.multiple_of(c * chunk, chunk)
        h = hf_ref[pl.ds(r0, chunk), :] + hb_ref[pl.ds(r0, chunk), :]
        ms = jnp.mean(h * h, axis=-1, keepdims=True)
        hn = h * lax.rsqrt(ms + EPS) * ng_ref[...]
        gate = jax.nn.sigmoid(mo_ref[pl.ds(r0, chunk), :].astype(F32))
        o_ref[pl.ds(r0, chunk), :] = (hn * gate).astype(o_ref.dtype)
        return 0

    lax.fori_loop(0, nc, epilogue, 0)


def _mlstm(proj, kt, grow, gcol, ng, batch, seq, chunk):
    m = proj.shape[0]
    nsb = 1
    del nsb
    return pl.pallas_call(
        functools.partial(_mlstm_body, chunk=chunk),
        grid=(batch, ML_HEADS),
        in_specs=[
            pl.BlockSpec((seq, ML_DQK), lambda b, h: (b, _C_MQ // ML_DQK + h)),
            pl.BlockSpec((None, None, ML_DQK, seq), lambda b, h: (b, h, 0, 0)),
            pl.BlockSpec((seq, ML_DV), lambda b, h: (b, _C_MV // ML_DV + h)),
            pl.BlockSpec((seq, ML_DV), lambda b, h: (b, _C_MO // ML_DV + h)),
            pl.BlockSpec((4, None, 1, seq), lambda b, h: (0, h, 0, b)),
            pl.BlockSpec((None, 1, seq, 4), lambda b, h: (b, h, 0, 0)),
            pl.BlockSpec((1, ML_DV), lambda b, h: (0, 0)),
        ],
        out_specs=pl.BlockSpec((seq, ML_DV), lambda b, h: (b, h)),
        out_shape=jax.ShapeDtypeStruct((m, ML_W), BF16),
        scratch_shapes=[
            pltpu.VMEM((seq, ML_DV + LANES), BF16),
            pltpu.VMEM((seq, ML_DV), F32),
            pltpu.VMEM((seq, ML_DV), F32),
        ],
        compiler_params=_cparams(("parallel", "parallel")),
        name="mlstm",
    )(proj, kt, proj, proj, grow, gcol, ng)


def _qkprep_body(q_ref, k_ref, cos_ref, sin_ref, gq_ref, gk_ref, gm_ref, oq_ref, ok_ref, *, qscale):
    cosf = cos_ref[...]
    sins = sin_ref[...]
    lane = lax.broadcasted_iota(jnp.int32, cosf.shape, 1)
    first_half = (lane % DA_DQK) < (DA_DQK // 2)
    gmat = gm_ref[...]

    def one(x_ref, g_ref, o_ref, scale):
        for j in range(x_ref.shape[1] // LANES):
            x = x_ref[:, j * LANES:(j + 1) * LANES].astype(F32)
            ms = jnp.dot((x * x).astype(BF16), gmat, preferred_element_type=F32)
            y = x * lax.rsqrt(ms + EPS) * g_ref[...]
            rot = jnp.where(first_half, pltpu.roll(y, LANES - DA_DQK // 2, 1), pltpu.roll(y, DA_DQK // 2, 1))
            o = y * cosf + rot * sins
            if scale != 1.0:
                o = o * scale
            o_ref[:, j * LANES:(j + 1) * LANES] = o.astype(o_ref.dtype)

    one(q_ref, gq_ref, oq_ref, qscale)
    one(k_ref, gk_ref, ok_ref, 1.0)


def _qkprep(proj, cosf, sins, gq, gk, gmat, seq, bm=512):
    m = proj.shape[0]
    nsb = seq // bm
    return pl.pallas_call(
        functools.partial(_qkprep_body, qscale=float(DA_DQK) ** -0.5 * LOG2E),
        grid=(m // bm,),
        in_specs=[
            pl.BlockSpec((bm, DA_W), lambda i: (i, _C_DQ // DA_W)),
            pl.BlockSpec((bm, DA_W), lambda i: (i, _C_DK // DA_W)),
            pl.BlockSpec((bm, LANES), lambda i: (i % nsb, 0)),
            pl.BlockSpec((bm, LANES), lambda i: (i % nsb, 0)),
            pl.BlockSpec((1, LANES), lambda i: (0, 0)),
            pl.BlockSpec((1, LANES), lambda i: (0, 0)),
            pl.BlockSpec((LANES, LANES), lambda i: (0, 0)),
        ],
        out_specs=[
            pl.BlockSpec((bm, DA_W), lambda i: (i, 0)),
            pl.BlockSpec((bm, DA_W), lambda i: (i, 0)),
        ],
        out_shape=[jax.ShapeDtypeStruct((m, DA_W), BF16)] * 2,
        compiler_params=_cparams(("parallel",)),
        name="qkprep",
    )(proj, proj, cosf, sins, gq, gk, gmat)


def _attn_body(qt_ref, k_ref, vt_ref, lam_ref, ng_ref, o_ref, *, kb, lam_init):
    qt = qt_ref[...]
    qb = qt.shape[1]
    s_len = k_ref.shape[0]
    row = lax.broadcasted_iota(jnp.int32, qt.shape, 0)
    zero = jnp.zeros_like(qt)
    q1 = jnp.where(row < DA_DQK, qt, zero)
    q2 = jnp.where(row >= DA_DQK, qt, zero)
    nv = vt_ref.shape[0]

    def upd(s, m_old, acc, vb):
        m_new = jnp.maximum(m_old, jnp.max(s, axis=0, keepdims=True))
        alpha = jnp.exp2(m_old - m_new)
        p = jnp.exp2(s - m_new).astype(BF16)
        return m_new, alpha * acc + jnp.dot(vb, p, preferred_element_type=F32)

    def scores(j):
        kblk = k_ref[j * kb:(j + 1) * kb, :]
        return (jnp.dot(kblk, q1, preferred_element_type=F32),
                jnp.dot(kblk, q2, preferred_element_type=F32))

    m1 = m2 = jnp.full((1, qb), -jnp.inf, F32)
    a1 = a2 = jnp.zeros((nv, qb), F32)
    nkb = s_len // kb
    s_cur = scores(0)
    for j in range(nkb):
        s_nxt = scores(j + 1) if j + 1 < nkb else None
        vb = vt_ref[:, j * kb:(j + 1) * kb]
        m1, a1 = upd(s_cur[0], m1, a1, vb)
        m2, a2 = upd(s_cur[1], m2, a2, vb)
        s_cur = s_nxt

    lp = lam_ref[...]
    lam = (jnp.exp(jnp.sum(lp[0:1, :] * lp[1:2, :], axis=1, keepdims=True))
           - jnp.exp(jnp.sum(lp[2:3, :] * lp[3:4, :], axis=1, keepdims=True)) + lam_init)
    o = a1[:DA_DV, :] / a1[DA_DV:DA_DV + 1, :] - lam * (a2[:DA_DV, :] / a2[DA_DV:DA_DV + 1, :])
    ms = jnp.mean(o * o, axis=0, keepdims=True)
    on = o * lax.rsqrt(ms + EPS) * ng_ref[...] * (1.0 - lam_init)
    o_ref[...] = on.T.astype(o_ref.dtype)


def _attn(qt, kr, vt, lam_params, ng_col, batch, seq, lam_init, qb=AT_QB, kb=AT_KB):
    nq = seq // qb
    nv = vt.shape[2]
    return pl.pallas_call(
        functools.partial(_attn_body, kb=kb, lam_init=lam_init),
        grid=(batch, DA_HEADS, nq),
        in_specs=[
            pl.BlockSpec((None, None, DA_DV, qb), lambda b, h, i: (b, h, 0, i)),
            pl.BlockSpec((seq, DA_DV), lambda b, h, i: (b, h)),
            pl.BlockSpec((None, None, nv, seq), lambda b, h, i: (b, h, 0, 0)),
            pl.BlockSpec((4, DA_DQK), lambda b, h, i: (0, 0)),
            pl.BlockSpec((DA_DV, 1), lambda b, h, i: (0, 0)),
        ],
        out_specs=pl.BlockSpec((qb, DA_DV), lambda b, h, i: (b * nq + i, h)),
        out_shape=jax.ShapeDtypeStruct((batch * seq, DA_W), BF16),
        compiler_params=_cparams(("parallel", "parallel", "parallel")),
        name="diffattn",
    )(qt, kr, vt, lam_params, ng_col)


def _post_body(hm_ref, hd_ref, gm_ref, gd_ref, x_ref, wm_ref, wd_ref, wo_ref, g2_ref, wr_ref, br_ref,
               x1_ref, h2_ref, lg_ref):
    ym = jnp.dot(hm_ref[...], wm_ref[...], preferred_element_type=F32)
    yd = jnp.dot(hd_ref[...], wd_ref[...], preferred_element_type=F32)
    mix = (jax.nn.sigmoid(gm_ref[...].astype(F32)) * ym
           + jax.nn.sigmoid(gd_ref[...].astype(F32)) * yd)
    x1 = x_ref[...] + jnp.dot(mix.astype(BF16), wo_ref[...], preferred_element_type=F32)
    x1_ref[...] = x1
    ms = jnp.mean(x1 * x1, axis=-1, keepdims=True)
    h2 = x1 * lax.rsqrt(ms + EPS) * g2_ref[...]
    h2_ref[...] = h2
    lg_ref[...] = jnp.dot(h2, wr_ref[...], preferred_element_type=F32,
                          precision=lax.Precision.HIGHEST) + br_ref[...]


def _post(hm, hd, proj, x2, wm, wd, wo, g2, wr, br, bm=256):
    m, d = x2.shape
    const = lambda i: (0, 0)
    return pl.pallas_call(
        _post_body,
        grid=(m // bm,),
        in_specs=[
            pl.BlockSpec((bm, ML_W), lambda i: (i, 0)),
            pl.BlockSpec((bm, DA_W), lambda i: (i, 0)),
            pl.BlockSpec((bm, d), lambda i: (i, _C_GM // D_MODEL)),
            pl.BlockSpec((bm, d), lambda i: (i, _C_GD // D_MODEL)),
            pl.BlockSpec((bm, d), lambda i: (i, 0)),
            pl.BlockSpec((ML_W, d), const, pipeline_mode=pl.Buffered(1)),
            pl.BlockSpec((DA_W, d), const, pipeline_mode=pl.Buffered(1)),
            pl.BlockSpec((d, d), const, pipeline_mode=pl.Buffered(1)),
            pl.BlockSpec((1, d), const),
            pl.BlockSpec((d, LANES), const),
            pl.BlockSpec((1, LANES), const),
        ],
        out_specs=[
            pl.BlockSpec((bm, d), lambda i: (i, 0)),
            pl.BlockSpec((bm, d), lambda i: (i, 0)),
            pl.BlockSpec((bm, LANES), lambda i: (i, 0)),
        ],
        out_shape=[
            jax.ShapeDtypeStruct((m, d), F32),
            jax.ShapeDtypeStruct((m, d), F32),
            jax.ShapeDtypeStruct((m, LANES), F32),
        ],
        compiler_params=_cparams(("parallel",)),
        name="postmix",
    )(hm, hd, proj, proj, x2, wm, wd, wo, g2, wr, br)


def _route_body(lg_ref, ids_ref, rank_ref, wts_ref, cnt_ref, carry_ref):
    @pl.when(pl.program_id(0) == 0)
    def _():
        carry_ref[...] = jnp.zeros_like(carry_ref)

    lg = lg_ref[...]
    bm = lg.shape[0]
    lane = lax.broadcasted_iota(jnp.int32, lg.shape, 1)
    lanef = lane.astype(F32)
    work = jnp.where(lane < N_EXPERTS, lg, -jnp.inf)
    vals, hots = [], []
    ids = jnp.zeros(lg.shape, F32)
    for k in range(TOP_K):
        mx = jnp.max(work, axis=1, keepdims=True)
        idx = jnp.min(jnp.where(work == mx, lanef, float(LANES)), axis=1, keepdims=True)
        hot = lanef == idx
        work = jnp.where(hot, -jnp.inf, work)
        vals.append(mx)
        hots.append(hot)
        ids = jnp.where(lane == k, idx, ids)
    exps = [jnp.exp(v - vals[0]) for v in vals]
    tot = exps[0] + exps[1] + exps[2] + exps[3]
    sel = jnp.zeros(lg.shape, F32)
    for hot in hots:
        sel = jnp.where(hot, 1.0, sel)
    r = lax.broadcasted_iota(jnp.int32, (bm, bm), 0)
    c = lax.broadcasted_iota(jnp.int32, (bm, bm), 1)
    strict = jnp.where(c < r, 1.0, 0.0).astype(BF16)
    cum = jnp.dot(strict, sel.astype(BF16), preferred_element_type=F32) + carry_ref[0:1, :]
    ranks = jnp.zeros(lg.shape, F32)
    wts = jnp.zeros(lg.shape, F32)
    for k in range(TOP_K):
        rk = jnp.sum(jnp.where(hots[k], cum, 0.0), axis=1, keepdims=True)
        ranks = jnp.where(lane == k, rk, ranks)
        wts = jnp.where(lane == k, exps[k] / tot, wts)
    newc = carry_ref[0:1, :] + jnp.sum(sel, axis=0, keepdims=True)
    carry_ref[...] = jnp.broadcast_to(newc, carry_ref.shape)
    ids_ref[...] = ids.astype(jnp.int32)
    rank_ref[...] = ranks.astype(jnp.int32)
    wts_ref[...] = wts
    cnt_ref[...] = jnp.broadcast_to(newc, cnt_ref.shape)


def _route(logits, bm=512):
    m = logits.shape[0]
    blk = pl.BlockSpec((bm, LANES), lambda i: (i, 0))
    return pl.pallas_call(
        _route_body,
        grid=(m // bm,),
        in_specs=[blk],
        out_specs=[blk, blk, blk, pl.BlockSpec((8, LANES), lambda i: (0, 0))],
        out_shape=[
            jax.ShapeDtypeStruct((m, LANES), jnp.int32),
            jax.ShapeDtypeStruct((m, LANES), jnp.int32),
            jax.ShapeDtypeStruct((m, LANES), F32),
            jax.ShapeDtypeStruct((8, LANES), F32),
        ],
        scratch_shapes=[pltpu.VMEM((8, LANES), F32)],
        compiler_params=_cparams(("arbitrary",)),
        name="route",
    )(logits)


def _dispatch_body(dest_ref, h2_ref, xs_in_ref, xs_ref, sem, *, bm):
    del xs_in_ref

    def copy(t, k):
        return pltpu.make_async_copy(h2_ref.at[pl.ds(t, 1), :],
                                     xs_ref.at[pl.ds(dest_ref[0, 0, t * TOP_K + k], 1), :], sem)

    def start(t, _):
        for k in range(TOP_K):
            copy(t, k).start()
        return 0

    def wait(t, _):
        for k in range(TOP_K):
            copy(t, k).wait()
        return 0

    lax.fori_loop(0, bm, start, 0)
    lax.fori_loop(0, bm, wait, 0)


def _dispatch(dest3, h2, xs0, bm):
    m = h2.shape[0]
    return pl.pallas_call(
        functools.partial(_dispatch_body, bm=bm),
        grid=(m // bm,),
        in_specs=[
            pl.BlockSpec((1, 1, bm * TOP_K), lambda i: (i, 0, 0), memory_space=pltpu.SMEM),
            pl.BlockSpec((bm, h2.shape[1]), lambda i: (i, 0)),
            pl.BlockSpec(memory_space=pl.ANY),
        ],
        out_specs=pl.BlockSpec(memory_space=pl.ANY),
        out_shape=jax.ShapeDtypeStruct(xs0.shape, xs0.dtype),
        scratch_shapes=[pltpu.SemaphoreType.DMA(())],
        input_output_aliases={2: 0},
        compiler_params=_cparams(("arbitrary",)),
        name="dispatch",
    )(dest3, h2, xs0)


def _expert_body(iexp_ref, istart_ref, insub_ref, itail_ref, xs_ref, wgu_ref, bgu_ref, wdn_ref, bdn_ref, pe_ref, po_ref,
                 ys_ref, xstage_ref, xb_ref, acc_ref, sem_in, sem_out):
    del iexp_ref
    i = pl.program_id(0)
    f = pl.program_id(1)
    nf = pl.num_programs(1)
    nsub = insub_ref[i]
    start = pl.multiple_of(istart_ref[i], MOE_SUB)

    @pl.when((f == 0) & (nsub > 0))
    def _():
        cp = pltpu.make_async_copy(xs_ref.at[pl.ds(start, MOE_RMAX), :], xstage_ref, sem_in)
        cp.start()
        cp.wait()
        xb_ref[...] = xstage_ref[...].astype(BF16)

    wg = wgu_ref[...].astype(BF16)
    wd = wdn_ref[...].astype(BF16)
    bgu = bgu_ref[...]

    def sub(sb, _):
        r0 = pl.multiple_of(sb * MOE_SUB, MOE_SUB)
        xsub = xb_ref[pl.ds(r0, MOE_SUB), :]
        gu = (jnp.dot(xsub, wg, preferred_element_type=F32) + bgu).astype(BF16)
        glu = jnp.minimum(jnp.dot(gu, pe_ref[...], preferred_element_type=F32), SWIGLU_LIMIT)
        lin = jnp.clip(jnp.dot(gu, po_ref[...], preferred_element_type=F32), -SWIGLU_LIMIT, SWIGLU_LIMIT)
        act = glu * jax.nn.sigmoid(SWIGLU_ALPHA * glu) * (lin + 1.0)
        contrib = jnp.dot(act.astype(BF16), wd, preferred_element_type=F32)

        @pl.when(f == 0)
        def _():
            acc_ref[pl.ds(r0, MOE_SUB), :] = contrib

        @pl.when(f > 0)
        def _():
            acc_ref[pl.ds(r0, MOE_SUB), :] = acc_ref[pl.ds(r0, MOE_SUB), :] + contrib

        return 0

    lax.fori_loop(0, nsub, sub, 0)

    @pl.when(f == nf - 1)
    def _():
        def out_copy(sb):
            r0 = pl.multiple_of(sb * MOE_SUB, MOE_SUB)
            return pltpu.make_async_copy(acc_ref.at[pl.ds(r0, MOE_SUB), :],
                                         ys_ref.at[pl.ds(start + r0, MOE_SUB), :], sem_out)

        def fin(sb, _):
            r0 = pl.multiple_of(sb * MOE_SUB, MOE_SUB)
            acc_ref[pl.ds(r0, MOE_SUB), :] = acc_ref[pl.ds(r0, MOE_SUB), :] + bdn_ref[...]
            out_copy(sb).start()
            return 0

        def fin_wait(sb, _):
            out_copy(sb).wait()
            return 0

        lax.fori_loop(0, nsub, fin, 0)
        lax.fori_loop(0, nsub, fin_wait, 0)

    @pl.when((i == pl.num_programs(0) - 1) & (f == nf - 1))
    def _():
        first = itail_ref[0] // MOE_SUB
        last = ys_ref.shape[0] // MOE_SUB
        acc_ref[0:MOE_SUB, :] = jnp.zeros((MOE_SUB, acc_ref.shape[1]), F32)

        def tail_copy(sb):
            r0 = pl.multiple_of(sb * MOE_SUB, MOE_SUB)
            return pltpu.make_async_copy(acc_ref.at[0:MOE_SUB, :], ys_ref.at[pl.ds(r0, MOE_SUB), :], sem_out)

        def tail_start(sb, _):
            tail_copy(sb).start()
            return 0

        def tail_wait(sb, _):
            tail_copy(sb).wait()
            return 0

        lax.fori_loop(first, last, tail_start, 0)
        lax.fori_loop(first, last, tail_wait, 0)


def _experts(item_exp, item_start, item_nsub, item_tail, xs, w_gu, b_gu, w_dn, b_dn, pe, po, n_rows):
    n_items = item_exp.shape[0]
    d = D_MODEL
    nf = D_EXPERT // MOE_FC
    grid_spec = pltpu.PrefetchScalarGridSpec(
        num_scalar_prefetch=4,
        grid=(n_items, nf),
        in_specs=[
            pl.BlockSpec(memory_space=pl.ANY),
            pl.BlockSpec((None, d, 2 * MOE_FC), lambda i, f, e, s, n, t: (e[i], 0, f)),
            pl.BlockSpec((None, 1, 2 * MOE_FC), lambda i, f, e, s, n, t: (e[i], 0, f)),
            pl.BlockSpec((None, MOE_FC, d), lambda i, f, e, s, n, t: (e[i], f, 0)),
            pl.BlockSpec((None, 1, d), lambda i, f, e, s, n, t: (e[i], 0, 0)),
            pl.BlockSpec((2 * MOE_FC, MOE_FC), lambda i, f, e, s, n, t: (0, 0)),
            pl.BlockSpec((2 * MOE_FC, MOE_FC), lambda i, f, e, s, n, t: (0, 0)),
        ],
        out_specs=pl.BlockSpec(memory_space=pl.ANY),
        scratch_shapes=[
            pltpu.VMEM((MOE_RMAX, d), F32),
            pltpu.VMEM((MOE_RMAX, d), BF16),
            pltpu.VMEM((MOE_RMAX, d), F32),
            pltpu.SemaphoreType.DMA(()),
            pltpu.SemaphoreType.DMA(()),
        ],
    )
    return pl.pallas_call(
        _expert_body,
        grid_spec=grid_spec,
        out_shape=jax.ShapeDtypeStruct((n_rows, d), F32),
        compiler_params=_cparams(("arbitrary", "arbitrary"), vmem=56 * 1024 * 1024),
        name="experts",
    )(item_exp, item_start, item_nsub, item_tail, xs, w_gu, b_gu, w_dn, b_dn, pe, po)


def _combine_body(dest_ref, ys_ref, x1_ref, wts_ref, o_ref, buf_ref, sem, *, bm):
    def copy(t, k):
        return pltpu.make_async_copy(ys_ref.at[pl.ds(dest_ref[0, 0, t * TOP_K + k], 1), :],
                                     buf_ref.at[k, pl.ds(t, 1), :], sem)

    def start(t, _):
        for k in range(TOP_K):
            copy(t, k).start()
        return 0

    def wait(t, _):
        for k in range(TOP_K):
            copy(t, k).wait()
        return 0

    lax.fori_loop(0, bm, start, 0)
    lax.fori_loop(0, bm, wait, 0)
    w = wts_ref[...]
    acc = x1_ref[...]
    for k in range(TOP_K):
        acc = acc + w[:, k:k + 1] * buf_ref[k]
    o_ref[...] = acc


def _combine(dest3, ys, x1, wts, bm):
    m, d = x1.shape
    return pl.pallas_call(
        functools.partial(_combine_body, bm=bm),
        grid=(m // bm,),
        in_specs=[
            pl.BlockSpec((1, 1, bm * TOP_K), lambda i: (i, 0, 0), memory_space=pltpu.SMEM),
            pl.BlockSpec(memory_space=pl.ANY),
            pl.BlockSpec((bm, d), lambda i: (i, 0)),
            pl.BlockSpec((bm, LANES), lambda i: (i, 0)),
        ],
        out_specs=pl.BlockSpec((bm, d), lambda i: (i, 0)),
        out_shape=jax.ShapeDtypeStruct((m, d), F32),
        scratch_shapes=[pltpu.VMEM((TOP_K, bm, d), F32), pltpu.SemaphoreType.DMA(())],
        compiler_params=_cparams(("arbitrary",)),
        name="combine",
    )(dest3, ys, x1, wts)


def _rope_tables(seq):
    half = DA_DQK // 2
    inv = ROPE_THETA ** (-jnp.arange(0, DA_DQK, 2, dtype=F32) / DA_DQK)
    ang = jnp.arange(seq, dtype=F32)[:, None] * inv[None, :]
    cos, sin = jnp.cos(ang), jnp.sin(ang)
    reps = LANES // DA_DQK
    cosf = jnp.tile(jnp.concatenate([cos, cos], axis=1), (1, reps))
    sins = jnp.tile(jnp.concatenate([-sin, sin], axis=1), (1, reps))
    del half
    return cosf, sins


def _moe_tables(ids, ranks, cnt_row):
    t = ids.shape[0]
    counts = cnt_row.astype(jnp.int32)
    nsb = (counts + MOE_SUB - 1) // MOE_SUB
    padded = nsb * MOE_SUB
    pad_end = jnp.cumsum(padded)
    pad_start = pad_end - padded
    dest = pad_start[ids] + ranks
    per_item = MOE_RMAX // MOE_SUB
    items_e = (nsb + per_item - 1) // per_item
    item_end = jnp.cumsum(items_e)
    n_items = (t * TOP_K // MOE_SUB + N_EXPERTS) // per_item + N_EXPERTS
    idx = jnp.arange(n_items, dtype=jnp.int32)
    e_of = jnp.minimum(jnp.searchsorted(item_end, idx, side="right"), N_EXPERTS - 1).astype(jnp.int32)
    local = idx - (item_end[e_of] - items_e[e_of])
    valid = idx < item_end[-1]
    nsub = jnp.where(valid, jnp.clip(nsb[e_of] - local * per_item, 0, per_item), 0).astype(jnp.int32)
    last_e = e_of[jnp.maximum(item_end[-1] - 1, 0)]
    item_exp = jnp.where(valid, e_of, last_e).astype(jnp.int32)
    item_start = jnp.where(valid, pad_start[e_of] + local * MOE_RMAX, 0).astype(jnp.int32)
    return dest.astype(jnp.int32), item_exp, item_start, nsub, pad_end[-1:].astype(jnp.int32)


def kernel(x, norm1_g, w_in, ml_gate_bias, ml_norm_g, w_ml_out, da_q_norm_g, da_k_norm_g, da_lambda, da_norm_g,
           w_da_out, w_o, norm2_g, w_router, b_router, w_gate_up, b_gate_up, w_down, b_down):
    batch, seq, d = x.shape
    depth = norm1_g.shape[0]
    tokens = batch * seq
    cosf, sins = _rope_tables(seq)
    gmat = jnp.kron(jnp.eye(LANES // DA_DQK, dtype=F32), jnp.full((DA_DQK, DA_DQK), 1.0 / DA_DQK, F32)).astype(BF16)
    lane_even = (jnp.arange(2 * MOE_FC)[:, None] == 2 * jnp.arange(MOE_FC)[None, :])
    lane_odd = (jnp.arange(2 * MOE_FC)[:, None] == 2 * jnp.arange(MOE_FC)[None, :] + 1)
    pe, po = lane_even.astype(BF16), lane_odd.astype(BF16)
    n_rows = tokens * TOP_K + N_EXPERTS * MOE_SUB
    bm_tok = 256

    x2 = x.reshape(tokens, d)
    for l in range(depth):
        lam_init = 0.8 - 0.6 * math.exp(-0.3 * l)
        w = w_in[l]
        w_main = jnp.concatenate([w[:, :_OFF_MG], w[:, _OFF_MG + _N_MG:]], axis=1).astype(BF16)
        w_gate = jnp.pad(w[:, _OFF_MG:_OFF_MG + _N_MG], ((0, 0), (0, LANES - _N_MG))).astype(BF16)
        gbias = jnp.pad(ml_gate_bias[l].reshape(1, _N_MG), ((0, 0), (0, LANES - _N_MG)))
        proj, gates = _inproj(x2, norm1_g[l].reshape(1, d), w_main, w_gate, gbias)

        gt = gates[:, :_N_MG].T
        gp = _gateprep(gt, ML_CHUNK)
        grow = gp.reshape(4, ML_HEADS, 1, tokens)
        gcol = gp.reshape(4, ML_HEADS, batch, seq).transpose(2, 1, 3, 0)
        kt = proj[:, _C_MK:_C_MK + ML_HEADS * ML_DQK].reshape(batch, seq, ML_HEADS, ML_DQK).transpose(0, 2, 3, 1)
        hm = _mlstm(proj, kt, grow, gcol, ml_norm_g[l].reshape(1, ML_DV), batch, seq, ML_CHUNK)

        gq = jnp.tile(da_q_norm_g[l], LANES // DA_DQK).reshape(1, LANES)
        gk = jnp.tile(da_k_norm_g[l], LANES // DA_DQK).reshape(1, LANES)
        qr, kr = _qkprep(proj, cosf, sins, gq, gk, gmat, seq)
        qt = qr.reshape(batch, seq, DA_HEADS, DA_DV).transpose(0, 2, 3, 1)
        vt = proj[:, _C_DV:_C_DV + DA_W].reshape(batch, seq, DA_HEADS, DA_DV).transpose(0, 2, 3, 1)
        vt = jnp.concatenate([vt, jnp.ones((batch, DA_HEADS, 8, seq), BF16)], axis=2)
        hd = _attn(qt, kr, vt, da_lambda[l], da_norm_g[l].reshape(DA_DV, 1), batch, seq, lam_init)

        wr = jnp.pad(w_router[l], ((0, 0), (0, LANES - N_EXPERTS)))
        br = jnp.pad(b_router[l].reshape(1, N_EXPERTS), ((0, 0), (0, LANES - N_EXPERTS)))
        x1, h2, logits = _post(hm, hd, proj, x2, w_ml_out[l].astype(BF16), w_da_out[l].astype(BF16),
                               w_o[l].astype(BF16), norm2_g[l].reshape(1, d), wr, br)

        ids, ranks, wts, cnt = _route(logits)
        dest, item_exp, item_start, item_nsub, item_tail = _moe_tables(ids[:, :TOP_K], ranks[:, :TOP_K], cnt[0, :N_EXPERTS])
        dest3 = dest.reshape(tokens // bm_tok, 1, bm_tok * TOP_K)
        xs0 = jnp.zeros((n_rows + MOE_RMAX, d), F32)
        xs = _dispatch(dest3, h2, xs0, bm_tok)
        ys = _experts(item_exp, item_start, item_nsub, item_tail, xs, w_gate_up[l],
                      b_gate_up[l].reshape(N_EXPERTS, 1, 2 * D_EXPERT), w_down[l],
                      b_down[l].reshape(N_EXPERTS, 1, d), pe, po, n_rows)
        x2 = _combine(dest3, ys, x1, wts, bm_tok)
    return x2.reshape(batch, seq, d)
```

```python
import functools
import math

import jax
import jax.numpy as jnp
from jax import lax
from jax.experimental import pallas as pl
from jax.experimental.pallas import tpu as pltpu

F32 = jnp.float32
BF16 = jnp.bfloat16

D_MODEL = 2048
ML_HEADS = 4
ML_DQK = 128
ML_DV = 256
ML_W = ML_HEADS * ML_DV
DA_HEADS = 8
DA_DQK = 64
DA_DV = 2 * DA_DQK
DA_W = DA_HEADS * DA_DV
ROPE_THETA = 10000.0
N_EXPERTS = 32
TOP_K = 4
D_EXPERT = D_MODEL
SWIGLU_LIMIT = 7.0
SWIGLU_ALPHA = 1.702
EPS = 1e-6
LOG2E = 1.4426950408889634

LANES = 128
VMEM_LIMIT = 48 * 1024 * 1024

_OFF_MG = 2 * ML_HEADS * ML_DQK + 2 * ML_W
_N_MG = 4 * ML_HEADS
_C_MQ, _C_MK, _C_MV, _C_MO = 0, 512, 1024, 2048
_C_DQ, _C_DK, _C_DV, _C_GM, _C_GD = 3072, 4096, 5120, 6144, 8192
_N_MAIN = 10240

ML_CHUNK = 256
AT_QB = 512
AT_KB = 256
MOE_SUB = 256
MOE_RMAX = 2048
MOE_BIG = 1024
MOE_FC = 256
MOE_NC = 512


def _cparams(sem, vmem=VMEM_LIMIT):
    return pltpu.CompilerParams(dimension_semantics=sem, vmem_limit_bytes=vmem)


def _pack_halves(x):
    n = x.shape[1] // 2
    lo = lax.bitcast_convert_type(x[:, :n].astype(BF16).astype(F32), jnp.uint32)
    hi = lax.bitcast_convert_type(x[:, n:].astype(BF16).astype(F32), jnp.uint32)
    return lax.bitcast_convert_type(lax.shift_right_logical(lo, jnp.uint32(16)) | hi, F32)


def _unpack_halves(w):
    u = lax.bitcast_convert_type(w, jnp.uint32)
    lo = lax.bitcast_convert_type(lax.shift_left(u, jnp.uint32(16)), F32)
    hi = lax.bitcast_convert_type(u & jnp.uint32(0xFFFF0000), F32)
    return lo, hi


def _inproj_body(x_ref, g_ref, w_ref, wg_ref, gb_ref, o_ref, og_ref, xn_ref):
    @pl.when(pl.program_id(1) == 0)
    def _():
        x = x_ref[...]
        ms = jnp.mean(x * x, axis=-1, keepdims=True)
        xn = (x * lax.rsqrt(ms + EPS) * g_ref[...]).astype(BF16)
        xn_ref[...] = xn
        og_ref[...] = jnp.dot(xn, wg_ref[...], preferred_element_type=F32) + gb_ref[...]

    o_ref[...] = jnp.dot(xn_ref[...], w_ref[...], preferred_element_type=F32).astype(o_ref.dtype)


def _inproj(x2, g1, w_main, w_gate, gate_bias, bm=1024, bn=1024):
    m, d = x2.shape
    n = w_main.shape[1]
    return pl.pallas_call(
        _inproj_body,
        grid=(m // bm, n // bn),
        in_specs=[
            pl.BlockSpec((bm, d), lambda i, j: (i, 0)),
            pl.BlockSpec((1, d), lambda i, j: (0, 0)),
            pl.BlockSpec((d, bn), lambda i, j: (0, j)),
            pl.BlockSpec((d, LANES), lambda i, j: (0, 0)),
            pl.BlockSpec((1, LANES), lambda i, j: (0, 0)),
        ],
        out_specs=[
            pl.BlockSpec((bm, bn), lambda i, j: (i, j)),
            pl.BlockSpec((bm, LANES), lambda i, j: (i, 0)),
        ],
        out_shape=[
            jax.ShapeDtypeStruct((m, n), BF16),
            jax.ShapeDtypeStruct((m, LANES), F32),
        ],
        scratch_shapes=[pltpu.VMEM((bm, d), BF16)],
        compiler_params=_cparams(("parallel", "arbitrary")),
        name="inproj",
    )(x2, g1, w_main, w_gate, gate_bias)


def _gateprep_body(g_ref, o_ref):
    x = g_ref[...]
    c = x.shape[1]
    lf = jnp.minimum(x, 0.0) - jnp.log1p(jnp.exp(-jnp.abs(x)))
    r = lax.broadcasted_iota(jnp.int32, (c, c), 0)
    s = lax.broadcasted_iota(jnp.int32, (c, c), 1)
    upper = (r <= s).astype(F32)
    lower = (r >= s).astype(F32)
    pre = jnp.dot(lf, upper, preferred_element_type=F32, precision=lax.Precision.HIGHEST)
    suf = jnp.dot(lf, lower, preferred_element_type=F32, precision=lax.Precision.HIGHEST)
    row = lax.broadcasted_iota(jnp.int32, x.shape, 0)
    h = ML_HEADS
    out = jnp.where((row >= h) & (row < 2 * h), pre, x)
    out = jnp.where(row >= 3 * h, suf, out)
    o_ref[...] = out


def _gateprep(gt, chunk):
    r, n = gt.shape
    return pl.pallas_call(
        _gateprep_body,
        grid=(n // chunk,),
        in_specs=[pl.BlockSpec((r, chunk), lambda i: (0, i))],
        out_specs=pl.BlockSpec((r, chunk), lambda i: (0, i)),
        out_shape=jax.ShapeDtypeStruct((r, n), F32),
        compiler_params=_cparams(("parallel",)),
        name="gateprep",
    )(gt)


def _mlstm_body(q_ref, kt_ref, v_ref, mo_ref, grow_ref, gcol_ref, ng_ref, o_ref,
                vaug_ref, hf_ref, hb_ref, *, chunk):
    s_len = q_ref.shape[0]
    nc = s_len // chunk
    dv = ML_DV
    wide = dv + LANES
    inv_scale = float(ML_DQK) ** 0.5

    vaug_ref[:, :dv] = v_ref[...]
    lane = lax.broadcasted_iota(jnp.int32, (s_len, LANES), 1)
    vaug_ref[:, dv:] = jnp.where(lane == 0, 1.0, 0.0).astype(BF16)

    rr = lax.broadcasted_iota(jnp.int32, (chunk, chunk), 0)
    cc = lax.broadcasted_iota(jnp.int32, (chunk, chunk), 1)

    def chunk_step(c, state, m_prev, reverse):
        r0 = pl.multiple_of(c * chunk, chunk)
        gi, gb = (2, 3) if reverse else (0, 1)
        qc = q_ref[pl.ds(r0, chunk), :]
        ktc = kt_ref[:, pl.ds(r0, chunk)]
        vac = vaug_ref[pl.ds(r0, chunk), :]
        i_row = grow_ref[gi, :, pl.ds(r0, chunk)]
        b_row = grow_ref[gb, :, pl.ds(r0, chunk)]
        gcol = gcol_ref[0, pl.ds(r0, chunk), :]
        i_col = gcol[:, gi:gi + 1]
        b_col = gcol[:, gb:gb + 1]
        mask = (cc >= rr) if reverse else (cc <= rr)
        log_d = jnp.where(mask, b_col - (b_row - i_row), -jnp.inf)
        inter = b_col + m_prev
        m_t = jnp.maximum(inter, jnp.max(log_d, axis=1, keepdims=True))
        s_inter = jnp.exp(inter - m_t)
        dm = jnp.exp(log_d - m_t)
        sqk = jnp.dot(qc, ktc, preferred_element_type=F32)
        p = (dm * sqk).astype(BF16)
        nd = (jnp.dot(p, vac, preferred_element_type=F32)
              + s_inter * jnp.dot(qc, state.astype(BF16), preferred_element_type=F32))
        den = nd[:, dv:dv + 1]
        floor = jnp.exp(-m_t) * inv_scale
        h = nd[:, :dv] / jnp.maximum(jnp.abs(den), floor)
        g = b_col[0:1, :] if reverse else b_col[chunk - 1:chunk, :]
        a_col = g - b_col + i_col
        m_new = jnp.maximum(g + m_prev, jnp.max(a_col, axis=0, keepdims=True))
        decay = jnp.exp(g + m_prev - m_new)
        w_col = jnp.exp(a_col - m_new)
        wv = (w_col * vac.astype(F32)).astype(BF16)
        new_state = decay * state + jnp.dot(ktc, wv, preferred_element_type=F32)
        return r0, h, new_state, m_new

    def body(c, carry):
        sf, mf, sb, mb = carry
        r0, h, sf, mf = chunk_step(c, sf, mf, False)
        hf_ref[pl.ds(r0, chunk), :] = h
        r1, h2, sb, mb = chunk_step(nc - 1 - c, sb, mb, True)
        hb_ref[pl.ds(r1, chunk), :] = h2
        return sf, mf, sb, mb

    z = jnp.zeros((ML_DQK, wide), F32)
    m0 = jnp.zeros((1, 1), F32)
    lax.fori_loop(0, nc, body, (z, m0, z, m0))

    def epilogue(c, _):
        r0 = pl.multiple_of(c * chunk, chunk)
        h = hf_ref[pl.ds(r0, chunk), :] + hb_ref[pl.ds(r0, chunk), :]
        ms = jnp.mean(h * h, axis=-1, keepdims=True)
        hn = h * lax.rsqrt(ms + EPS) * ng_ref[...]
        gate = jax.nn.sigmoid(mo_ref[pl.ds(r0, chunk), :].astype(F32))
        o_ref[pl.ds(r0, chunk), :] = (hn * gate).astype(o_ref.dtype)
        return 0

    lax.fori_loop(0, nc, epilogue, 0)


def _mlstm(proj, kt, grow, gcol, ng, batch, seq, chunk):
    m = proj.shape[0]
    nsb = 1
    del nsb
    return pl.pallas_call(
        functools.partial(_mlstm_body, chunk=chunk),
        grid=(batch, ML_HEADS),
        in_specs=[
            pl.BlockSpec((seq, ML_DQK), lambda b, h: (b, _C_MQ // ML_DQK + h)),
            pl.BlockSpec((None, None, ML_DQK, seq), lambda b, h: (b, h, 0, 0)),
            pl.BlockSpec((seq, ML_DV), lambda b, h: (b, _C_MV // ML_DV + h)),
            pl.BlockSpec((seq, ML_DV), lambda b, h: (b, _C_MO // ML_DV + h)),
            pl.BlockSpec((4, None, 1, seq), lambda b, h: (0, h, 0, b)),
            pl.BlockSpec((None, 1, seq, 4), lambda b, h: (b, h, 0, 0)),
            pl.BlockSpec((1, ML_DV), lambda b, h: (0, 0)),
        ],
        out_specs=pl.BlockSpec((seq, ML_DV), lambda b, h: (b, h)),
        out_shape=jax.ShapeDtypeStruct((m, ML_W), BF16),
        scratch_shapes=[
            pltpu.VMEM((seq, ML_DV + LANES), BF16),
            pltpu.VMEM((seq, ML_DV), F32),
            pltpu.VMEM((seq, ML_DV), F32),
        ],
        compiler_params=_cparams(("parallel", "parallel")),
        name="mlstm",
    )(proj, kt, proj, proj, grow, gcol, ng)


def _qkprep_body(q_ref, k_ref, cos_ref, sin_ref, gq_ref, gk_ref, gm_ref, oq_ref, ok_ref, *, qscale):
    cosf = cos_ref[...]
    sins = sin_ref[...]
    lane = lax.broadcasted_iota(jnp.int32, cosf.shape, 1)
    first_half = (lane % DA_DQK) < (DA_DQK // 2)
    gmat = gm_ref[...]

    def one(x_ref, g_ref, o_ref, scale):
        for j in range(x_ref.shape[1] // LANES):
            x = x_ref[:, j * LANES:(j + 1) * LANES].astype(F32)
            ms = jnp.dot((x * x).astype(BF16), gmat, preferred_element_type=F32)
            y = x * lax.rsqrt(ms + EPS) * g_ref[...]
            rot = jnp.where(first_half, pltpu.roll(y, LANES - DA_DQK // 2, 1), pltpu.roll(y, DA_DQK // 2, 1))
            o = y * cosf + rot * sins
            if scale != 1.0:
                o = o * scale
            o_ref[:, j * LANES:(j + 1) * LANES] = o.astype(o_ref.dtype)

    one(q_ref, gq_ref, oq_ref, qscale)
    one(k_ref, gk_ref, ok_ref, 1.0)


def _qkprep(proj, cosf, sins, gq, gk, gmat, seq, bm=512):
    m = proj.shape[0]
    nsb = seq // bm
    return pl.pallas_call(
        functools.partial(_qkprep_body, qscale=float(DA_DQK) ** -0.5 * LOG2E),
        grid=(m // bm,),
        in_specs=[
            pl.BlockSpec((bm, DA_W), lambda i: (i, _C_DQ // DA_W)),
            pl.BlockSpec((bm, DA_W), lambda i: (i, _C_DK // DA_W)),
            pl.BlockSpec((bm, LANES), lambda i: (i % nsb, 0)),
            pl.BlockSpec((bm, LANES), lambda i: (i % nsb, 0)),
            pl.BlockSpec((1, LANES), lambda i: (0, 0)),
            pl.BlockSpec((1, LANES), lambda i: (0, 0)),
            pl.BlockSpec((LANES, LANES), lambda i: (0, 0)),
        ],
        out_specs=[
            pl.BlockSpec((bm, DA_W), lambda i: (i, 0)),
            pl.BlockSpec((bm, DA_W), lambda i: (i, 0)),
        ],
        out_shape=[jax.ShapeDtypeStruct((m, DA_W), BF16)] * 2,
        compiler_params=_cparams(("parallel",)),
        name="qkprep",
    )(proj, proj, cosf, sins, gq, gk, gmat)


def _attn_body(qt_ref, k_ref, vt_ref, lam_ref, ng_ref, o_ref, *, kb, lam_init):
    qt = qt_ref[...]
    qb = qt.shape[1]
    s_len = k_ref.shape[0]
    row = lax.broadcasted_iota(jnp.int32, qt.shape, 0)
    zero = jnp.zeros_like(qt)
    q1 = jnp.where(row < DA_DQK, qt, zero)
    q2 = jnp.where(row >= DA_DQK, qt, zero)
    nv = vt_ref.shape[0]

    def upd(s, m_old, acc, vb):
        m_new = jnp.maximum(m_old, jnp.max(s, axis=0, keepdims=True))
        alpha = jnp.exp2(m_old - m_new)
        p = jnp.exp2(s - m_new).astype(BF16)
        return m_new, alpha * acc + jnp.dot(vb, p, preferred_element_type=F32)

    def scores(j):
        kblk = k_ref[j * kb:(j + 1) * kb, :]
        return (jnp.dot(kblk, q1, preferred_element_type=F32),
                jnp.dot(kblk, q2, preferred_element_type=F32))

    m1 = m2 = jnp.full((1, qb), -jnp.inf, F32)
    a1 = a2 = jnp.zeros((nv, qb), F32)
    nkb = s_len // kb
    s_cur = scores(0)
    for j in range(nkb):
        s_nxt = scores(j + 1) if j + 1 < nkb else None
        vb = vt_ref[:, j * kb:(j + 1) * kb]
        m1, a1 = upd(s_cur[0], m1, a1, vb)
        m2, a2 = upd(s_cur[1], m2, a2, vb)
        s_cur = s_nxt

    lp = lam_ref[...]
    lam = (jnp.exp(jnp.sum(lp[0:1, :] * lp[1:2, :], axis=1, keepdims=True))
           - jnp.exp(jnp.sum(lp[2:3, :] * lp[3:4, :], axis=1, keepdims=True)) + lam_init)
    o = a1[:DA_DV, :] / a1[DA_DV:DA_DV + 1, :] - lam * (a2[:DA_DV, :] / a2[DA_DV:DA_DV + 1, :])
    ms = jnp.mean(o * o, axis=0, keepdims=True)
    on = o * lax.rsqrt(ms + EPS) * ng_ref[...] * (1.0 - lam_init)
    o_ref[...] = on.T.astype(o_ref.dtype)


def _attn(qt, kr, vt, lam_params, ng_col, batch, seq, lam_init, qb=AT_QB, kb=AT_KB):
    nq = seq // qb
    nv = vt.shape[2]
    return pl.pallas_call(
        functools.partial(_attn_body, kb=kb, lam_init=lam_init),
        grid=(batch, DA_HEADS, nq),
        in_specs=[
            pl.BlockSpec((None, None, DA_DV, qb), lambda b, h, i: (b, h, 0, i)),
            pl.BlockSpec((seq, DA_DV), lambda b, h, i: (b, h)),
            pl.BlockSpec((None, None, nv, seq), lambda b, h, i: (b, h, 0, 0)),
            pl.BlockSpec((4, DA_DQK), lambda b, h, i: (0, 0)),
            pl.BlockSpec((DA_DV, 1), lambda b, h, i: (0, 0)),
        ],
        out_specs=pl.BlockSpec((qb, DA_DV), lambda b, h, i: (b * nq + i, h)),
        out_shape=jax.ShapeDtypeStruct((batch * seq, DA_W), BF16),
        compiler_params=_cparams(("parallel", "parallel", "parallel")),
        name="diffattn",
    )(qt, kr, vt, lam_params, ng_col)


def _post_body(hm_ref, hd_ref, gm_ref, gd_ref, x_ref, wm_ref, wd_ref, wo_ref, g2_ref, wrh_ref, wrl_ref, br_ref,
               x1_ref, h2_ref, lg_ref, *, tile):
    rows = [slice(t * tile, (t + 1) * tile) for t in range(x_ref.shape[0] // tile)]
    dot = functools.partial(jnp.dot, preferred_element_type=F32)
    ys = [(dot(hm_ref[r, :], wm_ref[...]), dot(hd_ref[r, :], wd_ref[...])) for r in rows]
    mixes = [(jax.nn.sigmoid(gm_ref[r, :].astype(F32)) * ym
              + jax.nn.sigmoid(gd_ref[r, :].astype(F32)) * yd).astype(BF16) for r, (ym, yd) in zip(rows, ys)]
    x1s = [x_ref[r, :] + dot(mix, wo_ref[...]) for r, mix in zip(rows, mixes)]
    for r, x1 in zip(rows, x1s):
        x1_ref[r, :] = x1
        ms = jnp.mean(x1 * x1, axis=-1, keepdims=True)
        h2 = x1 * lax.rsqrt(ms + EPS) * g2_ref[...]
        h2_ref[r, :] = _pack_halves(h2)
        hi = h2.astype(BF16)
        lo = (h2 - hi.astype(F32)).astype(BF16)
        lg_ref[r, :] = (dot(hi, wrh_ref[...]) + dot(hi, wrl_ref[...]) + dot(lo, wrh_ref[...])) + br_ref[...]


def _post(hm, hd, proj, x2, wm, wd, wo, g2, wr, br, bm=512, tile=256):
    m, d = x2.shape
    const = lambda i: (0, 0)
    wr_hi = wr.astype(BF16)
    wr_lo = (wr - wr_hi.astype(F32)).astype(BF16)
    return pl.pallas_call(
        functools.partial(_post_body, tile=tile),
        grid=(m // bm,),
        in_specs=[
            pl.BlockSpec((bm, ML_W), lambda i: (i, 0)),
            pl.BlockSpec((bm, DA_W), lambda i: (i, 0)),
            pl.BlockSpec((bm, d), lambda i: (i, _C_GM // D_MODEL)),
            pl.BlockSpec((bm, d), lambda i: (i, _C_GD // D_MODEL)),
            pl.BlockSpec((bm, d), lambda i: (i, 0)),
            pl.BlockSpec((ML_W, d), const, pipeline_mode=pl.Buffered(1)),
            pl.BlockSpec((DA_W, d), const, pipeline_mode=pl.Buffered(1)),
            pl.BlockSpec((d, d), const, pipeline_mode=pl.Buffered(1)),
            pl.BlockSpec((1, d), const),
            pl.BlockSpec((d, LANES), const),
            pl.BlockSpec((d, LANES), const),
            pl.BlockSpec((1, LANES), const),
        ],
        out_specs=[
            pl.BlockSpec((bm, d), lambda i: (i, 0)),
            pl.BlockSpec((bm, d // 2), lambda i: (i, 0)),
            pl.BlockSpec((bm, LANES), lambda i: (i, 0)),
        ],
        out_shape=[
            jax.ShapeDtypeStruct((m, d), F32),
            jax.ShapeDtypeStruct((m, d // 2), F32),
            jax.ShapeDtypeStruct((m, LANES), F32),
        ],
        compiler_params=_cparams(("parallel",), vmem=62 * 1024 * 1024),
        name="postmix",
    )(hm, hd, proj, proj, x2, wm, wd, wo, g2, wr_hi, wr_lo, br)


def _route_body(lg_ref, ids_ref, rank_ref, wts_ref, cnt_ref, carry_ref):
    @pl.when(pl.program_id(0) == 0)
    def _():
        carry_ref[...] = jnp.zeros_like(carry_ref)

    lg = lg_ref[...]
    bm = lg.shape[0]
    lane = lax.broadcasted_iota(jnp.int32, lg.shape, 1)
    lanef = lane.astype(F32)
    work = jnp.where(lane < N_EXPERTS, lg, -jnp.inf)
    vals, hots = [], []
    ids = jnp.zeros(lg.shape, F32)
    for k in range(TOP_K):
        mx = jnp.max(work, axis=1, keepdims=True)
        idx = jnp.min(jnp.where(work == mx, lanef, float(LANES)), axis=1, keepdims=True)
        hot = lanef == idx
        work = jnp.where(hot, -jnp.inf, work)
        vals.append(mx)
        hots.append(hot)
        ids = jnp.where(lane == k, idx, ids)
    exps = [jnp.exp(v - vals[0]) for v in vals]
    tot = exps[0] + exps[1] + exps[2] + exps[3]
    sel = jnp.zeros(lg.shape, F32)
    for hot in hots:
        sel = jnp.where(hot, 1.0, sel)
    r = lax.broadcasted_iota(jnp.int32, (bm, bm), 0)
    c = lax.broadcasted_iota(jnp.int32, (bm, bm), 1)
    strict = jnp.where(c < r, 1.0, 0.0).astype(BF16)
    cum = jnp.dot(strict, sel.astype(BF16), preferred_element_type=F32) + carry_ref[0:1, :]
    ranks = jnp.zeros(lg.shape, F32)
    wts = jnp.zeros(lg.shape, F32)
    for k in range(TOP_K):
        rk = jnp.sum(jnp.where(hots[k], cum, 0.0), axis=1, keepdims=True)
        ranks = jnp.where(lane == k, rk, ranks)
        wts = jnp.where(lane == k, exps[k] / tot, wts)
    newc = carry_ref[0:1, :] + jnp.sum(sel, axis=0, keepdims=True)
    carry_ref[...] = jnp.broadcast_to(newc, carry_ref.shape)
    ids_ref[...] = ids.astype(jnp.int32)
    rank_ref[...] = ranks.astype(jnp.int32)
    wts_ref[...] = wts
    cnt_ref[...] = jnp.broadcast_to(newc, cnt_ref.shape)


def _route(logits, bm=512):
    m = logits.shape[0]
    blk = pl.BlockSpec((bm, LANES), lambda i: (i, 0))
    return pl.pallas_call(
        _route_body,
        grid=(m // bm,),
        in_specs=[blk],
        out_specs=[blk, blk, blk, pl.BlockSpec((8, LANES), lambda i: (0, 0))],
        out_shape=[
            jax.ShapeDtypeStruct((m, LANES), jnp.int32),
            jax.ShapeDtypeStruct((m, LANES), jnp.int32),
            jax.ShapeDtypeStruct((m, LANES), F32),
            jax.ShapeDtypeStruct((8, LANES), F32),
        ],
        scratch_shapes=[pltpu.VMEM((8, LANES), F32)],
        compiler_params=_cparams(("arbitrary",)),
        name="route",
    )(logits)


def _dispatch_body(dest_ref, h2_ref, xs_in_ref, xs_ref, sem, *, bm):
    del xs_in_ref

    def copy(t, k):
        return pltpu.make_async_copy(h2_ref.at[pl.ds(t, 1), :],
                                     xs_ref.at[pl.ds(dest_ref[0, 0, t * TOP_K + k], 1), :], sem)

    def start(t, _):
        for k in range(TOP_K):
            copy(t, k).start(priority=k % 2)
        return 0

    def wait(t, _):
        for k in range(TOP_K):
            copy(t, k).wait()
        return 0

    lax.fori_loop(0, bm, start, 0)
    lax.fori_loop(0, bm, wait, 0)


def _dispatch(dest3, h2, xs0, bm):
    m = h2.shape[0]
    return pl.pallas_call(
        functools.partial(_dispatch_body, bm=bm),
        grid=(m // bm,),
        in_specs=[
            pl.BlockSpec((1, 1, bm * TOP_K), lambda i: (i, 0, 0), memory_space=pltpu.SMEM),
            pl.BlockSpec((bm, h2.shape[1]), lambda i: (i, 0)),
            pl.BlockSpec(memory_space=pl.ANY),
        ],
        out_specs=pl.BlockSpec(memory_space=pl.ANY),
        out_shape=jax.ShapeDtypeStruct(xs0.shape, xs0.dtype),
        scratch_shapes=[pltpu.SemaphoreType.DMA(())],
        input_output_aliases={2: 0},
        compiler_params=_cparams(("arbitrary",)),
        name="dispatch",
    )(dest3, h2, xs0)


def _expert_body(iexp_ref, istart_ref, insub_ref, itail_ref, xs_ref, wgu_ref, bgu_ref, wdn_ref, bdn_ref, pe_ref, po_ref,
                 ys_ref, xb_ref, acc_ref, wgb_ref, wdb_ref, sem_in, sem_out):
    del iexp_ref
    i = pl.program_id(0)
    f = pl.program_id(1)
    nf = pl.num_programs(1)
    nsub = insub_ref[i]
    start = pl.multiple_of(istart_ref[i], MOE_SUB)
    d = acc_ref.shape[1]
    half = d // 2

    def sub_rows(sb):
        return pl.ds(pl.multiple_of(sb * MOE_SUB, MOE_SUB), MOE_SUB)

    @pl.when((f == 0) & (nsub > 0))
    def _():
        def x_copy(sb):
            r0 = pl.multiple_of(sb * MOE_SUB, MOE_SUB)
            return pltpu.make_async_copy(xs_ref.at[pl.ds(start + r0, MOE_SUB), :],
                                         acc_ref.at[pl.ds(r0, MOE_SUB), 0:half], sem_in)

        def x_start(sb, _):
            x_copy(sb).start()
            return 0

        def x_wait(sb, _):
            x_copy(sb).wait()
            return 0

        def unpack(sb, _):
            lo, hi = _unpack_halves(acc_ref[sub_rows(sb), 0:half])
            xb_ref[sub_rows(sb), 0:half] = lo.astype(BF16)
            xb_ref[sub_rows(sb), half:d] = hi.astype(BF16)
            return 0

        def init(sb, _):
            acc_ref[sub_rows(sb), :] = jnp.broadcast_to(bdn_ref[...], (MOE_SUB, d))
            return 0

        lax.fori_loop(0, nsub, x_start, 0)
        lax.fori_loop(0, nsub, x_wait, 0)
        lax.fori_loop(0, nsub, unpack, 0)
        lax.fori_loop(0, nsub, init, 0)

    @pl.when(nsub > 0)
    def _():
        wgb_ref[...] = wgu_ref[...].astype(BF16)
        wdb_ref[...] = wdn_ref[...].astype(BF16)

        def block(r0, rows):
            gu = (jnp.dot(xb_ref[pl.ds(r0, rows), :], wgb_ref[...], preferred_element_type=F32)
                  + bgu_ref[...]).astype(BF16)
            glu = jnp.minimum(jnp.dot(gu, pe_ref[...], preferred_element_type=F32), SWIGLU_LIMIT)
            lin = jnp.clip(jnp.dot(gu, po_ref[...], preferred_element_type=F32), -SWIGLU_LIMIT, SWIGLU_LIMIT)
            act = (glu * jax.nn.sigmoid(SWIGLU_ALPHA * glu) * (lin + 1.0)).astype(BF16)
            for n0 in range(0, d, MOE_NC):
                acc_ref[pl.ds(r0, rows), n0:n0 + MOE_NC] += jnp.dot(
                    act, wdb_ref[:, n0:n0 + MOE_NC], preferred_element_type=F32)

        per_big = MOE_BIG // MOE_SUB
        nbig = lax.shift_right_logical(nsub, per_big.bit_length() - 1)

        def big(b, _):
            block(pl.multiple_of(b * MOE_BIG, MOE_BIG), MOE_BIG)
            return 0

        def small(sb, _):
            block(pl.multiple_of(sb * MOE_SUB, MOE_SUB), MOE_SUB)
            return 0

        lax.fori_loop(0, nbig, big, 0)
        lax.fori_loop(nbig * per_big, nsub, small, 0)

    @pl.when(f == nf - 1)
    def _():
        def out_copy(sb):
            r0 = pl.multiple_of(sb * MOE_SUB, MOE_SUB)
            return pltpu.make_async_copy(acc_ref.at[pl.ds(r0, MOE_SUB), :],
                                         ys_ref.at[pl.ds(start + r0, MOE_SUB), :], sem_out)

        def out_start(sb, _):
            out_copy(sb).start()
            return 0

        def out_wait(sb, _):
            out_copy(sb).wait()
            return 0

        lax.fori_loop(0, nsub, out_start, 0)
        lax.fori_loop(0, nsub, out_wait, 0)

    @pl.when((i == pl.num_programs(0) - 1) & (f == nf - 1))
    def _():
        first = lax.shift_right_logical(itail_ref[0], MOE_SUB.bit_length() - 1)
        last = ys_ref.shape[0] // MOE_SUB
        acc_ref[0:MOE_SUB, :] = jnp.zeros((MOE_SUB, d), F32)

        def tail_copy(sb):
            r0 = pl.multiple_of(sb * MOE_SUB, MOE_SUB)
            return pltpu.make_async_copy(acc_ref.at[0:MOE_SUB, :], ys_ref.at[pl.ds(r0, MOE_SUB), :], sem_out)

        def tail_start(sb, _):
            tail_copy(sb).start()
            return 0

        def tail_wait(sb, _):
            tail_copy(sb).wait()
            return 0

        lax.fori_loop(first, last, tail_start, 0)
        lax.fori_loop(first, last, tail_wait, 0)


def _experts(item_exp, item_start, item_nsub, item_tail, xs, w_gu, b_gu, w_dn, b_dn, pe, po, n_rows):
    n_items = item_exp.shape[0]
    d = D_MODEL
    nf = D_EXPERT // MOE_FC
    grid_spec = pltpu.PrefetchScalarGridSpec(
        num_scalar_prefetch=4,
        grid=(n_items, nf),
        in_specs=[
            pl.BlockSpec(memory_space=pl.ANY),
            pl.BlockSpec((None, d, 2 * MOE_FC), lambda i, f, e, s, n, t: (e[i], 0, f)),
            pl.BlockSpec((None, 1, 2 * MOE_FC), lambda i, f, e, s, n, t: (e[i], 0, f)),
            pl.BlockSpec((None, MOE_FC, d), lambda i, f, e, s, n, t: (e[i], f, 0)),
            pl.BlockSpec((None, 1, d), lambda i, f, e, s, n, t: (e[i], 0, 0)),
            pl.BlockSpec((2 * MOE_FC, MOE_FC), lambda i, f, e, s, n, t: (0, 0)),
            pl.BlockSpec((2 * MOE_FC, MOE_FC), lambda i, f, e, s, n, t: (0, 0)),
        ],
        out_specs=pl.BlockSpec(memory_space=pl.ANY),
        scratch_shapes=[
            pltpu.VMEM((MOE_RMAX, d), BF16),
            pltpu.VMEM((MOE_RMAX, d), F32),
            pltpu.VMEM((d, 2 * MOE_FC), BF16),
            pltpu.VMEM((MOE_FC, d), BF16),
            pltpu.SemaphoreType.DMA(()),
            pltpu.SemaphoreType.DMA(()),
        ],
    )
    return pl.pallas_call(
        _expert_body,
        grid_spec=grid_spec,
        out_shape=jax.ShapeDtypeStruct((n_rows, d), F32),
        compiler_params=_cparams(("arbitrary", "arbitrary"), vmem=58 * 1024 * 1024),
        name="experts",
    )(item_exp, item_start, item_nsub, item_tail, xs, w_gu, b_gu, w_dn, b_dn, pe, po)


def _combine_body(dest_ref, ys_ref, x1_ref, wts_ref, o_ref, buf_ref, sem, *, bm):
    def copy(t, k):
        return pltpu.make_async_copy(ys_ref.at[pl.ds(dest_ref[0, 0, t * TOP_K + k], 1), :],
                                     buf_ref.at[k, pl.ds(t, 1), :], sem)

    def start(t, _):
        for k in range(TOP_K):
            copy(t, k).start(priority=k % 2)
        return 0

    def wait(t, _):
        for k in range(TOP_K):
            copy(t, k).wait()
        return 0

    lax.fori_loop(0, bm, start, 0)
    lax.fori_loop(0, bm, wait, 0)
    w = wts_ref[...]
    acc = x1_ref[...]
    for k in range(TOP_K):
        acc = acc + w[:, k:k + 1] * buf_ref[k]
    o_ref[...] = acc


def _combine(dest3, ys, x1, wts, bm):
    m, d = x1.shape
    return pl.pallas_call(
        functools.partial(_combine_body, bm=bm),
        grid=(m // bm,),
        in_specs=[
            pl.BlockSpec((1, 1, bm * TOP_K), lambda i: (i, 0, 0), memory_space=pltpu.SMEM),
            pl.BlockSpec(memory_space=pl.ANY),
            pl.BlockSpec((bm, d), lambda i: (i, 0)),
            pl.BlockSpec((bm, LANES), lambda i: (i, 0)),
        ],
        out_specs=pl.BlockSpec((bm, d), lambda i: (i, 0)),
        out_shape=jax.ShapeDtypeStruct((m, d), F32),
        scratch_shapes=[pltpu.VMEM((TOP_K, bm, d), F32), pltpu.SemaphoreType.DMA(())],
        compiler_params=_cparams(("arbitrary",)),
        name="combine",
    )(dest3, ys, x1, wts)


def _rope_tables(seq):
    half = DA_DQK // 2
    inv = ROPE_THETA ** (-jnp.arange(0, DA_DQK, 2, dtype=F32) / DA_DQK)
    ang = jnp.arange(seq, dtype=F32)[:, None] * inv[None, :]
    cos, sin = jnp.cos(ang), jnp.sin(ang)
    reps = LANES // DA_DQK
    cosf = jnp.tile(jnp.concatenate([cos, cos], axis=1), (1, reps))
    sins = jnp.tile(jnp.concatenate([-sin, sin], axis=1), (1, reps))
    del half
    return cosf, sins


def _moe_tables(ids, ranks, cnt_row):
    t = ids.shape[0]
    counts = cnt_row.astype(jnp.int32)
    nsb = (counts + MOE_SUB - 1) // MOE_SUB
    padded = nsb * MOE_SUB
    pad_end = jnp.cumsum(padded)
    pad_start = pad_end - padded
    dest = pad_start[ids] + ranks
    per_item = MOE_RMAX // MOE_SUB
    items_e = (nsb + per_item - 1) // per_item
    item_end = jnp.cumsum(items_e)
    n_items = (t * TOP_K // MOE_SUB + N_EXPERTS) // per_item + N_EXPERTS
    idx = jnp.arange(n_items, dtype=jnp.int32)
    e_of = jnp.minimum(jnp.searchsorted(item_end, idx, side="right"), N_EXPERTS - 1).astype(jnp.int32)
    local = idx - (item_end[e_of] - items_e[e_of])
    valid = idx < item_end[-1]
    nsub = jnp.where(valid, jnp.clip(nsb[e_of] - local * per_item, 0, per_item), 0).astype(jnp.int32)
    last_e = e_of[jnp.maximum(item_end[-1] - 1, 0)]
    item_exp = jnp.where(valid, e_of, last_e).astype(jnp.int32)
    item_start = jnp.where(valid, pad_start[e_of] + local * MOE_RMAX, 0).astype(jnp.int32)
    return dest.astype(jnp.int32), item_exp, item_start, nsub, pad_end[-1:].astype(jnp.int32)


def kernel(x, norm1_g, w_in, ml_gate_bias, ml_norm_g, w_ml_out, da_q_norm_g, da_k_norm_g, da_lambda, da_norm_g,
           w_da_out, w_o, norm2_g, w_router, b_router, w_gate_up, b_gate_up, w_down, b_down):
    batch, seq, d = x.shape
    depth = norm1_g.shape[0]
    tokens = batch * seq
    cosf, sins = _rope_tables(seq)
    gmat = jnp.kron(jnp.eye(LANES // DA_DQK, dtype=F32), jnp.full((DA_DQK, DA_DQK), 1.0 / DA_DQK, F32)).astype(BF16)
    lane_even = (jnp.arange(2 * MOE_FC)[:, None] == 2 * jnp.arange(MOE_FC)[None, :])
    lane_odd = (jnp.arange(2 * MOE_FC)[:, None] == 2 * jnp.arange(MOE_FC)[None, :] + 1)
    pe, po = lane_even.astype(BF16), lane_odd.astype(BF16)
    n_rows = tokens * TOP_K + N_EXPERTS * MOE_SUB
    bm_tok = 256

    x2 = x.reshape(tokens, d)
    for l in range(depth):
        lam_init = 0.8 - 0.6 * math.exp(-0.3 * l)
        w = w_in[l]
        w_main = jnp.concatenate([w[:, :_OFF_MG], w[:, _OFF_MG + _N_MG:]], axis=1).astype(BF16)
        w_gate = jnp.pad(w[:, _OFF_MG:_OFF_MG + _N_MG], ((0, 0), (0, LANES - _N_MG))).astype(BF16)
        gbias = jnp.pad(ml_gate_bias[l].reshape(1, _N_MG), ((0, 0), (0, LANES - _N_MG)))
        proj, gates = _inproj(x2, norm1_g[l].reshape(1, d), w_main, w_gate, gbias)

        gt = gates[:, :_N_MG].T
        gp = _gateprep(gt, ML_CHUNK)
        grow = gp.reshape(4, ML_HEADS, 1, tokens)
        gcol = gp.reshape(4, ML_HEADS, batch, seq).transpose(2, 1, 3, 0)
        kt = proj[:, _C_MK:_C_MK + ML_HEADS * ML_DQK].reshape(batch, seq, ML_HEADS, ML_DQK).transpose(0, 2, 3, 1)
        hm = _mlstm(proj, kt, grow, gcol, ml_norm_g[l].reshape(1, ML_DV), batch, seq, ML_CHUNK)

        gq = jnp.tile(da_q_norm_g[l], LANES // DA_DQK).reshape(1, LANES)
        gk = jnp.tile(da_k_norm_g[l], LANES // DA_DQK).reshape(1, LANES)
        qr, kr = _qkprep(proj, cosf, sins, gq, gk, gmat, seq)
        qt = qr.reshape(batch, seq, DA_HEADS, DA_DV).transpose(0, 2, 3, 1)
        vt = proj[:, _C_DV:_C_DV + DA_W].reshape(batch, seq, DA_HEADS, DA_DV).transpose(0, 2, 3, 1)
        vt = jnp.concatenate([vt, jnp.ones((batch, DA_HEADS, 8, seq), BF16)], axis=2)
        hd = _attn(qt, kr, vt, da_lambda[l], da_norm_g[l].reshape(DA_DV, 1), batch, seq, lam_init)

        wr = jnp.pad(w_router[l], ((0, 0), (0, LANES - N_EXPERTS)))
        br = jnp.pad(b_router[l].reshape(1, N_EXPERTS), ((0, 0), (0, LANES - N_EXPERTS)))
        x1, h2, logits = _post(hm, hd, proj, x2, w_ml_out[l].astype(BF16), w_da_out[l].astype(BF16),
                               w_o[l].astype(BF16), norm2_g[l].reshape(1, d), wr, br)

        ids, ranks, wts, cnt = _route(logits)
        dest, item_exp, item_start, item_nsub, item_tail = _moe_tables(ids[:, :TOP_K], ranks[:, :TOP_K], cnt[0, :N_EXPERTS])
        dest3 = dest.reshape(tokens // bm_tok, 1, bm_tok * TOP_K)
        xs0 = jnp.zeros((n_rows, d // 2), F32)
        xs = _dispatch(dest3, h2, xs0, bm_tok)
        ys = _experts(item_exp, item_start, item_nsub, item_tail, xs, w_gate_up[l],
                      b_gate_up[l].reshape(N_EXPERTS, 1, 2 * D_EXPERT), w_down[l],
                      b_down[l].reshape(N_EXPERTS, 1, d), pe, po, n_rows)
        x2 = _combine(dest3, ys, x1, wts, bm_tok)
    return x2.reshape(batch, seq, d)
```

```python
import functools
import math

import jax
import jax.numpy as jnp
from jax import lax
from jax.experimental import pallas as pl
from jax.experimental.pallas import tpu as pltpu

F32 = jnp.float32
BF16 = jnp.bfloat16

D_MODEL = 2048
ML_HEADS = 4
ML_DQK = 128
ML_DV = 256
ML_W = ML_HEADS * ML_DV
DA_HEADS = 8
DA_DQK = 64
DA_DV = 2 * DA_DQK
DA_W = DA_HEADS * DA_DV
ROPE_THETA = 10000.0
N_EXPERTS = 32
TOP_K = 4
D_EXPERT = D_MODEL
SWIGLU_LIMIT = 7.0
SWIGLU_ALPHA = 1.702
EPS = 1e-6
LOG2E = 1.4426950408889634

LANES = 128
VMEM_LIMIT = 48 * 1024 * 1024

_OFF_MG = 2 * ML_HEADS * ML_DQK + 2 * ML_W
_N_MG = 4 * ML_HEADS
_C_MQ, _C_MK, _C_MV, _C_MO = 0, 512, 1024, 2048
_C_DQ, _C_DK, _C_DV, _C_GM, _C_GD = 3072, 4096, 5120, 6144, 8192
_N_MAIN = 10240

ML_CHUNK = 256
AT_QB = 512
AT_KB = 256
MOE_SUB = 256
MOE_RMAX = 2048
MOE_BIG = 1024
MOE_FC = 256
MOE_NC = 512


def _cparams(sem, vmem=VMEM_LIMIT):
    return pltpu.CompilerParams(dimension_semantics=sem, vmem_limit_bytes=vmem)


def _pack_halves(x):
    n = x.shape[1] // 2
    lo = lax.bitcast_convert_type(x[:, :n].astype(BF16).astype(F32), jnp.uint32)
    hi = lax.bitcast_convert_type(x[:, n:].astype(BF16).astype(F32), jnp.uint32)
    return lax.bitcast_convert_type(lax.shift_right_logical(lo, jnp.uint32(16)) | hi, F32)


def _unpack_halves(w):
    u = lax.bitcast_convert_type(w, jnp.uint32)
    lo = lax.bitcast_convert_type(lax.shift_left(u, jnp.uint32(16)), F32)
    hi = lax.bitcast_convert_type(u & jnp.uint32(0xFFFF0000), F32)
    return lo, hi


def _inproj_body(x_ref, g_ref, w_ref, wg_ref, gb_ref, o_ref, og_ref, xn_ref):
    @pl.when(pl.program_id(1) == 0)
    def _():
        x = x_ref[...]
        ms = jnp.mean(x * x, axis=-1, keepdims=True)
        xn = (x * lax.rsqrt(ms + EPS) * g_ref[...]).astype(BF16)
        xn_ref[...] = xn
        og_ref[...] = jnp.dot(xn, wg_ref[...], preferred_element_type=F32) + gb_ref[...]

    o_ref[...] = jnp.dot(xn_ref[...], w_ref[...], preferred_element_type=F32).astype(o_ref.dtype)


def _inproj(x2, g1, w_main, w_gate, gate_bias, bm=1024, bn=1024):
    m, d = x2.shape
    n = w_main.shape[1]
    return pl.pallas_call(
        _inproj_body,
        grid=(m // bm, n // bn),
        in_specs=[
            pl.BlockSpec((bm, d), lambda i, j: (i, 0)),
            pl.BlockSpec((1, d), lambda i, j: (0, 0)),
            pl.BlockSpec((d, bn), lambda i, j: (0, j)),
            pl.BlockSpec((d, LANES), lambda i, j: (0, 0)),
            pl.BlockSpec((1, LANES), lambda i, j: (0, 0)),
        ],
        out_specs=[
            pl.BlockSpec((bm, bn), lambda i, j: (i, j)),
            pl.BlockSpec((bm, LANES), lambda i, j: (i, 0)),
        ],
        out_shape=[
            jax.ShapeDtypeStruct((m, n), BF16),
            jax.ShapeDtypeStruct((m, LANES), F32),
        ],
        scratch_shapes=[pltpu.VMEM((bm, d), BF16)],
        compiler_params=_cparams(("parallel", "arbitrary")),
        name="inproj",
    )(x2, g1, w_main, w_gate, gate_bias)


def _gateprep_body(g_ref, o_ref):
    x = g_ref[...]
    c = x.shape[1]
    lf = jnp.minimum(x, 0.0) - jnp.log1p(jnp.exp(-jnp.abs(x)))
    r = lax.broadcasted_iota(jnp.int32, (c, c), 0)
    s = lax.broadcasted_iota(jnp.int32, (c, c), 1)
    upper = (r <= s).astype(F32)
    lower = (r >= s).astype(F32)
    pre = jnp.dot(lf, upper, preferred_element_type=F32, precision=lax.Precision.HIGHEST)
    suf = jnp.dot(lf, lower, preferred_element_type=F32, precision=lax.Precision.HIGHEST)
    row = lax.broadcasted_iota(jnp.int32, x.shape, 0)
    h = ML_HEADS
    out = jnp.where((row >= h) & (row < 2 * h), pre, x)
    out = jnp.where(row >= 3 * h, suf, out)
    o_ref[...] = out


def _gateprep(gt, chunk):
    r, n = gt.shape
    return pl.pallas_call(
        _gateprep_body,
        grid=(n // chunk,),
        in_specs=[pl.BlockSpec((r, chunk), lambda i: (0, i))],
        out_specs=pl.BlockSpec((r, chunk), lambda i: (0, i)),
        out_shape=jax.ShapeDtypeStruct((r, n), F32),
        compiler_params=_cparams(("parallel",)),
        name="gateprep",
    )(gt)


def _mlstm_body(q_ref, kt_ref, v_ref, mo_ref, grow_ref, gcol_ref, ng_ref, o_ref,
                vaug_ref, hf_ref, hb_ref, *, chunk):
    s_len = q_ref.shape[0]
    nc = s_len // chunk
    dv = ML_DV
    wide = dv + LANES
    inv_scale = float(ML_DQK) ** 0.5

    vaug_ref[:, :dv] = v_ref[...]
    lane = lax.broadcasted_iota(jnp.int32, (s_len, LANES), 1)
    vaug_ref[:, dv:] = jnp.where(lane == 0, 1.0, 0.0).astype(BF16)

    rr = lax.broadcasted_iota(jnp.int32, (chunk, chunk), 0)
    cc = lax.broadcasted_iota(jnp.int32, (chunk, chunk), 1)

    def chunk_step(c, state, m_prev, reverse):
        r0 = pl.multiple_of(c * chunk, chunk)
        gi, gb = (2, 3) if reverse else (0, 1)
        qc = q_ref[pl.ds(r0, chunk), :]
        ktc = kt_ref[:, pl.ds(r0, chunk)]
        vac = vaug_ref[pl.ds(r0, chunk), :]
        i_row = grow_ref[gi, :, pl.ds(r0, chunk)]
        b_row = grow_ref[gb, :, pl.ds(r0, chunk)]
        gcol = gcol_ref[0, pl.ds(r0, chunk), :]
        i_col = gcol[:, gi:gi + 1]
        b_col = gcol[:, gb:gb + 1]
        mask = (cc >= rr) if reverse else (cc <= rr)
        log_d = jnp.where(mask, b_col - (b_row - i_row), -jnp.inf)
        inter = b_col + m_prev
        m_t = jnp.maximum(inter, jnp.max(log_d, axis=1, keepdims=True))
        s_inter = jnp.exp(inter - m_t)
        dm = jnp.exp(log_d - m_t)
        sqk = jnp.dot(qc, ktc, preferred_element_type=F32)
        p = (dm * sqk).astype(BF16)
        nd = (jnp.dot(p, vac, preferred_element_type=F32)
              + s_inter * jnp.dot(qc, state.astype(BF16), preferred_element_type=F32))
        den = nd[:, dv:dv + 1]
        floor = jnp.exp(-m_t) * inv_scale
        h = nd[:, :dv] / jnp.maximum(jnp.abs(den), floor)
        g = b_col[0:1, :] if reverse else b_col[chunk - 1:chunk, :]
        a_col = g - b_col + i_col
        m_new = jnp.maximum(g + m_prev, jnp.max(a_col, axis=0, keepdims=True))
        decay = jnp.exp(g + m_prev - m_new)
        w_col = jnp.exp(a_col - m_new)
        wv = (w_col * vac.astype(F32)).astype(BF16)
        new_state = decay * state + jnp.dot(ktc, wv, preferred_element_type=F32)
        return r0, h, new_state, m_new

    def body(c, carry):
        sf, mf, sb, mb = carry
        r0, h, sf, mf = chunk_step(c, sf, mf, False)
        hf_ref[pl.ds(r0, chunk), :] = h
        r1, h2, sb, mb = chunk_step(nc - 1 - c, sb, mb, True)
        hb_ref[pl.ds(r1, chunk), :] = h2
        return sf, mf, sb, mb

    z = jnp.zeros((ML_DQK, wide), F32)
    m0 = jnp.zeros((1, 1), F32)
    lax.fori_loop(0, nc, body, (z, m0, z, m0))

    def epilogue(c, _):
        r0 = pl.multiple_of(c * chunk, chunk)
        h = hf_ref[pl.ds(r0, chunk), :] + hb_ref[pl.ds(r0, chunk), :]
        ms = jnp.mean(h * h, axis=-1, keepdims=True)
        hn = h * lax.rsqrt(ms + EPS) * ng_ref[...]
        gate = jax.nn.sigmoid(mo_ref[pl.ds(r0, chunk), :].astype(F32))
        o_ref[pl.ds(r0, chunk), :] = (hn * gate).astype(o_ref.dtype)
        return 0

    lax.fori_loop(0, nc, epilogue, 0)


def _mlstm(proj, kt, grow, gcol, ng, batch, seq, chunk):
    m = proj.shape[0]
    nsb = 1
    del nsb
    return pl.pallas_call(
        functools.partial(_mlstm_body, chunk=chunk),
        grid=(batch, ML_HEADS),
        in_specs=[
            pl.BlockSpec((seq, ML_DQK), lambda b, h: (b, _C_MQ // ML_DQK + h)),
            pl.BlockSpec((None, None, ML_DQK, seq), lambda b, h: (b, h, 0, 0)),
            pl.BlockSpec((seq, ML_DV), lambda b, h: (b, _C_MV // ML_DV + h)),
            pl.BlockSpec((seq, ML_DV), lambda b, h: (b, _C_MO // ML_DV + h)),
            pl.BlockSpec((4, None, 1, seq), lambda b, h: (0, h, 0, b)),
            pl.BlockSpec((None, 1, seq, 4), lambda b, h: (b, h, 0, 0)),
            pl.BlockSpec((1, ML_DV), lambda b, h: (0, 0)),
        ],
        out_specs=pl.BlockSpec((seq, ML_DV), lambda b, h: (b, h)),
        out_shape=jax.ShapeDtypeStruct((m, ML_W), BF16),
        scratch_shapes=[
            pltpu.VMEM((seq, ML_DV + LANES), BF16),
            pltpu.VMEM((seq, ML_DV), F32),
            pltpu.VMEM((seq, ML_DV), F32),
        ],
        compiler_params=_cparams(("parallel", "parallel")),
        name="mlstm",
    )(proj, kt, proj, proj, grow, gcol, ng)


def _qkprep_body(q_ref, k_ref, cos_ref, sin_ref, gq_ref, gk_ref, gm_ref, oq_ref, ok_ref, *, qscale):
    cosf = cos_ref[...]
    sins = sin_ref[...]
    lane = lax.broadcasted_iota(jnp.int32, cosf.shape, 1)
    first_half = (lane % DA_DQK) < (DA_DQK // 2)
    gmat = gm_ref[...]

    def one(x_ref, g_ref, o_ref, scale):
        for j in range(x_ref.shape[1] // LANES):
            x = x_ref[:, j * LANES:(j + 1) * LANES].astype(F32)
            ms = jnp.dot((x * x).astype(BF16), gmat, preferred_element_type=F32)
            y = x * lax.rsqrt(ms + EPS) * g_ref[...]
            rot = jnp.where(first_half, pltpu.roll(y, LANES - DA_DQK // 2, 1), pltpu.roll(y, DA_DQK // 2, 1))
            o = y * cosf + rot * sins
            if scale != 1.0:
                o = o * scale
            o_ref[:, j * LANES:(j + 1) * LANES] = o.astype(o_ref.dtype)

    one(q_ref, gq_ref, oq_ref, qscale)
    one(k_ref, gk_ref, ok_ref, 1.0)


def _qkprep(proj, cosf, sins, gq, gk, gmat, seq, bm=512):
    m = proj.shape[0]
    nsb = seq // bm
    return pl.pallas_call(
        functools.partial(_qkprep_body, qscale=float(DA_DQK) ** -0.5 * LOG2E),
        grid=(m // bm,),
        in_specs=[
            pl.BlockSpec((bm, DA_W), lambda i: (i, _C_DQ // DA_W)),
            pl.BlockSpec((bm, DA_W), lambda i: (i, _C_DK // DA_W)),
            pl.BlockSpec((bm, LANES), lambda i: (i % nsb, 0)),
            pl.BlockSpec((bm, LANES), lambda i: (i % nsb, 0)),
            pl.BlockSpec((1, LANES), lambda i: (0, 0)),
            pl.BlockSpec((1, LANES), lambda i: (0, 0)),
            pl.BlockSpec((LANES, LANES), lambda i: (0, 0)),
        ],
        out_specs=[
            pl.BlockSpec((bm, DA_W), lambda i: (i, 0)),
            pl.BlockSpec((bm, DA_W), lambda i: (i, 0)),
        ],
        out_shape=[jax.ShapeDtypeStruct((m, DA_W), BF16)] * 2,
        compiler_params=_cparams(("parallel",)),
        name="qkprep",
    )(proj, proj, cosf, sins, gq, gk, gmat)


def _attn_body(qt_ref, k_ref, vt_ref, lam_ref, ng_ref, o_ref, *, kb, lam_init):
    qt = qt_ref[...]
    qb = qt.shape[1]
    s_len = k_ref.shape[0]
    row = lax.broadcasted_iota(jnp.int32, qt.shape, 0)
    zero = jnp.zeros_like(qt)
    q1 = jnp.where(row < DA_DQK, qt, zero)
    q2 = jnp.where(row >= DA_DQK, qt, zero)
    nv = vt_ref.shape[0]

    def upd(s, m_old, acc, vb):
        m_new = jnp.maximum(m_old, jnp.max(s, axis=0, keepdims=True))
        alpha = jnp.exp2(m_old - m_new)
        p = jnp.exp2(s - m_new).astype(BF16)
        return m_new, alpha * acc + jnp.dot(vb, p, preferred_element_type=F32)

    def scores(j):
        kblk = k_ref[j * kb:(j + 1) * kb, :]
        return (jnp.dot(kblk, q1, preferred_element_type=F32),
                jnp.dot(kblk, q2, preferred_element_type=F32))

    m1 = m2 = jnp.full((1, qb), -jnp.inf, F32)
    a1 = a2 = jnp.zeros((nv, qb), F32)
    nkb = s_len // kb
    s_cur = scores(0)
    for j in range(nkb):
        s_nxt = scores(j + 1) if j + 1 < nkb else None
        vb = vt_ref[:, j * kb:(j + 1) * kb]
        m1, a1 = upd(s_cur[0], m1, a1, vb)
        m2, a2 = upd(s_cur[1], m2, a2, vb)
        s_cur = s_nxt

    lp = lam_ref[...]
    lam = (jnp.exp(jnp.sum(lp[0:1, :] * lp[1:2, :], axis=1, keepdims=True))
           - jnp.exp(jnp.sum(lp[2:3, :] * lp[3:4, :], axis=1, keepdims=True)) + lam_init)
    o = a1[:DA_DV, :] / a1[DA_DV:DA_DV + 1, :] - lam * (a2[:DA_DV, :] / a2[DA_DV:DA_DV + 1, :])
    ms = jnp.mean(o * o, axis=0, keepdims=True)
    on = o * lax.rsqrt(ms + EPS) * ng_ref[...] * (1.0 - lam_init)
    o_ref[...] = on.T.astype(o_ref.dtype)


def _attn(qt, kr, vt, lam_params, ng_col, batch, seq, lam_init, qb=AT_QB, kb=AT_KB):
    nq = seq // qb
    nv = vt.shape[2]
    return pl.pallas_call(
        functools.partial(_attn_body, kb=kb, lam_init=lam_init),
        grid=(batch, DA_HEADS, nq),
        in_specs=[
            pl.BlockSpec((None, None, DA_DV, qb), lambda b, h, i: (b, h, 0, i)),
            pl.BlockSpec((seq, DA_DV), lambda b, h, i: (b, h)),
            pl.BlockSpec((None, None, nv, seq), lambda b, h, i: (b, h, 0, 0)),
            pl.BlockSpec((4, DA_DQK), lambda b, h, i: (0, 0)),
            pl.BlockSpec((DA_DV, 1), lambda b, h, i: (0, 0)),
        ],
        out_specs=pl.BlockSpec((qb, DA_DV), lambda b, h, i: (b * nq + i, h)),
        out_shape=jax.ShapeDtypeStruct((batch * seq, DA_W), BF16),
        compiler_params=_cparams(("parallel", "parallel", "parallel")),
        name="diffattn",
    )(qt, kr, vt, lam_params, ng_col)


def _post_body(hm_ref, hd_ref, gm_ref, gd_ref, x_ref, wm_ref, wd_ref, wo_ref, g2_ref, wrh_ref, wrl_ref, br_ref,
               x1_ref, h2_ref, lg_ref, *, tile):
    rows = [slice(t * tile, (t + 1) * tile) for t in range(x_ref.shape[0] // tile)]
    dot = functools.partial(jnp.dot, preferred_element_type=F32)
    ys = [(dot(hm_ref[r, :], wm_ref[...]), dot(hd_ref[r, :], wd_ref[...])) for r in rows]
    mixes = [(jax.nn.sigmoid(gm_ref[r, :].astype(F32)) * ym
              + jax.nn.sigmoid(gd_ref[r, :].astype(F32)) * yd).astype(BF16) for r, (ym, yd) in zip(rows, ys)]
    x1s = [x_ref[r, :] + dot(mix, wo_ref[...]) for r, mix in zip(rows, mixes)]
    for r, x1 in zip(rows, x1s):
        x1_ref[r, :] = x1
        ms = jnp.mean(x1 * x1, axis=-1, keepdims=True)
        h2 = x1 * lax.rsqrt(ms + EPS) * g2_ref[...]
        h2_ref[r, :] = _pack_halves(h2)
        hi = h2.astype(BF16)
        lo = (h2 - hi.astype(F32)).astype(BF16)
        lg_ref[r, :] = (dot(hi, wrh_ref[...]) + dot(hi, wrl_ref[...]) + dot(lo, wrh_ref[...])) + br_ref[...]


def _post(hm, hd, proj, x2, wm, wd, wo, g2, wr, br, bm=512, tile=256):
    m, d = x2.shape
    const = lambda i: (0, 0)
    wr_hi = wr.astype(BF16)
    wr_lo = (wr - wr_hi.astype(F32)).astype(BF16)
    return pl.pallas_call(
        functools.partial(_post_body, tile=tile),
        grid=(m // bm,),
        in_specs=[
            pl.BlockSpec((bm, ML_W), lambda i: (i, 0)),
            pl.BlockSpec((bm, DA_W), lambda i: (i, 0)),
            pl.BlockSpec((bm, d), lambda i: (i, _C_GM // D_MODEL)),
            pl.BlockSpec((bm, d), lambda i: (i, _C_GD // D_MODEL)),
            pl.BlockSpec((bm, d), lambda i: (i, 0)),
            pl.BlockSpec((ML_W, d), const, pipeline_mode=pl.Buffered(1)),
            pl.BlockSpec((DA_W, d), const, pipeline_mode=pl.Buffered(1)),
            pl.BlockSpec((d, d), const, pipeline_mode=pl.Buffered(1)),
            pl.BlockSpec((1, d), const),
            pl.BlockSpec((d, LANES), const),
            pl.BlockSpec((d, LANES), const),
            pl.BlockSpec((1, LANES), const),
        ],
        out_specs=[
            pl.BlockSpec((bm, d), lambda i: (i, 0)),
            pl.BlockSpec((bm, d // 2), lambda i: (i, 0)),
            pl.BlockSpec((bm, LANES), lambda i: (i, 0)),
        ],
        out_shape=[
            jax.ShapeDtypeStruct((m, d), F32),
            jax.ShapeDtypeStruct((m, d // 2), F32),
            jax.ShapeDtypeStruct((m, LANES), F32),
        ],
        compiler_params=_cparams(("parallel",), vmem=62 * 1024 * 1024),
        name="postmix",
    )(hm, hd, proj, proj, x2, wm, wd, wo, g2, wr_hi, wr_lo, br)


def _route_body(lg_ref, ids_ref, rank_ref, wts_ref, cnt_ref, carry_ref):
    @pl.when(pl.program_id(0) == 0)
    def _():
        carry_ref[...] = jnp.zeros_like(carry_ref)

    lg = lg_ref[...]
    bm = lg.shape[0]
    lane = lax.broadcasted_iota(jnp.int32, lg.shape, 1)
    lanef = lane.astype(F32)
    work = jnp.where(lane < N_EXPERTS, lg, -jnp.inf)
    vals, hots = [], []
    ids = jnp.zeros(lg.shape, F32)
    for k in range(TOP_K):
        mx = jnp.max(work, axis=1, keepdims=True)
        idx = jnp.min(jnp.where(work == mx, lanef, float(LANES)), axis=1, keepdims=True)
        hot = lanef == idx
        work = jnp.where(hot, -jnp.inf, work)
        vals.append(mx)
        hots.append(hot)
        ids = jnp.where(lane == k, idx, ids)
    exps = [jnp.exp(v - vals[0]) for v in vals]
    tot = exps[0] + exps[1] + exps[2] + exps[3]
    sel = jnp.zeros(lg.shape, F32)
    for hot in hots:
        sel = jnp.where(hot, 1.0, sel)
    r = lax.broadcasted_iota(jnp.int32, (bm, bm), 0)
    c = lax.broadcasted_iota(jnp.int32, (bm, bm), 1)
    strict = jnp.where(c < r, 1.0, 0.0).astype(BF16)
    cum = jnp.dot(strict, sel.astype(BF16), preferred_element_type=F32) + carry_ref[0:1, :]
    ranks = jnp.zeros(lg.shape, F32)
    wts = jnp.zeros(lg.shape, F32)
    for k in range(TOP_K):
        rk = jnp.sum(jnp.where(hots[k], cum, 0.0), axis=1, keepdims=True)
        ranks = jnp.where(lane == k, rk, ranks)
        wts = jnp.where(lane == k, exps[k] / tot, wts)
    newc = carry_ref[0:1, :] + jnp.sum(sel, axis=0, keepdims=True)
    carry_ref[...] = jnp.broadcast_to(newc, carry_ref.shape)
    ids_ref[...] = ids.astype(jnp.int32)
    rank_ref[...] = ranks.astype(jnp.int32)
    wts_ref[...] = wts
    cnt_ref[...] = jnp.broadcast_to(newc, cnt_ref.shape)


def _route(logits, bm=512):
    m = logits.shape[0]
    blk = pl.BlockSpec((bm, LANES), lambda i: (i, 0))
    return pl.pallas_call(
        _route_body,
        grid=(m // bm,),
        in_specs=[blk],
        out_specs=[blk, blk, blk, pl.BlockSpec((8, LANES), lambda i: (0, 0))],
        out_shape=[
            jax.ShapeDtypeStruct((m, LANES), jnp.int32),
            jax.ShapeDtypeStruct((m, LANES), jnp.int32),
            jax.ShapeDtypeStruct((m, LANES), F32),
            jax.ShapeDtypeStruct((8, LANES), F32),
        ],
        scratch_shapes=[pltpu.VMEM((8, LANES), F32)],
        compiler_params=_cparams(("arbitrary",)),
        name="route",
    )(logits)


def _dispatch_body(dest_ref, h2_ref, xs_in_ref, xs_ref, sem, *, bm):
    del xs_in_ref

    def copy(t, k):
        return pltpu.make_async_copy(h2_ref.at[pl.ds(t, 1), :],
                                     xs_ref.at[pl.ds(dest_ref[0, 0, t * TOP_K + k], 1), :], sem)

    def start(t, _):
        for k in range(TOP_K):
            copy(t, k).start(priority=k % 2)
        return 0

    def wait(t, _):
        for k in range(TOP_K):
            copy(t, k).wait()
        return 0

    lax.fori_loop(0, bm, start, 0, unroll=8)
    lax.fori_loop(0, bm, wait, 0, unroll=8)


def _dispatch(dest3, h2, xs0, bm):
    m = h2.shape[0]
    return pl.pallas_call(
        functools.partial(_dispatch_body, bm=bm),
        grid=(m // bm,),
        in_specs=[
            pl.BlockSpec((1, 1, bm * TOP_K), lambda i: (i, 0, 0), memory_space=pltpu.SMEM),
            pl.BlockSpec((bm, h2.shape[1]), lambda i: (i, 0)),
            pl.BlockSpec(memory_space=pl.ANY),
        ],
        out_specs=pl.BlockSpec(memory_space=pl.ANY),
        out_shape=jax.ShapeDtypeStruct(xs0.shape, xs0.dtype),
        scratch_shapes=[pltpu.SemaphoreType.DMA(())],
        input_output_aliases={2: 0},
        compiler_params=_cparams(("arbitrary",)),
        name="dispatch",
    )(dest3, h2, xs0)


def _expert_body(iexp_ref, istart_ref, insub_ref, itail_ref, xs_ref, wgu_ref, bgu_ref, wdn_ref, bdn_ref,
                 ys_ref, xb_ref, acc_ref, wgb_ref, wdb_ref, sem_in, sem_out):
    del iexp_ref
    i = pl.program_id(0)
    f = pl.program_id(1)
    nf = pl.num_programs(1)
    nsub = insub_ref[i]
    start = pl.multiple_of(istart_ref[i], MOE_SUB)
    d = acc_ref.shape[1]
    half = d // 2

    def sub_rows(sb):
        return pl.ds(pl.multiple_of(sb * MOE_SUB, MOE_SUB), MOE_SUB)

    @pl.when((f == 0) & (nsub > 0))
    def _():
        def x_copy(sb):
            r0 = pl.multiple_of(sb * MOE_SUB, MOE_SUB)
            return pltpu.make_async_copy(xs_ref.at[pl.ds(start + r0, MOE_SUB), :],
                                         acc_ref.at[pl.ds(r0, MOE_SUB), 0:half], sem_in)

        def x_start(sb, _):
            x_copy(sb).start()
            return 0

        def x_wait(sb, _):
            x_copy(sb).wait()
            return 0

        def unpack(sb, _):
            lo, hi = _unpack_halves(acc_ref[sub_rows(sb), 0:half])
            xb_ref[sub_rows(sb), 0:half] = lo.astype(BF16)
            xb_ref[sub_rows(sb), half:d] = hi.astype(BF16)
            return 0

        def init(sb, _):
            acc_ref[sub_rows(sb), :] = jnp.broadcast_to(bdn_ref[...], (MOE_SUB, d))
            return 0

        lax.fori_loop(0, nsub, x_start, 0)
        lax.fori_loop(0, nsub, x_wait, 0)
        lax.fori_loop(0, nsub, unpack, 0)
        lax.fori_loop(0, nsub, init, 0)

    @pl.when(nsub > 0)
    def _():
        wgb_ref[...] = wgu_ref[...].astype(BF16)
        wdb_ref[...] = wdn_ref[...].astype(BF16)

        lane = lax.broadcasted_iota(jnp.int32, (MOE_SUB, LANES), 1)
        low = lane < LANES // 2
        idx = jnp.where(low, 2 * lane, 2 * lane - (LANES - 1))

        def activation(gu):
            glus, lins = [], []
            for j in range(0, 2 * MOE_FC, 2 * LANES):
                pa = jnp.take_along_axis(gu[:, j:j + LANES], idx, axis=1)
                pb = jnp.take_along_axis(gu[:, j + LANES:j + 2 * LANES], idx, axis=1)
                glus.append(jnp.where(low, pa, pltpu.roll(pb, LANES // 2, 1)))
                lins.append(jnp.where(low, pltpu.roll(pa, LANES // 2, 1), pb))
            glu = jnp.minimum(jnp.concatenate(glus, axis=1), SWIGLU_LIMIT)
            lin = jnp.clip(jnp.concatenate(lins, axis=1), -SWIGLU_LIMIT, SWIGLU_LIMIT)
            return (glu * jax.nn.sigmoid(SWIGLU_ALPHA * glu) * (lin + 1.0)).astype(BF16)

        def block(r0, rows):
            tiles = [pl.ds(r0 + t, MOE_SUB) for t in range(0, rows, MOE_SUB)]
            gus = [jnp.dot(xb_ref[r, :], wgb_ref[...], preferred_element_type=F32) + bgu_ref[...] for r in tiles]
            acts = [activation(gu) for gu in gus]
            for r, act in zip(tiles, acts):
                for n0 in range(0, d, MOE_NC):
                    acc_ref[r, n0:n0 + MOE_NC] += jnp.dot(
                        act, wdb_ref[:, n0:n0 + MOE_NC], preferred_element_type=F32)

        per_big = MOE_BIG // MOE_SUB
        nbig = lax.shift_right_logical(nsub, per_big.bit_length() - 1)

        def big(b, _):
            block(pl.multiple_of(b * MOE_BIG, MOE_BIG), MOE_BIG)
            return 0

        def small(sb, _):
            block(pl.multiple_of(sb * MOE_SUB, MOE_SUB), MOE_SUB)
            return 0

        lax.fori_loop(0, nbig, big, 0)
        lax.fori_loop(nbig * per_big, nsub, small, 0)

    @pl.when(f == nf - 1)
    def _():
        def out_copy(sb):
            r0 = pl.multiple_of(sb * MOE_SUB, MOE_SUB)
            return pltpu.make_async_copy(acc_ref.at[pl.ds(r0, MOE_SUB), :],
                                         ys_ref.at[pl.ds(start + r0, MOE_SUB), :], sem_out)

        def out_start(sb, _):
            out_copy(sb).start()
            return 0

        def out_wait(sb, _):
            out_copy(sb).wait()
            return 0

        lax.fori_loop(0, nsub, out_start, 0)
        lax.fori_loop(0, nsub, out_wait, 0)

    @pl.when((i == pl.num_programs(0) - 1) & (f == nf - 1))
    def _():
        first = lax.shift_right_logical(itail_ref[0], MOE_SUB.bit_length() - 1)
        last = ys_ref.shape[0] // MOE_SUB
        acc_ref[0:MOE_SUB, :] = jnp.zeros((MOE_SUB, d), F32)

        def tail_copy(sb):
            r0 = pl.multiple_of(sb * MOE_SUB, MOE_SUB)
            return pltpu.make_async_copy(acc_ref.at[0:MOE_SUB, :], ys_ref.at[pl.ds(r0, MOE_SUB), :], sem_out)

        def tail_start(sb, _):
            tail_copy(sb).start()
            return 0

        def tail_wait(sb, _):
            tail_copy(sb).wait()
            return 0

        lax.fori_loop(first, last, tail_start, 0)
        lax.fori_loop(first, last, tail_wait, 0)


def _experts(item_exp, item_start, item_nsub, item_tail, xs, w_gu, b_gu, w_dn, b_dn, n_rows):
    n_items = item_exp.shape[0]
    d = D_MODEL
    nf = D_EXPERT // MOE_FC
    grid_spec = pltpu.PrefetchScalarGridSpec(
        num_scalar_prefetch=4,
        grid=(n_items, nf),
        in_specs=[
            pl.BlockSpec(memory_space=pl.ANY),
            pl.BlockSpec((None, d, 2 * MOE_FC), lambda i, f, e, s, n, t: (e[i], 0, f)),
            pl.BlockSpec((None, 1, 2 * MOE_FC), lambda i, f, e, s, n, t: (e[i], 0, f)),
            pl.BlockSpec((None, MOE_FC, d), lambda i, f, e, s, n, t: (e[i], f, 0)),
            pl.BlockSpec((None, 1, d), lambda i, f, e, s, n, t: (e[i], 0, 0)),
        ],
        out_specs=pl.BlockSpec(memory_space=pl.ANY),
        scratch_shapes=[
            pltpu.VMEM((MOE_RMAX, d), BF16),
            pltpu.VMEM((MOE_RMAX, d), F32),
            pltpu.VMEM((d, 2 * MOE_FC), BF16),
            pltpu.VMEM((MOE_FC, d), BF16),
            pltpu.SemaphoreType.DMA(()),
            pltpu.SemaphoreType.DMA(()),
        ],
    )
    return pl.pallas_call(
        _expert_body,
        grid_spec=grid_spec,
        out_shape=jax.ShapeDtypeStruct((n_rows, d), F32),
        compiler_params=_cparams(("arbitrary", "arbitrary"), vmem=58 * 1024 * 1024),
        name="experts",
    )(item_exp, item_start, item_nsub, item_tail, xs, w_gu, b_gu, w_dn, b_dn)


def _combine_body(dest_ref, ys_ref, x1_ref, wts_ref, o_ref, buf_ref, sem, *, bm):
    def copy(t, k):
        return pltpu.make_async_copy(ys_ref.at[pl.ds(dest_ref[0, 0, t * TOP_K + k], 1), :],
                                     buf_ref.at[k, pl.ds(t, 1), :], sem)

    def start(t, _):
        for k in range(TOP_K):
            copy(t, k).start(priority=k % 2)
        return 0

    def wait(t, _):
        for k in range(TOP_K):
            copy(t, k).wait()
        return 0

    lax.fori_loop(0, bm, start, 0, unroll=8)
    lax.fori_loop(0, bm, wait, 0, unroll=8)
    w = wts_ref[...]
    acc = x1_ref[...]
    for k in range(TOP_K):
        acc = acc + w[:, k:k + 1] * buf_ref[k]
    o_ref[...] = acc


def _combine(dest3, ys, x1, wts, bm):
    m, d = x1.shape
    return pl.pallas_call(
        functools.partial(_combine_body, bm=bm),
        grid=(m // bm,),
        in_specs=[
            pl.BlockSpec((1, 1, bm * TOP_K), lambda i: (i, 0, 0), memory_space=pltpu.SMEM),
            pl.BlockSpec(memory_space=pl.ANY),
            pl.BlockSpec((bm, d), lambda i: (i, 0)),
            pl.BlockSpec((bm, LANES), lambda i: (i, 0)),
        ],
        out_specs=pl.BlockSpec((bm, d), lambda i: (i, 0)),
        out_shape=jax.ShapeDtypeStruct((m, d), F32),
        scratch_shapes=[pltpu.VMEM((TOP_K, bm, d), F32), pltpu.SemaphoreType.DMA(())],
        compiler_params=_cparams(("arbitrary",)),
        name="combine",
    )(dest3, ys, x1, wts)


def _rope_tables(seq):
    half = DA_DQK // 2
    inv = ROPE_THETA ** (-jnp.arange(0, DA_DQK, 2, dtype=F32) / DA_DQK)
    ang = jnp.arange(seq, dtype=F32)[:, None] * inv[None, :]
    cos, sin = jnp.cos(ang), jnp.sin(ang)
    reps = LANES // DA_DQK
    cosf = jnp.tile(jnp.concatenate([cos, cos], axis=1), (1, reps))
    sins = jnp.tile(jnp.concatenate([-sin, sin], axis=1), (1, reps))
    del half
    return cosf, sins


def _moe_tables(ids, ranks, cnt_row):
    t = ids.shape[0]
    counts = cnt_row.astype(jnp.int32)
    nsb = (counts + MOE_SUB - 1) // MOE_SUB
    padded = nsb * MOE_SUB
    pad_end = jnp.cumsum(padded)
    pad_start = pad_end - padded
    dest = pad_start[ids] + ranks
    per_item = MOE_RMAX // MOE_SUB
    items_e = (nsb + per_item - 1) // per_item
    item_end = jnp.cumsum(items_e)
    n_items = (t * TOP_K // MOE_SUB + N_EXPERTS) // per_item + N_EXPERTS
    idx = jnp.arange(n_items, dtype=jnp.int32)
    e_of = jnp.minimum(jnp.searchsorted(item_end, idx, side="right"), N_EXPERTS - 1).astype(jnp.int32)
    local = idx - (item_end[e_of] - items_e[e_of])
    valid = idx < item_end[-1]
    nsub = jnp.where(valid, jnp.clip(nsb[e_of] - local * per_item, 0, per_item), 0).astype(jnp.int32)
    last_e = e_of[jnp.maximum(item_end[-1] - 1, 0)]
    item_exp = jnp.where(valid, e_of, last_e).astype(jnp.int32)
    item_start = jnp.where(valid, pad_start[e_of] + local * MOE_RMAX, 0).astype(jnp.int32)
    return dest.astype(jnp.int32), item_exp, item_start, nsub, pad_end[-1:].astype(jnp.int32)


def kernel(x, norm1_g, w_in, ml_gate_bias, ml_norm_g, w_ml_out, da_q_norm_g, da_k_norm_g, da_lambda, da_norm_g,
           w_da_out, w_o, norm2_g, w_router, b_router, w_gate_up, b_gate_up, w_down, b_down):
    batch, seq, d = x.shape
    depth = norm1_g.shape[0]
    tokens = batch * seq
    cosf, sins = _rope_tables(seq)
    gmat = jnp.kron(jnp.eye(LANES // DA_DQK, dtype=F32), jnp.full((DA_DQK, DA_DQK), 1.0 / DA_DQK, F32)).astype(BF16)
    n_rows = tokens * TOP_K + N_EXPERTS * MOE_SUB
    bm_tok = 256

    x2 = x.reshape(tokens, d)
    for l in range(depth):
        lam_init = 0.8 - 0.6 * math.exp(-0.3 * l)
        w = w_in[l]
        w_main = jnp.concatenate([w[:, :_OFF_MG], w[:, _OFF_MG + _N_MG:]], axis=1).astype(BF16)
        w_gate = jnp.pad(w[:, _OFF_MG:_OFF_MG + _N_MG], ((0, 0), (0, LANES - _N_MG))).astype(BF16)
        gbias = jnp.pad(ml_gate_bias[l].reshape(1, _N_MG), ((0, 0), (0, LANES - _N_MG)))
        proj, gates = _inproj(x2, norm1_g[l].reshape(1, d), w_main, w_gate, gbias)

        gt = gates[:, :_N_MG].T
        gp = _gateprep(gt, ML_CHUNK)
        grow = gp.reshape(4, ML_HEADS, 1, tokens)
        gcol = gp.reshape(4, ML_HEADS, batch, seq).transpose(2, 1, 3, 0)
        kt = proj[:, _C_MK:_C_MK + ML_HEADS * ML_DQK].reshape(batch, seq, ML_HEADS, ML_DQK).transpose(0, 2, 3, 1)
        hm = _mlstm(proj, kt, grow, gcol, ml_norm_g[l].reshape(1, ML_DV), batch, seq, ML_CHUNK)

        gq = jnp.tile(da_q_norm_g[l], LANES // DA_DQK).reshape(1, LANES)
        gk = jnp.tile(da_k_norm_g[l], LANES // DA_DQK).reshape(1, LANES)
        qr, kr = _qkprep(proj, cosf, sins, gq, gk, gmat, seq)
        qt = qr.reshape(batch, seq, DA_HEADS, DA_DV).transpose(0, 2, 3, 1)
        vt = proj[:, _C_DV:_C_DV + DA_W].reshape(batch, seq, DA_HEADS, DA_DV).transpose(0, 2, 3, 1)
        vt = jnp.concatenate([vt, jnp.ones((batch, DA_HEADS, 8, seq), BF16)], axis=2)
        hd = _attn(qt, kr, vt, da_lambda[l], da_norm_g[l].reshape(DA_DV, 1), batch, seq, lam_init)

        wr = jnp.pad(w_router[l], ((0, 0), (0, LANES - N_EXPERTS)))
        br = jnp.pad(b_router[l].reshape(1, N_EXPERTS), ((0, 0), (0, LANES - N_EXPERTS)))
        x1, h2, logits = _post(hm, hd, proj, x2, w_ml_out[l].astype(BF16), w_da_out[l].astype(BF16),
                               w_o[l].astype(BF16), norm2_g[l].reshape(1, d), wr, br)

        ids, ranks, wts, cnt = _route(logits)
        dest, item_exp, item_start, item_nsub, item_tail = _moe_tables(ids[:, :TOP_K], ranks[:, :TOP_K], cnt[0, :N_EXPERTS])
        dest3 = dest.reshape(tokens // bm_tok, 1, bm_tok * TOP_K)
        xs0 = jnp.zeros((n_rows, d // 2), F32)
        xs = _dispatch(dest3, h2, xs0, bm_tok)
        ys = _experts(item_exp, item_start, item_nsub, item_tail, xs, w_gate_up[l],
                      b_gate_up[l].reshape(N_EXPERTS, 1, 2 * D_EXPERT), w_down[l],
                      b_down[l].reshape(N_EXPERTS, 1, d), n_rows)
        x2 = _combine(dest3, ys, x1, wts, bm_tok)
    return x2.reshape(batch, seq, d)
```

```python
import functools
import math

import jax
import jax.numpy as jnp
from jax import lax
from jax.experimental import pallas as pl
from jax.experimental.pallas import tpu as pltpu

F32 = jnp.float32
BF16 = jnp.bfloat16

D_MODEL = 2048
ML_HEADS = 4
ML_DQK = 128
ML_DV = 256
ML_W = ML_HEADS * ML_DV
DA_HEADS = 8
DA_DQK = 64
DA_DV = 2 * DA_DQK
DA_W = DA_HEADS * DA_DV
ROPE_THETA = 10000.0
N_EXPERTS = 32
TOP_K = 4
D_EXPERT = D_MODEL
SWIGLU_LIMIT = 7.0
SWIGLU_ALPHA = 1.702
EPS = 1e-6
LOG2E = 1.4426950408889634

LANES = 128
VMEM_LIMIT = 48 * 1024 * 1024

_OFF_MG = 2 * ML_HEADS * ML_DQK + 2 * ML_W
_N_MG = 4 * ML_HEADS
_C_MQ, _C_MK, _C_MV, _C_MO = 0, 512, 1024, 2048
_C_DQ, _C_DK, _C_DV, _C_GM, _C_GD = 3072, 4096, 5120, 6144, 8192
_N_MAIN = 10240

ML_CHUNK = 256
AT_QB = 512
AT_KB = 256
MOE_SUB = 256
MOE_RMAX = 2048
MOE_BIG = 1024
MOE_FC = 256
MOE_NC = 512
MOE_KC = 256


def _cparams(sem, vmem=VMEM_LIMIT):
    return pltpu.CompilerParams(dimension_semantics=sem, vmem_limit_bytes=vmem)


def _pack_halves(x):
    n = x.shape[1] // 2
    lo = lax.bitcast_convert_type(x[:, :n].astype(BF16).astype(F32), jnp.uint32)
    hi = lax.bitcast_convert_type(x[:, n:].astype(BF16).astype(F32), jnp.uint32)
    return lax.bitcast_convert_type(lax.shift_right_logical(lo, jnp.uint32(16)) | hi, F32)


def _unpack_halves(w):
    u = lax.bitcast_convert_type(w, jnp.uint32)
    lo = lax.bitcast_convert_type(lax.shift_left(u, jnp.uint32(16)), F32)
    hi = lax.bitcast_convert_type(u & jnp.uint32(0xFFFF0000), F32)
    return lo, hi


def _inproj_body(x_ref, g_ref, w_ref, wg_ref, gb_ref, o_ref, og_ref, xn_ref):
    @pl.when(pl.program_id(1) == 0)
    def _():
        x = x_ref[...]
        ms = jnp.mean(x * x, axis=-1, keepdims=True)
        xn = (x * lax.rsqrt(ms + EPS) * g_ref[...]).astype(BF16)
        xn_ref[...] = xn
        gates = jnp.dot(xn, wg_ref[...], preferred_element_type=F32) + gb_ref[...]
        og_ref[...] = gates.T[0:_N_MG, :]

    o_ref[...] = jnp.dot(xn_ref[...], w_ref[...], preferred_element_type=F32).astype(o_ref.dtype)


def _inproj(x2, g1, w_main, w_gate, gate_bias, bm=1024, bn=1024):
    m, d = x2.shape
    n = w_main.shape[1]
    return pl.pallas_call(
        _inproj_body,
        grid=(m // bm, n // bn),
        in_specs=[
            pl.BlockSpec((bm, d), lambda i, j: (i, 0)),
            pl.BlockSpec((1, d), lambda i, j: (0, 0)),
            pl.BlockSpec((d, bn), lambda i, j: (0, j)),
            pl.BlockSpec((d, LANES), lambda i, j: (0, 0)),
            pl.BlockSpec((1, LANES), lambda i, j: (0, 0)),
        ],
        out_specs=[
            pl.BlockSpec((bm, bn), lambda i, j: (i, j)),
            pl.BlockSpec((_N_MG, bm), lambda i, j: (0, i)),
        ],
        out_shape=[
            jax.ShapeDtypeStruct((m, n), BF16),
            jax.ShapeDtypeStruct((_N_MG, m), F32),
        ],
        scratch_shapes=[pltpu.VMEM((bm, d), BF16)],
        compiler_params=_cparams(("parallel", "arbitrary")),
        name="inproj",
    )(x2, g1, w_main, w_gate, gate_bias)


def _gateprep_body(g_ref, o_ref):
    x = g_ref[...]
    c = x.shape[1]
    lf = jnp.minimum(x, 0.0) - jnp.log1p(jnp.exp(-jnp.abs(x)))
    r = lax.broadcasted_iota(jnp.int32, (c, c), 0)
    s = lax.broadcasted_iota(jnp.int32, (c, c), 1)
    upper = (r <= s).astype(F32)
    lower = (r >= s).astype(F32)
    pre = jnp.dot(lf, upper, preferred_element_type=F32, precision=lax.Precision.HIGHEST)
    suf = jnp.dot(lf, lower, preferred_element_type=F32, precision=lax.Precision.HIGHEST)
    row = lax.broadcasted_iota(jnp.int32, x.shape, 0)
    h = ML_HEADS
    out = jnp.where((row >= h) & (row < 2 * h), pre, x)
    out = jnp.where(row >= 3 * h, suf, out)
    o_ref[...] = out


def _gateprep(gt, chunk):
    r, n = gt.shape
    return pl.pallas_call(
        _gateprep_body,
        grid=(n // chunk,),
        in_specs=[pl.BlockSpec((r, chunk), lambda i: (0, i))],
        out_specs=pl.BlockSpec((r, chunk), lambda i: (0, i)),
        out_shape=jax.ShapeDtypeStruct((r, n), F32),
        compiler_params=_cparams(("parallel",)),
        name="gateprep",
    )(gt)


def _mlstm_body(q_ref, k_ref, v_ref, mo_ref, grow_ref, gcol_ref, ng_ref, o_ref,
                kt_ref, vaug_ref, hf_ref, hb_ref, *, chunk):
    s_len = q_ref.shape[0]
    nc = s_len // chunk
    dv = ML_DV
    wide = dv + LANES
    inv_scale = float(ML_DQK) ** 0.5

    def transpose_k(c, _):
        r0 = pl.multiple_of(c * chunk, chunk)
        kt_ref[:, pl.ds(r0, chunk)] = k_ref[pl.ds(r0, chunk), :].astype(F32).T.astype(BF16)
        return 0

    lax.fori_loop(0, nc, transpose_k, 0)

    vaug_ref[:, :dv] = v_ref[...]
    lane = lax.broadcasted_iota(jnp.int32, (s_len, LANES), 1)
    vaug_ref[:, dv:] = jnp.where(lane == 0, 1.0, 0.0).astype(BF16)

    rr = lax.broadcasted_iota(jnp.int32, (chunk, chunk), 0)
    cc = lax.broadcasted_iota(jnp.int32, (chunk, chunk), 1)

    def chunk_step(c, state, m_prev, reverse):
        r0 = pl.multiple_of(c * chunk, chunk)
        gi, gb = (2, 3) if reverse else (0, 1)
        qc = q_ref[pl.ds(r0, chunk), :]
        ktc = kt_ref[:, pl.ds(r0, chunk)]
        vac = vaug_ref[pl.ds(r0, chunk), :]
        i_row = grow_ref[gi, :, pl.ds(r0, chunk)]
        b_row = grow_ref[gb, :, pl.ds(r0, chunk)]
        gcol = gcol_ref[0, pl.ds(r0, chunk), :]
        i_col = gcol[:, gi:gi + 1]
        b_col = gcol[:, gb:gb + 1]
        mask = (cc >= rr) if reverse else (cc <= rr)
        log_d = jnp.where(mask, b_col - (b_row - i_row), -jnp.inf)
        inter = b_col + m_prev
        m_t = jnp.maximum(inter, jnp.max(log_d, axis=1, keepdims=True))
        s_inter = jnp.exp(inter - m_t)
        dm = jnp.exp(log_d - m_t)
        sqk = jnp.dot(qc, ktc, preferred_element_type=F32)
        p = (dm * sqk).astype(BF16)
        nd = (jnp.dot(p, vac, preferred_element_type=F32)
              + s_inter * jnp.dot(qc, state.astype(BF16), preferred_element_type=F32))
        den = nd[:, dv:dv + 1]
        floor = jnp.exp(-m_t) * inv_scale
        h = nd[:, :dv] / jnp.maximum(jnp.abs(den), floor)
        g = b_col[0:1, :] if reverse else b_col[chunk - 1:chunk, :]
        a_col = g - b_col + i_col
        m_new = jnp.maximum(g + m_prev, jnp.max(a_col, axis=0, keepdims=True))
        decay = jnp.exp(g + m_prev - m_new)
        w_col = jnp.exp(a_col - m_new)
        wv = (w_col * vac.astype(F32)).astype(BF16)
        new_state = decay * state + jnp.dot(ktc, wv, preferred_element_type=F32)
        return r0, h, new_state, m_new

    def body(c, carry):
        sf, mf, sb, mb = carry
        r0, h, sf, mf = chunk_step(c, sf, mf, False)
        hf_ref[pl.ds(r0, chunk), :] = h
        r1, h2, sb, mb = chunk_step(nc - 1 - c, sb, mb, True)
        hb_ref[pl.ds(r1, chunk), :] = h2
        return sf, mf, sb, mb

    z = jnp.zeros((ML_DQK, wide), F32)
    m0 = jnp.zeros((1, 1), F32)
    lax.fori_loop(0, nc, body, (z, m0, z, m0))

    def epilogue(c, _):
        r0 = pl.multiple_of(c * chunk, chunk)
        h = hf_ref[pl.ds(r0, chunk), :] + hb_ref[pl.ds(r0, chunk), :]
        ms = jnp.mean(h * h, axis=-1, keepdims=True)
        hn = h * lax.rsqrt(ms + EPS) * ng_ref[...]
        gate = jax.nn.sigmoid(mo_ref[pl.ds(r0, chunk), :].astype(F32))
        o_ref[pl.ds(r0, chunk), :] = (hn * gate).astype(o_ref.dtype)
        return 0

    lax.fori_loop(0, nc, epilogue, 0)


def _mlstm(proj, grow, gcol, ng, batch, seq, chunk):
    m = proj.shape[0]
    return pl.pallas_call(
        functools.partial(_mlstm_body, chunk=chunk),
        grid=(batch, ML_HEADS),
        in_specs=[
            pl.BlockSpec((seq, ML_DQK), lambda b, h: (b, _C_MQ // ML_DQK + h)),
            pl.BlockSpec((seq, ML_DQK), lambda b, h: (b, _C_MK // ML_DQK + h)),
            pl.BlockSpec((seq, ML_DV), lambda b, h: (b, _C_MV // ML_DV + h)),
            pl.BlockSpec((seq, ML_DV), lambda b, h: (b, _C_MO // ML_DV + h)),
            pl.BlockSpec((4, None, 1, seq), lambda b, h: (0, h, 0, b)),
            pl.BlockSpec((None, 1, seq, 4), lambda b, h: (b, h, 0, 0)),
            pl.BlockSpec((1, ML_DV), lambda b, h: (0, 0)),
        ],
        out_specs=pl.BlockSpec((seq, ML_DV), lambda b, h: (b, h)),
        out_shape=jax.ShapeDtypeStruct((m, ML_W), BF16),
        scratch_shapes=[
            pltpu.VMEM((ML_DQK, seq), BF16),
            pltpu.VMEM((seq, ML_DV + LANES), BF16),
            pltpu.VMEM((seq, ML_DV), F32),
            pltpu.VMEM((seq, ML_DV), F32),
        ],
        compiler_params=_cparams(("parallel", "parallel")),
        name="mlstm",
    )(proj, proj, proj, proj, grow, gcol, ng)


def _qkprep_body(q_ref, k_ref, cos_ref, sin_ref, gq_ref, gk_ref, gm_ref, oq_ref, ok_ref, *, qscale):
    cosf = cos_ref[...]
    sins = sin_ref[...]
    lane = lax.broadcasted_iota(jnp.int32, cosf.shape, 1)
    first_half = (lane % DA_DQK) < (DA_DQK // 2)
    gmat = gm_ref[...]

    def one(x_ref, g_ref, o_ref, scale):
        for j in range(x_ref.shape[1] // LANES):
            x = x_ref[:, j * LANES:(j + 1) * LANES].astype(F32)
            ms = jnp.dot((x * x).astype(BF16), gmat, preferred_element_type=F32)
            y = x * lax.rsqrt(ms + EPS) * g_ref[...]
            rot = jnp.where(first_half, pltpu.roll(y, LANES - DA_DQK // 2, 1), pltpu.roll(y, DA_DQK // 2, 1))
            o = y * cosf + rot * sins
            if scale != 1.0:
                o = o * scale
            o_ref[:, j * LANES:(j + 1) * LANES] = o.astype(o_ref.dtype)

    one(q_ref, gq_ref, oq_ref, qscale)
    one(k_ref, gk_ref, ok_ref, 1.0)


def _qkprep(proj, cosf, sins, gq, gk, gmat, seq, bm=512):
    m = proj.shape[0]
    nsb = seq // bm
    return pl.pallas_call(
        functools.partial(_qkprep_body, qscale=float(DA_DQK) ** -0.5 * LOG2E),
        grid=(m // bm,),
        in_specs=[
            pl.BlockSpec((bm, DA_W), lambda i: (i, _C_DQ // DA_W)),
            pl.BlockSpec((bm, DA_W), lambda i: (i, _C_DK // DA_W)),
            pl.BlockSpec((bm, LANES), lambda i: (i % nsb, 0)),
            pl.BlockSpec((bm, LANES), lambda i: (i % nsb, 0)),
            pl.BlockSpec((1, LANES), lambda i: (0, 0)),
            pl.BlockSpec((1, LANES), lambda i: (0, 0)),
            pl.BlockSpec((LANES, LANES), lambda i: (0, 0)),
        ],
        out_specs=[
            pl.BlockSpec((bm, DA_W), lambda i: (i, 0)),
            pl.BlockSpec((bm, DA_W), lambda i: (i, 0)),
        ],
        out_shape=[jax.ShapeDtypeStruct((m, DA_W), BF16)] * 2,
        compiler_params=_cparams(("parallel",)),
        name="qkprep",
    )(proj, proj, cosf, sins, gq, gk, gmat)


def _attn_body(q_ref, k_ref, v_ref, lam_ref, ng_ref, o_ref, vt_ref, *, kb, lam_init):
    s_len = k_ref.shape[0]

    @pl.when(pl.program_id(2) == 0)
    def _():
        def transpose_v(c, _):
            r0 = pl.multiple_of(c * kb, kb)
            vt_ref[0:DA_DV, pl.ds(r0, kb)] = v_ref[pl.ds(r0, kb), :].astype(F32).T.astype(BF16)
            return 0

        lax.fori_loop(0, s_len // kb, transpose_v, 0)
        vt_ref[DA_DV:, :] = jnp.ones((vt_ref.shape[0] - DA_DV, s_len), BF16)

    qt = q_ref[...].astype(F32).T.astype(BF16)
    qb = qt.shape[1]
    row = lax.broadcasted_iota(jnp.int32, qt.shape, 0)
    zero = jnp.zeros_like(qt)
    q1 = jnp.where(row < DA_DQK, qt, zero)
    q2 = jnp.where(row >= DA_DQK, qt, zero)
    nv = vt_ref.shape[0]

    def upd(s, m_old, acc, vb):
        m_new = jnp.maximum(m_old, jnp.max(s, axis=0, keepdims=True))
        alpha = jnp.exp2(m_old - m_new)
        p = jnp.exp2(s - m_new).astype(BF16)
        return m_new, alpha * acc + jnp.dot(vb, p, preferred_element_type=F32)

    def scores(j):
        kblk = k_ref[j * kb:(j + 1) * kb, :]
        return (jnp.dot(kblk, q1, preferred_element_type=F32),
                jnp.dot(kblk, q2, preferred_element_type=F32))

    m1 = m2 = jnp.full((1, qb), -jnp.inf, F32)
    a1 = a2 = jnp.zeros((nv, qb), F32)
    nkb = s_len // kb
    s_cur = scores(0)
    for j in range(nkb):
        s_nxt = scores(j + 1) if j + 1 < nkb else None
        vb = vt_ref[:, j * kb:(j + 1) * kb]
        m1, a1 = upd(s_cur[0], m1, a1, vb)
        m2, a2 = upd(s_cur[1], m2, a2, vb)
        s_cur = s_nxt

    lp = lam_ref[...]
    lam = (jnp.exp(jnp.sum(lp[0:1, :] * lp[1:2, :], axis=1, keepdims=True))
           - jnp.exp(jnp.sum(lp[2:3, :] * lp[3:4, :], axis=1, keepdims=True)) + lam_init)
    o = a1[:DA_DV, :] / a1[DA_DV:DA_DV + 1, :] - lam * (a2[:DA_DV, :] / a2[DA_DV:DA_DV + 1, :])
    ms = jnp.mean(o * o, axis=0, keepdims=True)
    on = o * lax.rsqrt(ms + EPS) * ng_ref[...] * (1.0 - lam_init)
    o_ref[...] = on.T.astype(o_ref.dtype)


def _attn(qr, kr, proj, lam_params, ng_col, batch, seq, lam_init, qb=AT_QB, kb=AT_KB):
    nq = seq // qb
    return pl.pallas_call(
        functools.partial(_attn_body, kb=kb, lam_init=lam_init),
        grid=(batch, DA_HEADS, nq),
        in_specs=[
            pl.BlockSpec((qb, DA_DV), lambda b, h, i: (b * nq + i, h)),
            pl.BlockSpec((seq, DA_DV), lambda b, h, i: (b, h)),
            pl.BlockSpec((seq, DA_DV), lambda b, h, i: (b, _C_DV // DA_DV + h)),
            pl.BlockSpec((4, DA_DQK), lambda b, h, i: (0, 0)),
            pl.BlockSpec((DA_DV, 1), lambda b, h, i: (0, 0)),
        ],
        out_specs=pl.BlockSpec((qb, DA_DV), lambda b, h, i: (b * nq + i, h)),
        out_shape=jax.ShapeDtypeStruct((batch * seq, DA_W), BF16),
        scratch_shapes=[pltpu.VMEM((DA_DV + 8, seq), BF16)],
        compiler_params=_cparams(("parallel", "parallel", "arbitrary")),
        name="diffattn",
    )(qr, kr, proj, lam_params, ng_col)


def _post_body(hm_ref, hd_ref, gm_ref, gd_ref, x_ref, wm_ref, wd_ref, wo_ref, g2_ref, wrh_ref, wrl_ref, br_ref,
               x1_ref, h2_ref, lg_ref, *, tile):
    rows = [slice(t * tile, (t + 1) * tile) for t in range(x_ref.shape[0] // tile)]
    dot = functools.partial(jnp.dot, preferred_element_type=F32)
    ys = [(dot(hm_ref[r, :], wm_ref[...]), dot(hd_ref[r, :], wd_ref[...])) for r in rows]
    mixes = [(jax.nn.sigmoid(gm_ref[r, :].astype(F32)) * ym
              + jax.nn.sigmoid(gd_ref[r, :].astype(F32)) * yd).astype(BF16) for r, (ym, yd) in zip(rows, ys)]
    x1s = [x_ref[r, :] + dot(mix, wo_ref[...]) for r, mix in zip(rows, mixes)]
    for r, x1 in zip(rows, x1s):
        x1_ref[r, :] = x1
        ms = jnp.mean(x1 * x1, axis=-1, keepdims=True)
        h2 = x1 * lax.rsqrt(ms + EPS) * g2_ref[...]
        h2_ref[r, :] = _pack_halves(h2)
        hi = h2.astype(BF16)
        lo = (h2 - hi.astype(F32)).astype(BF16)
        lg_ref[r, :] = (dot(hi, wrh_ref[...]) + dot(hi, wrl_ref[...]) + dot(lo, wrh_ref[...])) + br_ref[...]


def _post(hm, hd, proj, x2, wm, wd, wo, g2, wr, br, bm=512, tile=256):
    m, d = x2.shape
    const = lambda i: (0, 0)
    wr_hi = wr.astype(BF16)
    wr_lo = (wr - wr_hi.astype(F32)).astype(BF16)
    return pl.pallas_call(
        functools.partial(_post_body, tile=tile),
        grid=(m // bm,),
        in_specs=[
            pl.BlockSpec((bm, ML_W), lambda i: (i, 0)),
            pl.BlockSpec((bm, DA_W), lambda i: (i, 0)),
            pl.BlockSpec((bm, d), lambda i: (i, _C_GM // D_MODEL)),
            pl.BlockSpec((bm, d), lambda i: (i, _C_GD // D_MODEL)),
            pl.BlockSpec((bm, d), lambda i: (i, 0)),
            pl.BlockSpec((ML_W, d), const, pipeline_mode=pl.Buffered(1)),
            pl.BlockSpec((DA_W, d), const, pipeline_mode=pl.Buffered(1)),
            pl.BlockSpec((d, d), const, pipeline_mode=pl.Buffered(1)),
            pl.BlockSpec((1, d), const),
            pl.BlockSpec((d, LANES), const),
            pl.BlockSpec((d, LANES), const),
            pl.BlockSpec((1, LANES), const),
        ],
        out_specs=[
            pl.BlockSpec((bm, d), lambda i: (i, 0)),
            pl.BlockSpec((bm, d // 2), lambda i: (i, 0)),
            pl.BlockSpec((bm, LANES), lambda i: (i, 0)),
        ],
        out_shape=[
            jax.ShapeDtypeStruct((m, d), F32),
            jax.ShapeDtypeStruct((m, d // 2), F32),
            jax.ShapeDtypeStruct((m, LANES), F32),
        ],
        compiler_params=_cparams(("parallel",), vmem=62 * 1024 * 1024),
        name="postmix",
    )(hm, hd, proj, proj, x2, wm, wd, wo, g2, wr_hi, wr_lo, br)


def _route_body(lg_ref, ids_ref, rank_ref, wts_ref, cnt_ref, carry_ref):
    @pl.when(pl.program_id(0) == 0)
    def _():
        carry_ref[...] = jnp.zeros_like(carry_ref)

    lg = lg_ref[...]
    bm = lg.shape[0]
    lane = lax.broadcasted_iota(jnp.int32, lg.shape, 1)
    lanef = lane.astype(F32)
    work = jnp.where(lane < N_EXPERTS, lg, -jnp.inf)
    vals, hots = [], []
    ids = jnp.zeros(lg.shape, F32)
    for k in range(TOP_K):
        mx = jnp.max(work, axis=1, keepdims=True)
        idx = jnp.min(jnp.where(work == mx, lanef, float(LANES)), axis=1, keepdims=True)
        hot = lanef == idx
        work = jnp.where(hot, -jnp.inf, work)
        vals.append(mx)
        hots.append(hot)
        ids = jnp.where(lane == k, idx, ids)
    exps = [jnp.exp(v - vals[0]) for v in vals]
    tot = exps[0] + exps[1] + exps[2] + exps[3]
    sel = jnp.zeros(lg.shape, F32)
    for hot in hots:
        sel = jnp.where(hot, 1.0, sel)
    r = lax.broadcasted_iota(jnp.int32, (bm, bm), 0)
    c = lax.broadcasted_iota(jnp.int32, (bm, bm), 1)
    strict = jnp.where(c < r, 1.0, 0.0).astype(BF16)
    cum = jnp.dot(strict, sel.astype(BF16), preferred_element_type=F32) + carry_ref[0:1, :]
    ranks = jnp.zeros(lg.shape, F32)
    wts = jnp.zeros(lg.shape, F32)
    for k in range(TOP_K):
        rk = jnp.sum(jnp.where(hots[k], cum, 0.0), axis=1, keepdims=True)
        ranks = jnp.where(lane == k, rk, ranks)
        wts = jnp.where(lane == k, exps[k] / tot, wts)
    newc = carry_ref[0:1, :] + jnp.sum(sel, axis=0, keepdims=True)
    carry_ref[...] = jnp.broadcast_to(newc, carry_ref.shape)
    ids_ref[...] = ids.T[0:8, :].astype(jnp.int32)
    rank_ref[...] = ranks.T[0:8, :].astype(jnp.int32)
    wts_ref[...] = wts
    cnt_ref[...] = jnp.broadcast_to(newc, cnt_ref.shape)


def _route(logits, bm=512):
    m = logits.shape[0]
    blk = pl.BlockSpec((bm, LANES), lambda i: (i, 0))
    tblk = pl.BlockSpec((8, bm), lambda i: (0, i))
    return pl.pallas_call(
        _route_body,
        grid=(m // bm,),
        in_specs=[blk],
        out_specs=[tblk, tblk, blk, pl.BlockSpec((8, LANES), lambda i: (0, 0))],
        out_shape=[
            jax.ShapeDtypeStruct((8, m), jnp.int32),
            jax.ShapeDtypeStruct((8, m), jnp.int32),
            jax.ShapeDtypeStruct((m, LANES), F32),
            jax.ShapeDtypeStruct((8, LANES), F32),
        ],
        scratch_shapes=[pltpu.VMEM((8, LANES), F32)],
        compiler_params=_cparams(("arbitrary",)),
        name="route",
    )(logits)


def _dispatch_body(pend_ref, dest_ref, h2_ref, xs_ref, zero_ref, sem, zsem, *, bm):
    @pl.when(pl.program_id(0) == 0)
    def _():
        zero_ref[...] = jnp.zeros_like(zero_ref)
        n_sub_total = xs_ref.shape[0] // MOE_SUB
        first_tail = lax.shift_right_logical(pend_ref[N_EXPERTS - 1], MOE_SUB.bit_length() - 1)

        def zcopy(r0):
            return pltpu.make_async_copy(zero_ref, xs_ref.at[pl.ds(pl.multiple_of(r0, MOE_SUB), MOE_SUB), :], zsem)

        def pad_row(e):
            return jnp.maximum(pend_ref[e] - MOE_SUB, 0)

        def nonempty(e):
            return pend_ref[e] > (pend_ref[e - 1] if e else 0)

        for e in range(N_EXPERTS):
            @pl.when(nonempty(e))
            def _():
                zcopy(pad_row(e)).start()

        def tail_start(sb, _):
            zcopy(sb * MOE_SUB).start()
            return 0

        def tail_wait(sb, _):
            zcopy(sb * MOE_SUB).wait()
            return 0

        lax.fori_loop(first_tail, n_sub_total, tail_start, 0)
        for e in range(N_EXPERTS):
            @pl.when(nonempty(e))
            def _():
                zcopy(pad_row(e)).wait()
        lax.fori_loop(first_tail, n_sub_total, tail_wait, 0)

    def copy(t, k):
        return pltpu.make_async_copy(h2_ref.at[pl.ds(t, 1), :],
                                     xs_ref.at[pl.ds(dest_ref[0, 0, k * bm + t], 1), :], sem)

    def start(t, _):
        for k in range(TOP_K):
            copy(t, k).start(priority=k % 2)
        return 0

    def wait(t, _):
        for k in range(TOP_K):
            copy(t, k).wait()
        return 0

    lax.fori_loop(0, bm, start, 0, unroll=8)
    lax.fori_loop(0, bm, wait, 0, unroll=8)


def _dispatch(pad_end, dest3, h2, n_rows, bm):
    m, width = h2.shape
    grid_spec = pltpu.PrefetchScalarGridSpec(
        num_scalar_prefetch=1,
        grid=(m // bm,),
        in_specs=[
            pl.BlockSpec((1, 1, bm * TOP_K), lambda i, p: (i, 0, 0), memory_space=pltpu.SMEM),
            pl.BlockSpec((bm, width), lambda i, p: (i, 0)),
        ],
        out_specs=pl.BlockSpec(memory_space=pl.ANY),
        scratch_shapes=[
            pltpu.VMEM((MOE_SUB, width), h2.dtype),
            pltpu.SemaphoreType.DMA(()),
            pltpu.SemaphoreType.DMA(()),
        ],
    )
    return pl.pallas_call(
        functools.partial(_dispatch_body, bm=bm),
        grid_spec=grid_spec,
        out_shape=jax.ShapeDtypeStruct((n_rows, width), h2.dtype),
        compiler_params=_cparams(("arbitrary",)),
        name="dispatch",
    )(pad_end, dest3, h2)


def _expert_body(iexp_ref, istart_ref, insub_ref, itail_ref, xs_ref, wgu_ref, bgu_ref, wdn_ref, bdn_ref,
                 ys_ref, xb_ref, acc_ref, wgb_ref, wdb_ref, sem_in, sem_out):
    del iexp_ref
    i = pl.program_id(0)
    f = pl.program_id(1)
    nf = pl.num_programs(1)
    nsub = insub_ref[i]
    start = pl.multiple_of(istart_ref[i], MOE_SUB)
    d = acc_ref.shape[1]
    half = d // 2

    def sub_rows(sb):
        return pl.ds(pl.multiple_of(sb * MOE_SUB, MOE_SUB), MOE_SUB)

    @pl.when((i == 0) & (f == 0))
    def _():
        xb_ref[0:MOE_BIG, :] = jnp.zeros((MOE_BIG, d), BF16)

    @pl.when((f == 0) & (nsub > 0))
    def _():
        def x_copy(sb):
            r0 = pl.multiple_of(sb * MOE_SUB, MOE_SUB)
            return pltpu.make_async_copy(xs_ref.at[pl.ds(start + r0, MOE_SUB), :],
                                         acc_ref.at[pl.ds(r0, MOE_SUB), 0:half], sem_in)

        def x_start(sb, _):
            x_copy(sb).start()
            return 0

        def x_wait(sb, _):
            x_copy(sb).wait()
            return 0

        def unpack(sb, _):
            lo, hi = _unpack_halves(acc_ref[sub_rows(sb), 0:half])
            xb_ref[sub_rows(sb), 0:half] = lo.astype(BF16)
            xb_ref[sub_rows(sb), half:d] = hi.astype(BF16)
            return 0

        def init(sb, _):
            acc_ref[sub_rows(sb), :] = jnp.broadcast_to(bdn_ref[...], (MOE_SUB, d))
            return 0

        lax.fori_loop(0, nsub, x_start, 0)
        lax.fori_loop(0, nsub, x_wait, 0)
        lax.fori_loop(0, nsub, unpack, 0)
        lax.fori_loop(0, jnp.maximum(nsub, MOE_BIG // MOE_SUB), init, 0)

    @pl.when(nsub > 0)
    def _():
        lane = lax.broadcasted_iota(jnp.int32, (MOE_SUB, LANES), 1)
        low = lane < LANES // 2
        idx = jnp.where(low, 2 * lane, 2 * lane - (LANES - 1))

        def activation(gu):
            glus, lins = [], []
            for j in range(0, 2 * MOE_FC, 2 * LANES):
                pa = jnp.take_along_axis(gu[:, j:j + LANES], idx, axis=1)
                pb = jnp.take_along_axis(gu[:, j + LANES:j + 2 * LANES], idx, axis=1)
                glus.append(jnp.where(low, pa, pltpu.roll(pb, LANES // 2, 1)))
                lins.append(jnp.where(low, pltpu.roll(pa, LANES // 2, 1), pb))
            glu = jnp.minimum(jnp.concatenate(glus, axis=1), SWIGLU_LIMIT)
            lin = jnp.clip(jnp.concatenate(lins, axis=1), -SWIGLU_LIMIT, SWIGLU_LIMIT)
            return (glu * jax.nn.sigmoid(SWIGLU_ALPHA * glu) * (lin + 1.0)).astype(BF16)

        def first_up(r):
            g = None
            for k0 in range(0, d, MOE_KC):
                wgb_ref[k0:k0 + MOE_KC, :] = wgu_ref[k0:k0 + MOE_KC, :].astype(BF16)
                part = jnp.dot(xb_ref[r, k0:k0 + MOE_KC], wgb_ref[k0:k0 + MOE_KC, :], preferred_element_type=F32)
                g = part if g is None else g + part
            wdb_ref[...] = wdn_ref[...].astype(BF16)
            return g

        def block(r0, rows, cast_first=False):
            tiles = [pl.ds(r0 + t, MOE_SUB) for t in range(0, rows, MOE_SUB)]
            gus = [(first_up(r) if cast_first and t == 0
                    else jnp.dot(xb_ref[r, :], wgb_ref[...], preferred_element_type=F32)) + bgu_ref[...]
                   for t, r in enumerate(tiles)]
            acts = [activation(gu) for gu in gus]
            for r, act in zip(tiles, acts):
                for n0 in range(0, d, MOE_NC):
                    acc_ref[r, n0:n0 + MOE_NC] += jnp.dot(
                        act, wdb_ref[:, n0:n0 + MOE_NC], preferred_element_type=F32)

        block(0, MOE_BIG, cast_first=True)
        per_big = MOE_BIG // MOE_SUB
        nbig = lax.shift_right_logical(jnp.maximum(nsub, per_big), per_big.bit_length() - 1)

        def big(b, _):
            block(pl.multiple_of(b * MOE_BIG, MOE_BIG), MOE_BIG)
            return 0

        def small(sb, _):
            block(pl.multiple_of(sb * MOE_SUB, MOE_SUB), MOE_SUB)
            return 0

        lax.fori_loop(1, nbig, big, 0)
        lax.fori_loop(nbig * per_big, nsub, small, 0)

    @pl.when(f == nf - 1)
    def _():
        def out_copy(sb):
            r0 = pl.multiple_of(sb * MOE_SUB, MOE_SUB)
            return pltpu.make_async_copy(acc_ref.at[pl.ds(r0, MOE_SUB), :],
                                         ys_ref.at[pl.ds(start + r0, MOE_SUB), :], sem_out)

        def out_start(sb, _):
            out_copy(sb).start()
            return 0

        def out_wait(sb, _):
            out_copy(sb).wait()
            return 0

        lax.fori_loop(0, nsub, out_start, 0)
        lax.fori_loop(0, nsub, out_wait, 0)

    @pl.when((i == pl.num_programs(0) - 1) & (f == nf - 1))
    def _():
        first = lax.shift_right_logical(itail_ref[0], MOE_SUB.bit_length() - 1)
        last = ys_ref.shape[0] // MOE_SUB
        acc_ref[0:MOE_SUB, :] = jnp.zeros((MOE_SUB, d), F32)

        def tail_copy(sb):
            r0 = pl.multiple_of(sb * MOE_SUB, MOE_SUB)
            return pltpu.make_async_copy(acc_ref.at[0:MOE_SUB, :], ys_ref.at[pl.ds(r0, MOE_SUB), :], sem_out)

        def tail_start(sb, _):
            tail_copy(sb).start()
            return 0

        def tail_wait(sb, _):
            tail_copy(sb).wait()
            return 0

        lax.fori_loop(first, last, tail_start, 0)
        lax.fori_loop(first, last, tail_wait, 0)


def _experts(item_exp, item_start, item_nsub, item_tail, xs, w_gu, b_gu, w_dn, b_dn, n_rows):
    n_items = item_exp.shape[0]
    d = D_MODEL
    nf = D_EXPERT // MOE_FC
    grid_spec = pltpu.PrefetchScalarGridSpec(
        num_scalar_prefetch=4,
        grid=(n_items, nf),
        in_specs=[
            pl.BlockSpec(memory_space=pl.ANY),
            pl.BlockSpec((None, d, 2 * MOE_FC), lambda i, f, e, s, n, t: (e[i], 0, f)),
            pl.BlockSpec((None, 1, 2 * MOE_FC), lambda i, f, e, s, n, t: (e[i], 0, f)),
            pl.BlockSpec((None, MOE_FC, d), lambda i, f, e, s, n, t: (e[i], f, 0)),
            pl.BlockSpec((None, 1, d), lambda i, f, e, s, n, t: (e[i], 0, 0)),
        ],
        out_specs=pl.BlockSpec(memory_space=pl.ANY),
        scratch_shapes=[
            pltpu.VMEM((MOE_RMAX, d), BF16),
            pltpu.VMEM((MOE_RMAX, d), F32),
            pltpu.VMEM((d, 2 * MOE_FC), BF16),
            pltpu.VMEM((MOE_FC, d), BF16),
            pltpu.SemaphoreType.DMA(()),
            pltpu.SemaphoreType.DMA(()),
        ],
    )
    return pl.pallas_call(
        _expert_body,
        grid_spec=grid_spec,
        out_shape=jax.ShapeDtypeStruct((n_rows, d), F32),
        compiler_params=_cparams(("arbitrary", "arbitrary"), vmem=58 * 1024 * 1024),
        name="experts",
    )(item_exp, item_start, item_nsub, item_tail, xs, w_gu, b_gu, w_dn, b_dn)


def _combine_body(dest_ref, ys_ref, x1_ref, wts_ref, o_ref, buf_ref, sem, *, bm):
    def copy(t, k):
        return pltpu.make_async_copy(ys_ref.at[pl.ds(dest_ref[0, 0, k * bm + t], 1), :],
                                     buf_ref.at[k, pl.ds(t, 1), :], sem)

    def start(t, _):
        for k in range(TOP_K):
            copy(t, k).start(priority=k % 2)
        return 0

    def wait(t, _):
        for k in range(TOP_K):
            copy(t, k).wait()
        return 0

    lax.fori_loop(0, bm, start, 0, unroll=8)
    lax.fori_loop(0, bm, wait, 0, unroll=8)
    w = wts_ref[...]
    acc = x1_ref[...]
    for k in range(TOP_K):
        acc = acc + w[:, k:k + 1] * buf_ref[k]
    o_ref[...] = acc


def _combine(dest3, ys, x1, wts, bm):
    m, d = x1.shape
    return pl.pallas_call(
        functools.partial(_combine_body, bm=bm),
        grid=(m // bm,),
        in_specs=[
            pl.BlockSpec((1, 1, bm * TOP_K), lambda i: (i, 0, 0), memory_space=pltpu.SMEM),
            pl.BlockSpec(memory_space=pl.ANY),
            pl.BlockSpec((bm, d), lambda i: (i, 0)),
            pl.BlockSpec((bm, LANES), lambda i: (i, 0)),
        ],
        out_specs=pl.BlockSpec((bm, d), lambda i: (i, 0)),
        out_shape=jax.ShapeDtypeStruct((m, d), F32),
        scratch_shapes=[pltpu.VMEM((TOP_K, bm, d), F32), pltpu.SemaphoreType.DMA(())],
        compiler_params=_cparams(("arbitrary",)),
        name="combine",
    )(dest3, ys, x1, wts)


def _rope_tables(seq):
    half = DA_DQK // 2
    inv = ROPE_THETA ** (-jnp.arange(0, DA_DQK, 2, dtype=F32) / DA_DQK)
    ang = jnp.arange(seq, dtype=F32)[:, None] * inv[None, :]
    cos, sin = jnp.cos(ang), jnp.sin(ang)
    reps = LANES // DA_DQK
    cosf = jnp.tile(jnp.concatenate([cos, cos], axis=1), (1, reps))
    sins = jnp.tile(jnp.concatenate([-sin, sin], axis=1), (1, reps))
    del half
    return cosf, sins


def _moe_tables(ids, ranks, cnt_row):
    t = ids.shape[1]
    counts = cnt_row.astype(jnp.int32)
    nsb = (counts + MOE_SUB - 1) // MOE_SUB
    padded = nsb * MOE_SUB
    pad_end = jnp.cumsum(padded)
    pad_start = pad_end - padded
    dest = pad_start[ids] + ranks
    per_item = MOE_RMAX // MOE_SUB
    items_e = (nsb + per_item - 1) // per_item
    item_end = jnp.cumsum(items_e)
    n_items = (t * TOP_K // MOE_SUB + N_EXPERTS) // per_item + N_EXPERTS
    idx = jnp.arange(n_items, dtype=jnp.int32)
    e_of = jnp.minimum(jnp.searchsorted(item_end, idx, side="right"), N_EXPERTS - 1).astype(jnp.int32)
    local = idx - (item_end[e_of] - items_e[e_of])
    valid = idx < item_end[-1]
    nsub = jnp.where(valid, jnp.clip(nsb[e_of] - local * per_item, 0, per_item), 0).astype(jnp.int32)
    last_e = e_of[jnp.maximum(item_end[-1] - 1, 0)]
    item_exp = jnp.where(valid, e_of, last_e).astype(jnp.int32)
    item_start = jnp.where(valid, pad_start[e_of] + local * MOE_RMAX, 0).astype(jnp.int32)
    return dest.astype(jnp.int32), item_exp, item_start, nsub, pad_end.astype(jnp.int32)


def kernel(x, norm1_g, w_in, ml_gate_bias, ml_norm_g, w_ml_out, da_q_norm_g, da_k_norm_g, da_lambda, da_norm_g,
           w_da_out, w_o, norm2_g, w_router, b_router, w_gate_up, b_gate_up, w_down, b_down):
    batch, seq, d = x.shape
    depth = norm1_g.shape[0]
    tokens = batch * seq
    cosf, sins = _rope_tables(seq)
    gmat = jnp.kron(jnp.eye(LANES // DA_DQK, dtype=F32), jnp.full((DA_DQK, DA_DQK), 1.0 / DA_DQK, F32)).astype(BF16)
    n_rows = tokens * TOP_K + N_EXPERTS * MOE_SUB
    bm_tok = 256

    x2 = x.reshape(tokens, d)
    for l in range(depth):
        lam_init = 0.8 - 0.6 * math.exp(-0.3 * l)
        w = w_in[l]
        w_main = jnp.concatenate([w[:, :_OFF_MG], w[:, _OFF_MG + _N_MG:]], axis=1).astype(BF16)
        w_gate = w[:, _OFF_MG:_OFF_MG + LANES].astype(BF16)
        gbias = jnp.pad(ml_gate_bias[l].reshape(1, _N_MG), ((0, 0), (0, LANES - _N_MG)))
        proj, gates = _inproj(x2, norm1_g[l].reshape(1, d), w_main, w_gate, gbias)

        gp = _gateprep(gates, ML_CHUNK)
        grow = gp.reshape(4, ML_HEADS, 1, tokens)
        gcol = gp.reshape(4, ML_HEADS, batch, seq).transpose(2, 1, 3, 0)
        hm = _mlstm(proj, grow, gcol, ml_norm_g[l].reshape(1, ML_DV), batch, seq, ML_CHUNK)

        gq = jnp.tile(da_q_norm_g[l], LANES // DA_DQK).reshape(1, LANES)
        gk = jnp.tile(da_k_norm_g[l], LANES // DA_DQK).reshape(1, LANES)
        qr, kr = _qkprep(proj, cosf, sins, gq, gk, gmat, seq)
        hd = _attn(qr, kr, proj, da_lambda[l], da_norm_g[l].reshape(DA_DV, 1), batch, seq, lam_init)

        wr = jnp.pad(w_router[l], ((0, 0), (0, LANES - N_EXPERTS)))
        br = jnp.pad(b_router[l].reshape(1, N_EXPERTS), ((0, 0), (0, LANES - N_EXPERTS)))
        x1, h2, logits = _post(hm, hd, proj, x2, w_ml_out[l].astype(BF16), w_da_out[l].astype(BF16),
                               w_o[l].astype(BF16), norm2_g[l].reshape(1, d), wr, br)

        ids, ranks, wts, cnt = _route(logits)
        dest, item_exp, item_start, item_nsub, pad_end = _moe_tables(ids[:TOP_K], ranks[:TOP_K], cnt[0, :N_EXPERTS])
        dest3 = dest.reshape(TOP_K, tokens // bm_tok, bm_tok).transpose(1, 0, 2).reshape(
            tokens // bm_tok, 1, bm_tok * TOP_K)
        xs = _dispatch(pad_end, dest3, h2, n_rows, bm_tok)
        ys = _experts(item_exp, item_start, item_nsub, pad_end[-1:], xs, w_gate_up[l],
                      b_gate_up[l].reshape(N_EXPERTS, 1, 2 * D_EXPERT), w_down[l],
                      b_down[l].reshape(N_EXPERTS, 1, d), n_rows)
        x2 = _combine(dest3, ys, x1, wts, bm_tok)
    return x2.reshape(batch, seq, d)
```

```python
import functools
import math

import jax
import jax.numpy as jnp
from jax import lax
from jax.experimental import pallas as pl
from jax.experimental.pallas import tpu as pltpu

F32 = jnp.float32
BF16 = jnp.bfloat16

D_MODEL = 2048
ML_HEADS = 4
ML_DQK = 128
ML_DV = 256
ML_W = ML_HEADS * ML_DV
DA_HEADS = 8
DA_DQK = 64
DA_DV = 2 * DA_DQK
DA_W = DA_HEADS * DA_DV
ROPE_THETA = 10000.0
N_EXPERTS = 32
TOP_K = 4
D_EXPERT = D_MODEL
SWIGLU_LIMIT = 7.0
SWIGLU_ALPHA = 1.702
EPS = 1e-6
LOG2E = 1.4426950408889634

LANES = 128
VMEM_LIMIT = 48 * 1024 * 1024

_OFF_MG = 2 * ML_HEADS * ML_DQK + 2 * ML_W
_N_MG = 4 * ML_HEADS
_C_MQ, _C_MK, _C_MV, _C_MO = 0, 512, 1024, 2048
_C_DQ, _C_DK, _C_DV, _C_GM, _C_GD = 3072, 4096, 5120, 6144, 8192
_N_MAIN = 10240

ML_CHUNK = 256
AT_QB = 512
AT_KB = 256
MOE_SUB = 256
MOE_RMAX = 2048
MOE_BIG = 1024
MOE_FC = 256
MOE_NC = 512
MOE_KC = 256


def _cparams(sem, vmem=VMEM_LIMIT):
    return pltpu.CompilerParams(dimension_semantics=sem, vmem_limit_bytes=vmem)


def _pack_halves(x):
    n = x.shape[1] // 2
    lo = lax.bitcast_convert_type(x[:, :n].astype(BF16).astype(F32), jnp.uint32)
    hi = lax.bitcast_convert_type(x[:, n:].astype(BF16).astype(F32), jnp.uint32)
    return lax.bitcast_convert_type(lax.shift_right_logical(lo, jnp.uint32(16)) | hi, F32)


def _unpack_halves(w):
    u = lax.bitcast_convert_type(w, jnp.uint32)
    lo = lax.bitcast_convert_type(lax.shift_left(u, jnp.uint32(16)), F32)
    hi = lax.bitcast_convert_type(u & jnp.uint32(0xFFFF0000), F32)
    return lo, hi


def _inproj_body(x_ref, g_ref, wlo_ref, whi_ref, wg_ref, gb_ref, o_ref, og_ref, xn_ref, *, nlo):
    j = pl.program_id(1)

    @pl.when(j == 0)
    def _():
        x = x_ref[...]
        ms = jnp.mean(x * x, axis=-1, keepdims=True)
        xn = (x * lax.rsqrt(ms + EPS) * g_ref[...]).astype(BF16)
        xn_ref[...] = xn
        gates = jnp.dot(xn, wg_ref[...], preferred_element_type=F32) + gb_ref[...]
        og_ref[...] = gates.T[0:_N_MG, :]

    @pl.when(j < nlo)
    def _():
        o_ref[...] = jnp.dot(xn_ref[...], wlo_ref[...], preferred_element_type=F32).astype(o_ref.dtype)

    @pl.when(j >= nlo)
    def _():
        o_ref[...] = jnp.dot(xn_ref[...], whi_ref[...], preferred_element_type=F32).astype(o_ref.dtype)


def _inproj(x2, g1, w_lo, w_hi, w_gate, gate_bias, bm=1024, bn=1024):
    m, d = x2.shape
    n = w_hi.shape[1]
    nlo = w_lo.shape[1] // bn
    return pl.pallas_call(
        functools.partial(_inproj_body, nlo=nlo),
        grid=(m // bm, n // bn),
        in_specs=[
            pl.BlockSpec((bm, d), lambda i, j: (i, 0)),
            pl.BlockSpec((1, d), lambda i, j: (0, 0)),
            pl.BlockSpec((d, bn), lambda i, j: (0, jnp.minimum(j, nlo - 1))),
            pl.BlockSpec((d, bn), lambda i, j: (0, jnp.maximum(j, nlo))),
            pl.BlockSpec((d, LANES), lambda i, j: (0, 0)),
            pl.BlockSpec((1, LANES), lambda i, j: (0, 0)),
        ],
        out_specs=[
            pl.BlockSpec((bm, bn), lambda i, j: (i, j)),
            pl.BlockSpec((_N_MG, bm), lambda i, j: (0, i)),
        ],
        out_shape=[
            jax.ShapeDtypeStruct((m, n), BF16),
            jax.ShapeDtypeStruct((_N_MG, m), F32),
        ],
        scratch_shapes=[pltpu.VMEM((bm, d), BF16)],
        compiler_params=_cparams(("parallel", "arbitrary")),
        name="inproj",
    )(x2, g1, w_lo, w_hi, w_gate, gate_bias)


def _gateprep_body(g_ref, o_ref):
    x = g_ref[...]
    c = x.shape[1]
    lf = jnp.minimum(x, 0.0) - jnp.log1p(jnp.exp(-jnp.abs(x)))
    r = lax.broadcasted_iota(jnp.int32, (c, c), 0)
    s = lax.broadcasted_iota(jnp.int32, (c, c), 1)
    upper = (r <= s).astype(F32)
    lower = (r >= s).astype(F32)
    pre = jnp.dot(lf, upper, preferred_element_type=F32, precision=lax.Precision.HIGHEST)
    suf = jnp.dot(lf, lower, preferred_element_type=F32, precision=lax.Precision.HIGHEST)
    row = lax.broadcasted_iota(jnp.int32, x.shape, 0)
    h = ML_HEADS
    out = jnp.where((row >= h) & (row < 2 * h), pre, x)
    out = jnp.where(row >= 3 * h, suf, out)
    o_ref[...] = out


def _gateprep(gt, chunk):
    r, n = gt.shape
    return pl.pallas_call(
        _gateprep_body,
        grid=(n // chunk,),
        in_specs=[pl.BlockSpec((r, chunk), lambda i: (0, i))],
        out_specs=pl.BlockSpec((r, chunk), lambda i: (0, i)),
        out_shape=jax.ShapeDtypeStruct((r, n), F32),
        compiler_params=_cparams(("parallel",)),
        name="gateprep",
    )(gt)


def _mlstm_body(q_ref, k_ref, v_ref, mo_ref, grow_ref, gcol_ref, ng_ref, o_ref,
                kt_ref, vaug_ref, hf_ref, hb_ref, *, chunk):
    s_len = q_ref.shape[0]
    nc = s_len // chunk
    dv = ML_DV
    wide = dv + LANES
    inv_scale = float(ML_DQK) ** 0.5

    def transpose_k(c, _):
        r0 = pl.multiple_of(c * chunk, chunk)
        kt_ref[:, pl.ds(r0, chunk)] = k_ref[pl.ds(r0, chunk), :].astype(F32).T.astype(BF16)
        return 0

    lax.fori_loop(0, nc, transpose_k, 0)

    vaug_ref[:, :dv] = v_ref[...]
    lane = lax.broadcasted_iota(jnp.int32, (s_len, LANES), 1)
    vaug_ref[:, dv:] = jnp.where(lane == 0, 1.0, 0.0).astype(BF16)

    rr = lax.broadcasted_iota(jnp.int32, (chunk, chunk), 0)
    cc = lax.broadcasted_iota(jnp.int32, (chunk, chunk), 1)

    def chunk_step(c, state, m_prev, reverse):
        r0 = pl.multiple_of(c * chunk, chunk)
        gi, gb = (2, 3) if reverse else (0, 1)
        qc = q_ref[pl.ds(r0, chunk), :]
        ktc = kt_ref[:, pl.ds(r0, chunk)]
        vac = vaug_ref[pl.ds(r0, chunk), :]
        i_row = grow_ref[gi, :, pl.ds(r0, chunk)]
        b_row = grow_ref[gb, :, pl.ds(r0, chunk)]
        gcol = gcol_ref[0, pl.ds(r0, chunk), :]
        i_col = gcol[:, gi:gi + 1]
        b_col = gcol[:, gb:gb + 1]
        mask = (cc >= rr) if reverse else (cc <= rr)
        log_d = jnp.where(mask, b_col - (b_row - i_row), -jnp.inf)
        inter = b_col + m_prev
        m_t = jnp.maximum(inter, jnp.max(log_d, axis=1, keepdims=True))
        s_inter = jnp.exp(inter - m_t)
        dm = jnp.exp(log_d - m_t)
        sqk = jnp.dot(qc, ktc, preferred_element_type=F32)
        p = (dm * sqk).astype(BF16)
        nd = (jnp.dot(p, vac, preferred_element_type=F32)
              + s_inter * jnp.dot(qc, state.astype(BF16), preferred_element_type=F32))
        den = nd[:, dv:dv + 1]
        floor = jnp.exp(-m_t) * inv_scale
        h = nd[:, :dv] / jnp.maximum(jnp.abs(den), floor)
        g = b_col[0:1, :] if reverse else b_col[chunk - 1:chunk, :]
        a_col = g - b_col + i_col
        m_new = jnp.maximum(g + m_prev, jnp.max(a_col, axis=0, keepdims=True))
        decay = jnp.exp(g + m_prev - m_new)
        w_col = jnp.exp(a_col - m_new)
        wv = (w_col * vac.astype(F32)).astype(BF16)
        new_state = decay * state + jnp.dot(ktc, wv, preferred_element_type=F32)
        return r0, h, new_state, m_new

    def body(c, carry):
        sf, mf, sb, mb = carry
        r0, h, sf, mf = chunk_step(c, sf, mf, False)
        hf_ref[pl.ds(r0, chunk), :] = h
        r1, h2, sb, mb = chunk_step(nc - 1 - c, sb, mb, True)
        hb_ref[pl.ds(r1, chunk), :] = h2
        return sf, mf, sb, mb

    z = jnp.zeros((ML_DQK, wide), F32)
    m0 = jnp.zeros((1, 1), F32)
    lax.fori_loop(0, nc, body, (z, m0, z, m0))

    def epilogue(c, _):
        r0 = pl.multiple_of(c * chunk, chunk)
        h = hf_ref[pl.ds(r0, chunk), :] + hb_ref[pl.ds(r0, chunk), :]
        ms = jnp.mean(h * h, axis=-1, keepdims=True)
        hn = h * lax.rsqrt(ms + EPS) * ng_ref[...]
        gate = jax.nn.sigmoid(mo_ref[pl.ds(r0, chunk), :].astype(F32))
        o_ref[pl.ds(r0, chunk), :] = (hn * gate).astype(o_ref.dtype)
        return 0

    lax.fori_loop(0, nc, epilogue, 0)


def _mlstm(proj, grow, gcol, ng, batch, seq, chunk):
    m = proj.shape[0]
    return pl.pallas_call(
        functools.partial(_mlstm_body, chunk=chunk),
        grid=(batch, ML_HEADS),
        in_specs=[
            pl.BlockSpec((seq, ML_DQK), lambda b, h: (b, _C_MQ // ML_DQK + h)),
            pl.BlockSpec((seq, ML_DQK), lambda b, h: (b, _C_MK // ML_DQK + h)),
            pl.BlockSpec((seq, ML_DV), lambda b, h: (b, _C_MV // ML_DV + h)),
            pl.BlockSpec((seq, ML_DV), lambda b, h: (b, _C_MO // ML_DV + h)),
            pl.BlockSpec((4, None, 1, seq), lambda b, h: (0, h, 0, b)),
            pl.BlockSpec((None, 1, seq, 4), lambda b, h: (b, h, 0, 0)),
            pl.BlockSpec((1, ML_DV), lambda b, h: (0, 0)),
        ],
        out_specs=pl.BlockSpec((seq, ML_DV), lambda b, h: (b, h)),
        out_shape=jax.ShapeDtypeStruct((m, ML_W), BF16),
        scratch_shapes=[
            pltpu.VMEM((ML_DQK, seq), BF16),
            pltpu.VMEM((seq, ML_DV + LANES), BF16),
            pltpu.VMEM((seq, ML_DV), F32),
            pltpu.VMEM((seq, ML_DV), F32),
        ],
        compiler_params=_cparams(("parallel", "parallel")),
        name="mlstm",
    )(proj, proj, proj, proj, grow, gcol, ng)


def _qkprep_body(q_ref, k_ref, cos_ref, sin_ref, gq_ref, gk_ref, gm_ref, oq_ref, ok_ref, *, qscale):
    cosf = cos_ref[...]
    sins = sin_ref[...]
    lane = lax.broadcasted_iota(jnp.int32, cosf.shape, 1)
    first_half = (lane % DA_DQK) < (DA_DQK // 2)
    gmat = gm_ref[...]

    def one(x_ref, g_ref, o_ref, scale):
        for j in range(x_ref.shape[1] // LANES):
            x = x_ref[:, j * LANES:(j + 1) * LANES].astype(F32)
            ms = jnp.dot((x * x).astype(BF16), gmat, preferred_element_type=F32)
            y = x * lax.rsqrt(ms + EPS) * g_ref[...]
            rot = jnp.where(first_half, pltpu.roll(y, LANES - DA_DQK // 2, 1), pltpu.roll(y, DA_DQK // 2, 1))
            o = y * cosf + rot * sins
            if scale != 1.0:
                o = o * scale
            o_ref[:, j * LANES:(j + 1) * LANES] = o.astype(o_ref.dtype)

    one(q_ref, gq_ref, oq_ref, qscale)
    one(k_ref, gk_ref, ok_ref, 1.0)


def _qkprep(proj, cosf, sins, gq, gk, gmat, seq, bm=512):
    m = proj.shape[0]
    nsb = seq // bm
    return pl.pallas_call(
        functools.partial(_qkprep_body, qscale=float(DA_DQK) ** -0.5 * LOG2E),
        grid=(m // bm,),
        in_specs=[
            pl.BlockSpec((bm, DA_W), lambda i: (i, _C_DQ // DA_W)),
            pl.BlockSpec((bm, DA_W), lambda i: (i, _C_DK // DA_W)),
            pl.BlockSpec((bm, LANES), lambda i: (i % nsb, 0)),
            pl.BlockSpec((bm, LANES), lambda i: (i % nsb, 0)),
            pl.BlockSpec((1, LANES), lambda i: (0, 0)),
            pl.BlockSpec((1, LANES), lambda i: (0, 0)),
            pl.BlockSpec((LANES, LANES), lambda i: (0, 0)),
        ],
        out_specs=[
            pl.BlockSpec((bm, DA_W), lambda i: (i, 0)),
            pl.BlockSpec((bm, DA_W), lambda i: (i, 0)),
        ],
        out_shape=[jax.ShapeDtypeStruct((m, DA_W), BF16)] * 2,
        compiler_params=_cparams(("parallel",)),
        name="qkprep",
    )(proj, proj, cosf, sins, gq, gk, gmat)


def _attn_body(q_ref, k_ref, v_ref, lam_ref, ng_ref, o_ref, vt_ref, *, kb, lam_init):
    s_len = k_ref.shape[0]

    @pl.when(pl.program_id(2) == 0)
    def _():
        def transpose_v(c, _):
            r0 = pl.multiple_of(c * kb, kb)
            vt_ref[0:DA_DV, pl.ds(r0, kb)] = v_ref[pl.ds(r0, kb), :].astype(F32).T.astype(BF16)
            return 0

        lax.fori_loop(0, s_len // kb, transpose_v, 0)
        vt_ref[DA_DV:, :] = jnp.ones((vt_ref.shape[0] - DA_DV, s_len), BF16)

    qt = q_ref[...].astype(F32).T.astype(BF16)
    qb = qt.shape[1]
    row = lax.broadcasted_iota(jnp.int32, qt.shape, 0)
    zero = jnp.zeros_like(qt)
    q1 = jnp.where(row < DA_DQK, qt, zero)
    q2 = jnp.where(row >= DA_DQK, qt, zero)
    nv = vt_ref.shape[0]

    def upd(s, m_old, acc, vb):
        m_new = jnp.maximum(m_old, jnp.max(s, axis=0, keepdims=True))
        alpha = jnp.exp2(m_old - m_new)
        p = jnp.exp2(s - m_new).astype(BF16)
        return m_new, alpha * acc + jnp.dot(vb, p, preferred_element_type=F32)

    def scores(j):
        kblk = k_ref[j * kb:(j + 1) * kb, :]
        return (jnp.dot(kblk, q1, preferred_element_type=F32),
                jnp.dot(kblk, q2, preferred_element_type=F32))

    m1 = m2 = jnp.full((1, qb), -jnp.inf, F32)
    a1 = a2 = jnp.zeros((nv, qb), F32)
    nkb = s_len // kb
    s_cur = scores(0)
    for j in range(nkb):
        s_nxt = scores(j + 1) if j + 1 < nkb else None
        vb = vt_ref[:, j * kb:(j + 1) * kb]
        m1, a1 = upd(s_cur[0], m1, a1, vb)
        m2, a2 = upd(s_cur[1], m2, a2, vb)
        s_cur = s_nxt

    lp = lam_ref[...]
    lam = (jnp.exp(jnp.sum(lp[0:1, :] * lp[1:2, :], axis=1, keepdims=True))
           - jnp.exp(jnp.sum(lp[2:3, :] * lp[3:4, :], axis=1, keepdims=True)) + lam_init)
    o = a1[:DA_DV, :] / a1[DA_DV:DA_DV + 1, :] - lam * (a2[:DA_DV, :] / a2[DA_DV:DA_DV + 1, :])
    ms = jnp.mean(o * o, axis=0, keepdims=True)
    on = o * lax.rsqrt(ms + EPS) * ng_ref[...] * (1.0 - lam_init)
    o_ref[...] = on.T.astype(o_ref.dtype)


def _attn(qr, kr, proj, lam_params, ng_col, batch, seq, lam_init, qb=AT_QB, kb=AT_KB):
    nq = seq // qb
    return pl.pallas_call(
        functools.partial(_attn_body, kb=kb, lam_init=lam_init),
        grid=(batch, DA_HEADS, nq),
        in_specs=[
            pl.BlockSpec((qb, DA_DV), lambda b, h, i: (b * nq + i, h)),
            pl.BlockSpec((seq, DA_DV), lambda b, h, i: (b, h)),
            pl.BlockSpec((seq, DA_DV), lambda b, h, i: (b, _C_DV // DA_DV + h)),
            pl.BlockSpec((4, DA_DQK), lambda b, h, i: (0, 0)),
            pl.BlockSpec((DA_DV, 1), lambda b, h, i: (0, 0)),
        ],
        out_specs=pl.BlockSpec((qb, DA_DV), lambda b, h, i: (b * nq + i, h)),
        out_shape=jax.ShapeDtypeStruct((batch * seq, DA_W), BF16),
        scratch_shapes=[pltpu.VMEM((DA_DV + 8, seq), BF16)],
        compiler_params=_cparams(("parallel", "parallel", "arbitrary")),
        name="diffattn",
    )(qr, kr, proj, lam_params, ng_col)


def _post_body(hm_ref, hd_ref, gm_ref, gd_ref, x_ref, wm_ref, wd_ref, wo_ref, g2_ref, wrh_ref, wrl_ref, br_ref,
               x1_ref, h2_ref, lg_ref, *, tile):
    rows = [slice(t * tile, (t + 1) * tile) for t in range(x_ref.shape[0] // tile)]
    dot = functools.partial(jnp.dot, preferred_element_type=F32)
    ys = [(dot(hm_ref[r, :], wm_ref[...]), dot(hd_ref[r, :], wd_ref[...])) for r in rows]
    mixes = [(jax.nn.sigmoid(gm_ref[r, :].astype(F32)) * ym
              + jax.nn.sigmoid(gd_ref[r, :].astype(F32)) * yd).astype(BF16) for r, (ym, yd) in zip(rows, ys)]
    x1s = [x_ref[r, :] + dot(mix, wo_ref[...]) for r, mix in zip(rows, mixes)]
    for r, x1 in zip(rows, x1s):
        x1_ref[r, :] = x1
        ms = jnp.mean(x1 * x1, axis=-1, keepdims=True)
        h2 = x1 * lax.rsqrt(ms + EPS) * g2_ref[...]
        h2_ref[r, :] = _pack_halves(h2)
        hi = h2.astype(BF16)
        lo = (h2 - hi.astype(F32)).astype(BF16)
        lg_ref[r, :] = (dot(hi, wrh_ref[...]) + dot(hi, wrl_ref[...]) + dot(lo, wrh_ref[...])) + br_ref[...]


def _post(hm, hd, proj, x2, wm, wd, wo, g2, wr, br, bm=512, tile=256):
    m, d = x2.shape
    const = lambda i: (0, 0)
    wr_hi = wr.astype(BF16)
    wr_lo = (wr - wr_hi.astype(F32)).astype(BF16)
    return pl.pallas_call(
        functools.partial(_post_body, tile=tile),
        grid=(m // bm,),
        in_specs=[
            pl.BlockSpec((bm, ML_W), lambda i: (i, 0)),
            pl.BlockSpec((bm, DA_W), lambda i: (i, 0)),
            pl.BlockSpec((bm, d), lambda i: (i, _C_GM // D_MODEL)),
            pl.BlockSpec((bm, d), lambda i: (i, _C_GD // D_MODEL)),
            pl.BlockSpec((bm, d), lambda i: (i, 0)),
            pl.BlockSpec((ML_W, d), const, pipeline_mode=pl.Buffered(1)),
            pl.BlockSpec((DA_W, d), const, pipeline_mode=pl.Buffered(1)),
            pl.BlockSpec((d, d), const, pipeline_mode=pl.Buffered(1)),
            pl.BlockSpec((1, d), const),
            pl.BlockSpec((d, LANES), const),
            pl.BlockSpec((d, LANES), const),
            pl.BlockSpec((1, LANES), const),
        ],
        out_specs=[
            pl.BlockSpec((bm, d), lambda i: (i, 0)),
            pl.BlockSpec((bm, d // 2), lambda i: (i, 0)),
            pl.BlockSpec((bm, LANES), lambda i: (i, 0)),
        ],
        out_shape=[
            jax.ShapeDtypeStruct((m, d), F32),
            jax.ShapeDtypeStruct((m, d // 2), F32),
            jax.ShapeDtypeStruct((m, LANES), F32),
        ],
        compiler_params=_cparams(("parallel",), vmem=62 * 1024 * 1024),
        name="postmix",
    )(hm, hd, proj, proj, x2, wm, wd, wo, g2, wr_hi, wr_lo, br)


def _route_body(lg_ref, ids_ref, rank_ref, wts_ref, cnt_ref, carry_ref):
    @pl.when(pl.program_id(0) == 0)
    def _():
        carry_ref[...] = jnp.zeros_like(carry_ref)

    lg = lg_ref[...]
    bm = lg.shape[0]
    lane = lax.broadcasted_iota(jnp.int32, lg.shape, 1)
    lanef = lane.astype(F32)
    work = jnp.where(lane < N_EXPERTS, lg, -jnp.inf)
    vals, hots = [], []
    ids = jnp.zeros(lg.shape, F32)
    for k in range(TOP_K):
        mx = jnp.max(work, axis=1, keepdims=True)
        idx = jnp.min(jnp.where(work == mx, lanef, float(LANES)), axis=1, keepdims=True)
        hot = lanef == idx
        work = jnp.where(hot, -jnp.inf, work)
        vals.append(mx)
        hots.append(hot)
        ids = jnp.where(lane == k, idx, ids)
    exps = [jnp.exp(v - vals[0]) for v in vals]
    tot = exps[0] + exps[1] + exps[2] + exps[3]
    sel = jnp.zeros(lg.shape, F32)
    for hot in hots:
        sel = jnp.where(hot, 1.0, sel)
    r = lax.broadcasted_iota(jnp.int32, (bm, bm), 0)
    c = lax.broadcasted_iota(jnp.int32, (bm, bm), 1)
    strict = jnp.where(c < r, 1.0, 0.0).astype(BF16)
    cum = jnp.dot(strict, sel.astype(BF16), preferred_element_type=F32) + carry_ref[0:1, :]
    ranks = jnp.zeros(lg.shape, F32)
    wts = jnp.zeros(lg.shape, F32)
    for k in range(TOP_K):
        rk = jnp.sum(jnp.where(hots[k], cum, 0.0), axis=1, keepdims=True)
        ranks = jnp.where(lane == k, rk, ranks)
        wts = jnp.where(lane == k, exps[k] / tot, wts)
    newc = carry_ref[0:1, :] + jnp.sum(sel, axis=0, keepdims=True)
    carry_ref[...] = jnp.broadcast_to(newc, carry_ref.shape)
    ids_ref[...] = ids.T[0:8, :].astype(jnp.int32)
    rank_ref[...] = ranks.T[0:8, :].astype(jnp.int32)
    wts_ref[...] = wts
    cnt_ref[...] = jnp.broadcast_to(newc, cnt_ref.shape)


def _route(logits, bm=512):
    m = logits.shape[0]
    blk = pl.BlockSpec((bm, LANES), lambda i: (i, 0))
    tblk = pl.BlockSpec((8, bm), lambda i: (0, i))
    return pl.pallas_call(
        _route_body,
        grid=(m // bm,),
        in_specs=[blk],
        out_specs=[tblk, tblk, blk, pl.BlockSpec((8, LANES), lambda i: (0, 0))],
        out_shape=[
            jax.ShapeDtypeStruct((8, m), jnp.int32),
            jax.ShapeDtypeStruct((8, m), jnp.int32),
            jax.ShapeDtypeStruct((m, LANES), F32),
            jax.ShapeDtypeStruct((8, LANES), F32),
        ],
        scratch_shapes=[pltpu.VMEM((8, LANES), F32)],
        compiler_params=_cparams(("arbitrary",)),
        name="route",
    )(logits)


def _dispatch_body(pend_ref, dest_ref, h2_ref, xs_ref, zero_ref, sem, zsem, *, bm):
    @pl.when(pl.program_id(0) == 0)
    def _():
        zero_ref[...] = jnp.zeros_like(zero_ref)
        n_sub_total = xs_ref.shape[0] // MOE_SUB
        first_tail = lax.shift_right_logical(pend_ref[N_EXPERTS - 1], MOE_SUB.bit_length() - 1)

        def zcopy(r0):
            return pltpu.make_async_copy(zero_ref, xs_ref.at[pl.ds(pl.multiple_of(r0, MOE_SUB), MOE_SUB), :], zsem)

        def pad_row(e):
            return jnp.maximum(pend_ref[e] - MOE_SUB, 0)

        def nonempty(e):
            return pend_ref[e] > (pend_ref[e - 1] if e else 0)

        for e in range(N_EXPERTS):
            @pl.when(nonempty(e))
            def _():
                zcopy(pad_row(e)).start()

        def tail_start(sb, _):
            zcopy(sb * MOE_SUB).start()
            return 0

        def tail_wait(sb, _):
            zcopy(sb * MOE_SUB).wait()
            return 0

        lax.fori_loop(first_tail, n_sub_total, tail_start, 0)
        for e in range(N_EXPERTS):
            @pl.when(nonempty(e))
            def _():
                zcopy(pad_row(e)).wait()
        lax.fori_loop(first_tail, n_sub_total, tail_wait, 0)

    def copy(t, k):
        return pltpu.make_async_copy(h2_ref.at[pl.ds(t, 1), :],
                                     xs_ref.at[pl.ds(dest_ref[0, 0, k * bm + t], 1), :], sem)

    def start(t, _):
        for k in range(TOP_K):
            copy(t, k).start(priority=k % 2)
        return 0

    def wait(t, _):
        for k in range(TOP_K):
            copy(t, k).wait()
        return 0

    lax.fori_loop(0, bm, start, 0, unroll=8)
    lax.fori_loop(0, bm, wait, 0, unroll=8)


def _dispatch(pad_end, dest3, h2, n_rows, bm):
    m, width = h2.shape
    grid_spec = pltpu.PrefetchScalarGridSpec(
        num_scalar_prefetch=1,
        grid=(m // bm,),
        in_specs=[
            pl.BlockSpec((1, 1, bm * TOP_K), lambda i, p: (i, 0, 0), memory_space=pltpu.SMEM),
            pl.BlockSpec((bm, width), lambda i, p: (i, 0)),
        ],
        out_specs=pl.BlockSpec(memory_space=pl.ANY),
        scratch_shapes=[
            pltpu.VMEM((MOE_SUB, width), h2.dtype),
            pltpu.SemaphoreType.DMA(()),
            pltpu.SemaphoreType.DMA(()),
        ],
    )
    return pl.pallas_call(
        functools.partial(_dispatch_body, bm=bm),
        grid_spec=grid_spec,
        out_shape=jax.ShapeDtypeStruct((n_rows, width), h2.dtype),
        compiler_params=_cparams(("arbitrary",)),
        name="dispatch",
    )(pad_end, dest3, h2)


def _expert_body(iexp_ref, istart_ref, insub_ref, itail_ref, xs_ref, wgu_ref, bgu_ref, wdn_ref, bdn_ref,
                 ys_ref, xb_ref, acc_ref, wgb_ref, wdb_ref, sem_in, sem_out):
    del iexp_ref
    i = pl.program_id(0)
    f = pl.program_id(1)
    nf = pl.num_programs(1)
    nsub = insub_ref[i]
    start = pl.multiple_of(istart_ref[i], MOE_SUB)
    d = acc_ref.shape[1]
    half = d // 2

    def sub_rows(sb):
        return pl.ds(pl.multiple_of(sb * MOE_SUB, MOE_SUB), MOE_SUB)

    @pl.when((i == 0) & (f == 0))
    def _():
        xb_ref[0:MOE_BIG, :] = jnp.zeros((MOE_BIG, d), BF16)

    @pl.when((f == 0) & (nsub > 0))
    def _():
        def x_copy(sb):
            r0 = pl.multiple_of(sb * MOE_SUB, MOE_SUB)
            return pltpu.make_async_copy(xs_ref.at[pl.ds(start + r0, MOE_SUB), :],
                                         acc_ref.at[pl.ds(r0, MOE_SUB), 0:half], sem_in)

        def x_start(sb, _):
            x_copy(sb).start()
            return 0

        def x_wait(sb, _):
            x_copy(sb).wait()
            return 0

        def unpack(sb, _):
            lo, hi = _unpack_halves(acc_ref[sub_rows(sb), 0:half])
            xb_ref[sub_rows(sb), 0:half] = lo.astype(BF16)
            xb_ref[sub_rows(sb), half:d] = hi.astype(BF16)
            return 0

        def init(sb, _):
            acc_ref[sub_rows(sb), :] = jnp.broadcast_to(bdn_ref[...], (MOE_SUB, d))
            return 0

        lax.fori_loop(0, nsub, x_start, 0)
        lax.fori_loop(0, nsub, x_wait, 0)
        lax.fori_loop(0, nsub, unpack, 0)
        lax.fori_loop(0, jnp.maximum(nsub, MOE_BIG // MOE_SUB), init, 0)

    @pl.when(nsub > 0)
    def _():
        lane = lax.broadcasted_iota(jnp.int32, (MOE_SUB, LANES), 1)
        low = lane < LANES // 2
        idx = jnp.where(low, 2 * lane, 2 * lane - (LANES - 1))

        def activation(gu):
            glus, lins = [], []
            for j in range(0, 2 * MOE_FC, 2 * LANES):
                pa = jnp.take_along_axis(gu[:, j:j + LANES], idx, axis=1)
                pb = jnp.take_along_axis(gu[:, j + LANES:j + 2 * LANES], idx, axis=1)
                glus.append(jnp.where(low, pa, pltpu.roll(pb, LANES // 2, 1)))
                lins.append(jnp.where(low, pltpu.roll(pa, LANES // 2, 1), pb))
            glu = jnp.minimum(jnp.concatenate(glus, axis=1), SWIGLU_LIMIT)
            lin = jnp.clip(jnp.concatenate(lins, axis=1), -SWIGLU_LIMIT, SWIGLU_LIMIT)
            return (glu * jax.nn.sigmoid(SWIGLU_ALPHA * glu) * (lin + 1.0)).astype(BF16)

        def first_up(r):
            g = None
            for k0 in range(0, d, MOE_KC):
                wgb_ref[k0:k0 + MOE_KC, :] = wgu_ref[k0:k0 + MOE_KC, :].astype(BF16)
                part = jnp.dot(xb_ref[r, k0:k0 + MOE_KC], wgb_ref[k0:k0 + MOE_KC, :], preferred_element_type=F32)
                g = part if g is None else g + part
            wdb_ref[...] = wdn_ref[...].astype(BF16)
            return g

        def block(r0, rows, cast_first=False):
            tiles = [pl.ds(r0 + t, MOE_SUB) for t in range(0, rows, MOE_SUB)]
            gus = [(first_up(r) if cast_first and t == 0
                    else jnp.dot(xb_ref[r, :], wgb_ref[...], preferred_element_type=F32)) + bgu_ref[...]
                   for t, r in enumerate(tiles)]
            acts = [activation(gu) for gu in gus]
            for r, act in zip(tiles, acts):
                for n0 in range(0, d, MOE_NC):
                    acc_ref[r, n0:n0 + MOE_NC] += jnp.dot(
                        act, wdb_ref[:, n0:n0 + MOE_NC], preferred_element_type=F32)

        block(0, MOE_BIG, cast_first=True)
        per_big = MOE_BIG // MOE_SUB
        nbig = lax.shift_right_logical(jnp.maximum(nsub, per_big), per_big.bit_length() - 1)

        def big(b, _):
            block(pl.multiple_of(b * MOE_BIG, MOE_BIG), MOE_BIG)
            return 0

        def small(sb, _):
            block(pl.multiple_of(sb * MOE_SUB, MOE_SUB), MOE_SUB)
            return 0

        lax.fori_loop(1, nbig, big, 0)
        lax.fori_loop(nbig * per_big, nsub, small, 0)

    @pl.when(f == nf - 1)
    def _():
        def out_copy(sb):
            r0 = pl.multiple_of(sb * MOE_SUB, MOE_SUB)
            return pltpu.make_async_copy(acc_ref.at[pl.ds(r0, MOE_SUB), :],
                                         ys_ref.at[pl.ds(start + r0, MOE_SUB), :], sem_out)

        def out_start(sb, _):
            out_copy(sb).start()
            return 0

        def out_wait(sb, _):
            out_copy(sb).wait()
            return 0

        lax.fori_loop(0, nsub, out_start, 0)
        lax.fori_loop(0, nsub, out_wait, 0)

    @pl.when((i == pl.num_programs(0) - 1) & (f == nf - 1))
    def _():
        first = lax.shift_right_logical(itail_ref[0], MOE_SUB.bit_length() - 1)
        last = ys_ref.shape[0] // MOE_SUB
        acc_ref[0:MOE_SUB, :] = jnp.zeros((MOE_SUB, d), F32)

        def tail_copy(sb):
            r0 = pl.multiple_of(sb * MOE_SUB, MOE_SUB)
            return pltpu.make_async_copy(acc_ref.at[0:MOE_SUB, :], ys_ref.at[pl.ds(r0, MOE_SUB), :], sem_out)

        def tail_start(sb, _):
            tail_copy(sb).start()
            return 0

        def tail_wait(sb, _):
            tail_copy(sb).wait()
            return 0

        lax.fori_loop(first, last, tail_start, 0)
        lax.fori_loop(first, last, tail_wait, 0)


def _experts(item_exp, item_start, item_nsub, item_tail, xs, w_gu, b_gu, w_dn, b_dn, n_rows):
    n_items = item_exp.shape[0]
    d = D_MODEL
    nf = D_EXPERT // MOE_FC
    grid_spec = pltpu.PrefetchScalarGridSpec(
        num_scalar_prefetch=4,
        grid=(n_items, nf),
        in_specs=[
            pl.BlockSpec(memory_space=pl.ANY),
            pl.BlockSpec((None, d, 2 * MOE_FC), lambda i, f, e, s, n, t: (e[i], 0, f)),
            pl.BlockSpec((None, 1, 2 * MOE_FC), lambda i, f, e, s, n, t: (e[i], 0, f)),
            pl.BlockSpec((None, MOE_FC, d), lambda i, f, e, s, n, t: (e[i], f, 0)),
            pl.BlockSpec((None, 1, d), lambda i, f, e, s, n, t: (e[i], 0, 0)),
        ],
        out_specs=pl.BlockSpec(memory_space=pl.ANY),
        scratch_shapes=[
            pltpu.VMEM((MOE_RMAX, d), BF16),
            pltpu.VMEM((MOE_RMAX, d), F32),
            pltpu.VMEM((d, 2 * MOE_FC), BF16),
            pltpu.VMEM((MOE_FC, d), BF16),
            pltpu.SemaphoreType.DMA(()),
            pltpu.SemaphoreType.DMA(()),
        ],
    )
    return pl.pallas_call(
        _expert_body,
        grid_spec=grid_spec,
        out_shape=jax.ShapeDtypeStruct((n_rows, d), F32),
        compiler_params=_cparams(("arbitrary", "arbitrary"), vmem=58 * 1024 * 1024),
        name="experts",
    )(item_exp, item_start, item_nsub, item_tail, xs, w_gu, b_gu, w_dn, b_dn)


def _combine_body(dest_ref, ys_ref, x1_ref, wts_ref, o_ref, buf_ref, sem, *, bm):
    def copy(t, k):
        return pltpu.make_async_copy(ys_ref.at[pl.ds(dest_ref[0, 0, k * bm + t], 1), :],
                                     buf_ref.at[k, pl.ds(t, 1), :], sem)

    def start(t, _):
        for k in range(TOP_K):
            copy(t, k).start(priority=k % 2)
        return 0

    def wait(t, _):
        for k in range(TOP_K):
            copy(t, k).wait()
        return 0

    lax.fori_loop(0, bm, start, 0, unroll=8)
    lax.fori_loop(0, bm, wait, 0, unroll=8)
    w = wts_ref[...]
    acc = x1_ref[...]
    for k in range(TOP_K):
        acc = acc + w[:, k:k + 1] * buf_ref[k]
    o_ref[...] = acc


def _combine(dest3, ys, x1, wts, bm):
    m, d = x1.shape
    return pl.pallas_call(
        functools.partial(_combine_body, bm=bm),
        grid=(m // bm,),
        in_specs=[
            pl.BlockSpec((1, 1, bm * TOP_K), lambda i: (i, 0, 0), memory_space=pltpu.SMEM),
            pl.BlockSpec(memory_space=pl.ANY),
            pl.BlockSpec((bm, d), lambda i: (i, 0)),
            pl.BlockSpec((bm, LANES), lambda i: (i, 0)),
        ],
        out_specs=pl.BlockSpec((bm, d), lambda i: (i, 0)),
        out_shape=jax.ShapeDtypeStruct((m, d), F32),
        scratch_shapes=[pltpu.VMEM((TOP_K, bm, d), F32), pltpu.SemaphoreType.DMA(())],
        compiler_params=_cparams(("arbitrary",)),
        name="combine",
    )(dest3, ys, x1, wts)


def _rope_tables(seq):
    half = DA_DQK // 2
    inv = ROPE_THETA ** (-jnp.arange(0, DA_DQK, 2, dtype=F32) / DA_DQK)
    ang = jnp.arange(seq, dtype=F32)[:, None] * inv[None, :]
    cos, sin = jnp.cos(ang), jnp.sin(ang)
    reps = LANES // DA_DQK
    cosf = jnp.tile(jnp.concatenate([cos, cos], axis=1), (1, reps))
    sins = jnp.tile(jnp.concatenate([-sin, sin], axis=1), (1, reps))
    del half
    return cosf, sins


def _moe_tables(ids, ranks, cnt_row):
    t = ids.shape[1]
    counts = cnt_row.astype(jnp.int32)
    nsb = (counts + MOE_SUB - 1) // MOE_SUB
    padded = nsb * MOE_SUB
    pad_end = jnp.cumsum(padded)
    pad_start = pad_end - padded
    experts = jnp.arange(N_EXPERTS, dtype=jnp.int32)[:, None, None]
    dest = ranks + jnp.sum(jnp.where(ids[None] == experts, pad_start[:, None, None], 0), axis=0)
    per_item = MOE_RMAX // MOE_SUB
    items_e = (nsb + per_item - 1) // per_item
    item_end = jnp.cumsum(items_e)
    n_items = (t * TOP_K // MOE_SUB + N_EXPERTS) // per_item + N_EXPERTS
    idx = jnp.arange(n_items, dtype=jnp.int32)
    e_of = jnp.minimum(jnp.searchsorted(item_end, idx, side="right"), N_EXPERTS - 1).astype(jnp.int32)
    local = idx - (item_end[e_of] - items_e[e_of])
    valid = idx < item_end[-1]
    nsub = jnp.where(valid, jnp.clip(nsb[e_of] - local * per_item, 0, per_item), 0).astype(jnp.int32)
    last_e = e_of[jnp.maximum(item_end[-1] - 1, 0)]
    item_exp = jnp.where(valid, e_of, last_e).astype(jnp.int32)
    item_start = jnp.where(valid, pad_start[e_of] + local * MOE_RMAX, 0).astype(jnp.int32)
    return dest.astype(jnp.int32), item_exp, item_start, nsub, pad_end.astype(jnp.int32)


def kernel(x, norm1_g, w_in, ml_gate_bias, ml_norm_g, w_ml_out, da_q_norm_g, da_k_norm_g, da_lambda, da_norm_g,
           w_da_out, w_o, norm2_g, w_router, b_router, w_gate_up, b_gate_up, w_down, b_down):
    batch, seq, d = x.shape
    depth = norm1_g.shape[0]
    tokens = batch * seq
    cosf, sins = _rope_tables(seq)
    gmat = jnp.kron(jnp.eye(LANES // DA_DQK, dtype=F32), jnp.full((DA_DQK, DA_DQK), 1.0 / DA_DQK, F32)).astype(BF16)
    n_rows = tokens * TOP_K + N_EXPERTS * MOE_SUB
    bm_tok = 256

    x2 = x.reshape(tokens, d)
    for l in range(depth):
        lam_init = 0.8 - 0.6 * math.exp(-0.3 * l)
        w = w_in[l]
        w_lo = w[:, :_OFF_MG].astype(BF16)
        w_hi = w[:, _N_MG:].astype(BF16)
        w_gate = w[:, _OFF_MG:_OFF_MG + LANES].astype(BF16)
        gbias = jnp.pad(ml_gate_bias[l].reshape(1, _N_MG), ((0, 0), (0, LANES - _N_MG)))
        proj, gates = _inproj(x2, norm1_g[l].reshape(1, d), w_lo, w_hi, w_gate, gbias)

        gp = _gateprep(gates, ML_CHUNK)
        grow = gp.reshape(4, ML_HEADS, 1, tokens)
        gcol = gp.reshape(4, ML_HEADS, batch, seq).transpose(2, 1, 3, 0)
        hm = _mlstm(proj, grow, gcol, ml_norm_g[l].reshape(1, ML_DV), batch, seq, ML_CHUNK)

        gq = jnp.tile(da_q_norm_g[l], LANES // DA_DQK).reshape(1, LANES)
        gk = jnp.tile(da_k_norm_g[l], LANES // DA_DQK).reshape(1, LANES)
        qr, kr = _qkprep(proj, cosf, sins, gq, gk, gmat, seq)
        hd = _attn(qr, kr, proj, da_lambda[l], da_norm_g[l].reshape(DA_DV, 1), batch, seq, lam_init)

        wr = jnp.pad(w_router[l], ((0, 0), (0, LANES - N_EXPERTS)))
        br = jnp.pad(b_router[l].reshape(1, N_EXPERTS), ((0, 0), (0, LANES - N_EXPERTS)))
        x1, h2, logits = _post(hm, hd, proj, x2, w_ml_out[l].astype(BF16), w_da_out[l].astype(BF16),
                               w_o[l].astype(BF16), norm2_g[l].reshape(1, d), wr, br)

        ids, ranks, wts, cnt = _route(logits)
        dest, item_exp, item_start, item_nsub, pad_end = _moe_tables(ids[:TOP_K], ranks[:TOP_K], cnt[0, :N_EXPERTS])
        dest3 = dest.reshape(TOP_K, tokens // bm_tok, bm_tok).transpose(1, 0, 2).reshape(
            tokens // bm_tok, 1, bm_tok * TOP_K)
        xs = _dispatch(pad_end, dest3, h2, n_rows, bm_tok)
        ys = _experts(item_exp, item_start, item_nsub, pad_end[-1:], xs, w_gate_up[l],
                      b_gate_up[l].reshape(N_EXPERTS, 1, 2 * D_EXPERT), w_down[l],
                      b_down[l].reshape(N_EXPERTS, 1, d), n_rows)
        x2 = _combine(dest3, ys, x1, wts, bm_tok)
    return x2.reshape(batch, seq, d)
```

```python
import functools
import math

import jax
import jax.numpy as jnp
from jax import lax
from jax.experimental import pallas as pl
from jax.experimental.pallas import tpu as pltpu

F32 = jnp.float32
BF16 = jnp.bfloat16

D_MODEL = 2048
ML_HEADS = 4
ML_DQK = 128
ML_DV = 256
ML_W = ML_HEADS * ML_DV
DA_HEADS = 8
DA_DQK = 64
DA_DV = 2 * DA_DQK
DA_W = DA_HEADS * DA_DV
ROPE_THETA = 10000.0
N_EXPERTS = 32
TOP_K = 4
D_EXPERT = D_MODEL
SWIGLU_LIMIT = 7.0
SWIGLU_ALPHA = 1.702
EPS = 1e-6
LOG2E = 1.4426950408889634

LANES = 128
VMEM_LIMIT = 48 * 1024 * 1024

_OFF_MG = 2 * ML_HEADS * ML_DQK + 2 * ML_W
_N_MG = 4 * ML_HEADS
_C_MQ, _C_MK, _C_MV, _C_MO = 0, 512, 1024, 2048
_C_DQ, _C_DK, _C_DV, _C_GM, _C_GD = 3072, 4096, 5120, 6144, 8192
_N_MAIN = 10240

ML_CHUNK = 256
AT_QB = 512
AT_KB = 256
AT_SAFE_BOUND = 60.0
MOE_SUB = 256
MOE_RMAX = 2048
MOE_BIG = 1024
MOE_TILE = 256
MOE_WSPLIT = 1
MOE_FC = 256
MOE_NC = 512
MOE_KC = 256


def _cparams(sem, vmem=VMEM_LIMIT):
    return pltpu.CompilerParams(dimension_semantics=sem, vmem_limit_bytes=vmem)


def _pack_halves(x):
    n = x.shape[1] // 2
    lo = lax.bitcast_convert_type(x[:, :n].astype(BF16).astype(F32), jnp.uint32)
    hi = lax.bitcast_convert_type(x[:, n:].astype(BF16).astype(F32), jnp.uint32)
    return lax.bitcast_convert_type(lax.shift_right_logical(lo, jnp.uint32(16)) | hi, F32)


def _unpack_halves(w):
    u = lax.bitcast_convert_type(w, jnp.uint32)
    lo = lax.bitcast_convert_type(lax.shift_left(u, jnp.uint32(16)), F32)
    hi = lax.bitcast_convert_type(u & jnp.uint32(0xFFFF0000), F32)
    return lo, hi


def _inproj_body(x_ref, g_ref, wlo_ref, whi_ref, wg_ref, gb_ref, o_ref, og_ref, xn_ref, *, nlo):
    j = pl.program_id(1)

    @pl.when(j == 0)
    def _():
        x = x_ref[...]
        ms = jnp.mean(x * x, axis=-1, keepdims=True)
        xn = (x * lax.rsqrt(ms + EPS) * g_ref[...]).astype(BF16)
        xn_ref[...] = xn
        gates = jnp.dot(xn, wg_ref[...], preferred_element_type=F32) + gb_ref[...]
        og_ref[...] = gates.T[0:_N_MG, :]

    @pl.when(j < nlo)
    def _():
        o_ref[...] = jnp.dot(xn_ref[...], wlo_ref[...], preferred_element_type=F32).astype(o_ref.dtype)

    @pl.when(j >= nlo)
    def _():
        o_ref[...] = jnp.dot(xn_ref[...], whi_ref[...], preferred_element_type=F32).astype(o_ref.dtype)


def _inproj(x2, g1, w_lo, w_hi, w_gate, gate_bias, bm=1024, bn=1024):
    m, d = x2.shape
    n = w_hi.shape[1]
    nlo = w_lo.shape[1] // bn
    return pl.pallas_call(
        functools.partial(_inproj_body, nlo=nlo),
        grid=(m // bm, n // bn),
        in_specs=[
            pl.BlockSpec((bm, d), lambda i, j: (i, 0)),
            pl.BlockSpec((1, d), lambda i, j: (0, 0)),
            pl.BlockSpec((d, bn), lambda i, j: (0, jnp.minimum(j, nlo - 1))),
            pl.BlockSpec((d, bn), lambda i, j: (0, jnp.maximum(j, nlo))),
            pl.BlockSpec((d, LANES), lambda i, j: (0, 0)),
            pl.BlockSpec((1, LANES), lambda i, j: (0, 0)),
        ],
        out_specs=[
            pl.BlockSpec((bm, bn), lambda i, j: (i, j)),
            pl.BlockSpec((_N_MG, bm), lambda i, j: (0, i)),
        ],
        out_shape=[
            jax.ShapeDtypeStruct((m, n), BF16),
            jax.ShapeDtypeStruct((_N_MG, m), F32),
        ],
        scratch_shapes=[pltpu.VMEM((bm, d), BF16)],
        compiler_params=_cparams(("parallel", "arbitrary")),
        name="inproj",
    )(x2, g1, w_lo, w_hi, w_gate, gate_bias)


def _gateprep_body(g_ref, o_ref):
    x = g_ref[...]
    c = x.shape[1]
    lf = jnp.minimum(x, 0.0) - jnp.log1p(jnp.exp(-jnp.abs(x)))
    r = lax.broadcasted_iota(jnp.int32, (c, c), 0)
    s = lax.broadcasted_iota(jnp.int32, (c, c), 1)
    upper = (r <= s).astype(F32)
    lower = (r >= s).astype(F32)
    pre = jnp.dot(lf, upper, preferred_element_type=F32, precision=lax.Precision.HIGHEST)
    suf = jnp.dot(lf, lower, preferred_element_type=F32, precision=lax.Precision.HIGHEST)
    row = lax.broadcasted_iota(jnp.int32, x.shape, 0)
    h = ML_HEADS
    out = jnp.where((row >= h) & (row < 2 * h), pre, x)
    out = jnp.where(row >= 3 * h, suf, out)
    o_ref[...] = out


def _gateprep(gt, chunk):
    r, n = gt.shape
    return pl.pallas_call(
        _gateprep_body,
        grid=(n // chunk,),
        in_specs=[pl.BlockSpec((r, chunk), lambda i: (0, i))],
        out_specs=pl.BlockSpec((r, chunk), lambda i: (0, i)),
        out_shape=jax.ShapeDtypeStruct((r, n), F32),
        compiler_params=_cparams(("parallel",)),
        name="gateprep",
    )(gt)


def _mlstm_body(q_ref, k_ref, v_ref, mo_ref, grow_ref, gcol_ref, ng_ref, o_ref,
                kt_ref, vaug_ref, hf_ref, hb_ref, *, chunk):
    s_len = q_ref.shape[0]
    nc = s_len // chunk
    dv = ML_DV
    wide = dv + LANES
    inv_scale = float(ML_DQK) ** 0.5

    def transpose_k(c, _):
        r0 = pl.multiple_of(c * chunk, chunk)
        kt_ref[:, pl.ds(r0, chunk)] = k_ref[pl.ds(r0, chunk), :].astype(F32).T.astype(BF16)
        return 0

    lax.fori_loop(0, nc, transpose_k, 0)

    vaug_ref[:, :dv] = v_ref[...]
    lane = lax.broadcasted_iota(jnp.int32, (s_len, LANES), 1)
    vaug_ref[:, dv:] = jnp.where(lane == 0, 1.0, 0.0).astype(BF16)

    rr = lax.broadcasted_iota(jnp.int32, (chunk, chunk), 0)
    cc = lax.broadcasted_iota(jnp.int32, (chunk, chunk), 1)

    def chunk_step(c, state, m_prev, reverse):
        r0 = pl.multiple_of(c * chunk, chunk)
        gi, gb = (2, 3) if reverse else (0, 1)
        qc = q_ref[pl.ds(r0, chunk), :]
        ktc = kt_ref[:, pl.ds(r0, chunk)]
        vac = vaug_ref[pl.ds(r0, chunk), :]
        i_row = grow_ref[gi, :, pl.ds(r0, chunk)]
        b_row = grow_ref[gb, :, pl.ds(r0, chunk)]
        gcol = gcol_ref[0, pl.ds(r0, chunk), :]
        i_col = gcol[:, gi:gi + 1]
        b_col = gcol[:, gb:gb + 1]
        mask = (cc >= rr) if reverse else (cc <= rr)
        log_d = jnp.where(mask, b_col - (b_row - i_row), -jnp.inf)
        inter = b_col + m_prev
        m_t = jnp.maximum(inter, jnp.max(log_d, axis=1, keepdims=True))
        s_inter = jnp.exp(inter - m_t)
        dm = jnp.exp(log_d - m_t)
        sqk = jnp.dot(qc, ktc, preferred_element_type=F32)
        p = (dm * sqk).astype(BF16)
        nd = (jnp.dot(p, vac, preferred_element_type=F32)
              + s_inter * jnp.dot(qc, state.astype(BF16), preferred_element_type=F32))
        den = nd[:, dv:dv + 1]
        floor = jnp.exp(-m_t) * inv_scale
        h = nd[:, :dv] / jnp.maximum(jnp.abs(den), floor)
        g = b_col[0:1, :] if reverse else b_col[chunk - 1:chunk, :]
        a_col = g - b_col + i_col
        m_new = jnp.maximum(g + m_prev, jnp.max(a_col, axis=0, keepdims=True))
        decay = jnp.exp(g + m_prev - m_new)
        w_col = jnp.exp(a_col - m_new)
        wv = (w_col * vac.astype(F32)).astype(BF16)
        new_state = decay * state + jnp.dot(ktc, wv, preferred_element_type=F32)
        return r0, h, new_state, m_new

    def body(c, carry):
        sf, mf, sb, mb = carry
        r0, h, sf, mf = chunk_step(c, sf, mf, False)
        hf_ref[pl.ds(r0, chunk), :] = h
        r1, h2, sb, mb = chunk_step(nc - 1 - c, sb, mb, True)
        hb_ref[pl.ds(r1, chunk), :] = h2
        return sf, mf, sb, mb

    z = jnp.zeros((ML_DQK, wide), F32)
    m0 = jnp.zeros((1, 1), F32)
    lax.fori_loop(0, nc, body, (z, m0, z, m0))

    def epilogue(c, _):
        r0 = pl.multiple_of(c * chunk, chunk)
        h = hf_ref[pl.ds(r0, chunk), :] + hb_ref[pl.ds(r0, chunk), :]
        ms = jnp.mean(h * h, axis=-1, keepdims=True)
        hn = h * lax.rsqrt(ms + EPS) * ng_ref[...]
        gate = jax.nn.sigmoid(mo_ref[pl.ds(r0, chunk), :].astype(F32))
        o_ref[pl.ds(r0, chunk), :] = (hn * gate).astype(o_ref.dtype)
        return 0

    lax.fori_loop(0, nc, epilogue, 0)


def _mlstm(proj, grow, gcol, ng, batch, seq, chunk):
    m = proj.shape[0]
    return pl.pallas_call(
        functools.partial(_mlstm_body, chunk=chunk),
        grid=(batch, ML_HEADS),
        in_specs=[
            pl.BlockSpec((seq, ML_DQK), lambda b, h: (b, _C_MQ // ML_DQK + h)),
            pl.BlockSpec((seq, ML_DQK), lambda b, h: (b, _C_MK // ML_DQK + h)),
            pl.BlockSpec((seq, ML_DV), lambda b, h: (b, _C_MV // ML_DV + h)),
            pl.BlockSpec((seq, ML_DV), lambda b, h: (b, _C_MO // ML_DV + h)),
            pl.BlockSpec((4, None, 1, seq), lambda b, h: (0, h, 0, b)),
            pl.BlockSpec((None, 1, seq, 4), lambda b, h: (b, h, 0, 0)),
            pl.BlockSpec((1, ML_DV), lambda b, h: (0, 0)),
        ],
        out_specs=pl.BlockSpec((seq, ML_DV), lambda b, h: (b, h)),
        out_shape=jax.ShapeDtypeStruct((m, ML_W), BF16),
        scratch_shapes=[
            pltpu.VMEM((ML_DQK, seq), BF16),
            pltpu.VMEM((seq, ML_DV + LANES), BF16),
            pltpu.VMEM((seq, ML_DV), F32),
            pltpu.VMEM((seq, ML_DV), F32),
        ],
        compiler_params=_cparams(("parallel", "parallel")),
        name="mlstm",
    )(proj, proj, proj, proj, grow, gcol, ng)


def _qkprep_body(q_ref, k_ref, cos_ref, sin_ref, gq_ref, gk_ref, gm_ref, oq_ref, ok_ref, *, qscale):
    cosf = cos_ref[...]
    sins = sin_ref[...]
    lane = lax.broadcasted_iota(jnp.int32, cosf.shape, 1)
    first_half = (lane % DA_DQK) < (DA_DQK // 2)
    gmat = gm_ref[...]

    def one(x_ref, g_ref, o_ref, scale):
        for j in range(x_ref.shape[1] // LANES):
            x = x_ref[:, j * LANES:(j + 1) * LANES].astype(F32)
            ms = jnp.dot((x * x).astype(BF16), gmat, preferred_element_type=F32)
            y = x * lax.rsqrt(ms + EPS) * g_ref[...]
            rot = jnp.where(first_half, pltpu.roll(y, LANES - DA_DQK // 2, 1), pltpu.roll(y, DA_DQK // 2, 1))
            o = y * cosf + rot * sins
            if scale != 1.0:
                o = o * scale
            o_ref[:, j * LANES:(j + 1) * LANES] = o.astype(o_ref.dtype)

    one(q_ref, gq_ref, oq_ref, qscale)
    one(k_ref, gk_ref, ok_ref, 1.0)


def _qkprep(proj, cosf, sins, gq, gk, gmat, seq, bm=512):
    m = proj.shape[0]
    nsb = seq // bm
    return pl.pallas_call(
        functools.partial(_qkprep_body, qscale=float(DA_DQK) ** -0.5 * LOG2E),
        grid=(m // bm,),
        in_specs=[
            pl.BlockSpec((bm, DA_W), lambda i: (i, _C_DQ // DA_W)),
            pl.BlockSpec((bm, DA_W), lambda i: (i, _C_DK // DA_W)),
            pl.BlockSpec((bm, LANES), lambda i: (i % nsb, 0)),
            pl.BlockSpec((bm, LANES), lambda i: (i % nsb, 0)),
            pl.BlockSpec((1, LANES), lambda i: (0, 0)),
            pl.BlockSpec((1, LANES), lambda i: (0, 0)),
            pl.BlockSpec((LANES, LANES), lambda i: (0, 0)),
        ],
        out_specs=[
            pl.BlockSpec((bm, DA_W), lambda i: (i, 0)),
            pl.BlockSpec((bm, DA_W), lambda i: (i, 0)),
        ],
        out_shape=[jax.ShapeDtypeStruct((m, DA_W), BF16)] * 2,
        compiler_params=_cparams(("parallel",)),
        name="qkprep",
    )(proj, proj, cosf, sins, gq, gk, gmat)


def _attn_body(q_ref, k_ref, v_ref, lam_ref, ng_ref, o_ref, vt_ref, kmax_ref, *, kb, lam_init):
    s_len = k_ref.shape[0]

    @pl.when(pl.program_id(2) == 0)
    def _():
        def transpose_v(c, _):
            r0 = pl.multiple_of(c * kb, kb)
            vt_ref[0:DA_DV, pl.ds(r0, kb)] = v_ref[pl.ds(r0, kb), :].astype(F32).T.astype(BF16)
            return 0

        lax.fori_loop(0, s_len // kb, transpose_v, 0)
        vt_ref[DA_DV:, :] = jnp.ones((vt_ref.shape[0] - DA_DV, s_len), BF16)

        gr = lax.broadcasted_iota(jnp.int32, (DA_DV, DA_DV), 0) // DA_DQK
        gc = lax.broadcasted_iota(jnp.int32, (DA_DV, DA_DV), 1) // DA_DQK
        group_sum = jnp.where(gr == gc, 1.0, 0.0).astype(BF16)

        def key_norms(c, mx):
            r0 = pl.multiple_of(c * kb, kb)
            kk = k_ref[pl.ds(r0, kb), :].astype(F32)
            n2 = jnp.dot((kk * kk).astype(BF16), group_sum, preferred_element_type=F32)
            return jnp.maximum(mx, jnp.max(n2, axis=0, keepdims=True))

        mx = lax.fori_loop(0, s_len // kb, key_norms, jnp.zeros((1, DA_DV), F32))
        kmax_ref[...] = jnp.broadcast_to(jnp.sqrt(mx), kmax_ref.shape)

    qt = q_ref[...].astype(F32).T.astype(BF16)
    qb = qt.shape[1]
    row = lax.broadcasted_iota(jnp.int32, qt.shape, 0)
    zero = jnp.zeros_like(qt)
    q1 = jnp.where(row < DA_DQK, qt, zero)
    q2 = jnp.where(row >= DA_DQK, qt, zero)
    nv = vt_ref.shape[0]

    def scores(j):
        kblk = k_ref[j * kb:(j + 1) * kb, :]
        return (jnp.dot(kblk, q1, preferred_element_type=F32),
                jnp.dot(kblk, q2, preferred_element_type=F32))

    def attend(upd, init):
        st1 = st2 = init
        s_cur = scores(0)
        for j in range(s_len // kb):
            s_nxt = scores(j + 1) if (j + 1) * kb < s_len else None
            vb = vt_ref[:, j * kb:(j + 1) * kb]
            st1 = upd(s_cur[0], st1, vb, 0)
            st2 = upd(s_cur[1], st2, vb, 1)
            s_cur = s_nxt
        return st1, st2

    def finish(a1, a2):
        lp = lam_ref[...]
        lam = (jnp.exp(jnp.sum(lp[0:1, :] * lp[1:2, :], axis=1, keepdims=True))
               - jnp.exp(jnp.sum(lp[2:3, :] * lp[3:4, :], axis=1, keepdims=True)) + lam_init)
        o = a1[:DA_DV, :] / a1[DA_DV:DA_DV + 1, :] - lam * (a2[:DA_DV, :] / a2[DA_DV:DA_DV + 1, :])
        ms = jnp.mean(o * o, axis=0, keepdims=True)
        on = o * lax.rsqrt(ms + EPS) * ng_ref[...] * (1.0 - lam_init)
        o_ref[...] = on.T.astype(o_ref.dtype)

    sq = qt.astype(F32) * qt.astype(F32)
    inflate = 1.0 + 2.0 ** -6
    bounds = (jnp.sqrt(jnp.sum(jnp.where(row < DA_DQK, sq, 0.0), axis=0, keepdims=True))
              * kmax_ref[0:1, 0:1] * inflate,
              jnp.sqrt(jnp.sum(jnp.where(row >= DA_DQK, sq, 0.0), axis=0, keepdims=True))
              * kmax_ref[0:1, DA_DQK:DA_DQK + 1] * inflate)
    safe = jnp.max(jnp.maximum(bounds[0], bounds[1])) < AT_SAFE_BOUND

    @pl.when(safe)
    def _():
        def upd(s, acc, vb, which):
            p = jnp.exp2(s - bounds[which]).astype(BF16)
            return acc + jnp.dot(vb, p, preferred_element_type=F32)

        finish(*attend(upd, jnp.zeros((nv, qb), F32)))

    @pl.when(jnp.logical_not(safe))
    def _():
        def upd(s, state, vb, which):
            m_old, acc = state
            m_new = jnp.maximum(m_old, jnp.max(s, axis=0, keepdims=True))
            alpha = jnp.exp2(m_old - m_new)
            p = jnp.exp2(s - m_new).astype(BF16)
            return m_new, alpha * acc + jnp.dot(vb, p, preferred_element_type=F32)

        init = (jnp.full((1, qb), -jnp.inf, F32), jnp.zeros((nv, qb), F32))
        (_, a1), (_, a2) = attend(upd, init)
        finish(a1, a2)


def _attn(qr, kr, proj, lam_params, ng_col, batch, seq, lam_init, qb=AT_QB, kb=AT_KB):
    nq = seq // qb
    return pl.pallas_call(
        functools.partial(_attn_body, kb=kb, lam_init=lam_init),
        grid=(batch, DA_HEADS, nq),
        in_specs=[
            pl.BlockSpec((qb, DA_DV), lambda b, h, i: (b * nq + i, h)),
            pl.BlockSpec((seq, DA_DV), lambda b, h, i: (b, h)),
            pl.BlockSpec((seq, DA_DV), lambda b, h, i: (b, _C_DV // DA_DV + h)),
            pl.BlockSpec((4, DA_DQK), lambda b, h, i: (0, 0)),
            pl.BlockSpec((DA_DV, 1), lambda b, h, i: (0, 0)),
        ],
        out_specs=pl.BlockSpec((qb, DA_DV), lambda b, h, i: (b * nq + i, h)),
        out_shape=jax.ShapeDtypeStruct((batch * seq, DA_W), BF16),
        scratch_shapes=[pltpu.VMEM((DA_DV + 8, seq), BF16), pltpu.VMEM((8, DA_DV), F32)],
        compiler_params=_cparams(("parallel", "parallel", "arbitrary")),
        name="diffattn",
    )(qr, kr, proj, lam_params, ng_col)


def _post_body(hm_ref, hd_ref, gm_ref, gd_ref, x_ref, wm_ref, wd_ref, wo_ref, g2_ref, wrh_ref, wrl_ref, br_ref,
               x1_ref, h2_ref, lg_ref, *, tile):
    rows = [slice(t * tile, (t + 1) * tile) for t in range(x_ref.shape[0] // tile)]
    dot = functools.partial(jnp.dot, preferred_element_type=F32)
    ys = [(dot(hm_ref[r, :], wm_ref[...]), dot(hd_ref[r, :], wd_ref[...])) for r in rows]
    mixes = [(jax.nn.sigmoid(gm_ref[r, :].astype(F32)) * ym
              + jax.nn.sigmoid(gd_ref[r, :].astype(F32)) * yd).astype(BF16) for r, (ym, yd) in zip(rows, ys)]
    x1s = [x_ref[r, :] + dot(mix, wo_ref[...]) for r, mix in zip(rows, mixes)]
    for r, x1 in zip(rows, x1s):
        x1_ref[r, :] = x1
        ms = jnp.mean(x1 * x1, axis=-1, keepdims=True)
        h2 = x1 * lax.rsqrt(ms + EPS) * g2_ref[...]
        h2_ref[r, :] = _pack_halves(h2)
        hi = h2.astype(BF16)
        lo = (h2 - hi.astype(F32)).astype(BF16)
        lg_ref[r, :] = (dot(hi, wrh_ref[...]) + dot(hi, wrl_ref[...]) + dot(lo, wrh_ref[...])) + br_ref[...]


def _post(hm, hd, proj, x2, wm, wd, wo, g2, wr, br, bm=512, tile=256):
    m, d = x2.shape
    const = lambda i: (0, 0)
    wr_hi = wr.astype(BF16)
    wr_lo = (wr - wr_hi.astype(F32)).astype(BF16)
    return pl.pallas_call(
        functools.partial(_post_body, tile=tile),
        grid=(m // bm,),
        in_specs=[
            pl.BlockSpec((bm, ML_W), lambda i: (i, 0)),
            pl.BlockSpec((bm, DA_W), lambda i: (i, 0)),
            pl.BlockSpec((bm, d), lambda i: (i, _C_GM // D_MODEL)),
            pl.BlockSpec((bm, d), lambda i: (i, _C_GD // D_MODEL)),
            pl.BlockSpec((bm, d), lambda i: (i, 0)),
            pl.BlockSpec((ML_W, d), const, pipeline_mode=pl.Buffered(1)),
            pl.BlockSpec((DA_W, d), const, pipeline_mode=pl.Buffered(1)),
            pl.BlockSpec((d, d), const, pipeline_mode=pl.Buffered(1)),
            pl.BlockSpec((1, d), const),
            pl.BlockSpec((d, LANES), const),
            pl.BlockSpec((d, LANES), const),
            pl.BlockSpec((1, LANES), const),
        ],
        out_specs=[
            pl.BlockSpec((bm, d), lambda i: (i, 0)),
            pl.BlockSpec((bm, d // 2), lambda i: (i, 0)),
            pl.BlockSpec((bm, LANES), lambda i: (i, 0)),
        ],
        out_shape=[
            jax.ShapeDtypeStruct((m, d), F32),
            jax.ShapeDtypeStruct((m, d // 2), F32),
            jax.ShapeDtypeStruct((m, LANES), F32),
        ],
        compiler_params=_cparams(("parallel",), vmem=62 * 1024 * 1024),
        name="postmix",
    )(hm, hd, proj, proj, x2, wm, wd, wo, g2, wr_hi, wr_lo, br)


def _route_body(lg_ref, ids_ref, rank_ref, wts_ref, cnt_ref, carry_ref):
    @pl.when(pl.program_id(0) == 0)
    def _():
        carry_ref[...] = jnp.zeros_like(carry_ref)

    lg = lg_ref[...]
    bm = lg.shape[0]
    lane = lax.broadcasted_iota(jnp.int32, lg.shape, 1)
    lanef = lane.astype(F32)
    work = jnp.where(lane < N_EXPERTS, lg, -jnp.inf)
    vals, hots = [], []
    ids = jnp.zeros(lg.shape, F32)
    for k in range(TOP_K):
        mx = jnp.max(work, axis=1, keepdims=True)
        idx = jnp.min(jnp.where(work == mx, lanef, float(LANES)), axis=1, keepdims=True)
        hot = lanef == idx
        work = jnp.where(hot, -jnp.inf, work)
        vals.append(mx)
        hots.append(hot)
        ids = jnp.where(lane == k, idx, ids)
    exps = [jnp.exp(v - vals[0]) for v in vals]
    tot = exps[0] + exps[1] + exps[2] + exps[3]
    sel = jnp.zeros(lg.shape, F32)
    for hot in hots:
        sel = jnp.where(hot, 1.0, sel)
    r = lax.broadcasted_iota(jnp.int32, (bm, bm), 0)
    c = lax.broadcasted_iota(jnp.int32, (bm, bm), 1)
    strict = jnp.where(c < r, 1.0, 0.0).astype(BF16)
    cum = jnp.dot(strict, sel.astype(BF16), preferred_element_type=F32) + carry_ref[0:1, :]
    ranks = jnp.zeros(lg.shape, F32)
    wts = jnp.zeros(lg.shape, F32)
    for k in range(TOP_K):
        rk = jnp.sum(jnp.where(hots[k], cum, 0.0), axis=1, keepdims=True)
        ranks = jnp.where(lane == k, rk, ranks)
        wts = jnp.where(lane == k, exps[k] / tot, wts)
    newc = carry_ref[0:1, :] + jnp.sum(sel, axis=0, keepdims=True)
    carry_ref[...] = jnp.broadcast_to(newc, carry_ref.shape)
    ids_ref[...] = ids.T[0:8, :].astype(jnp.int32)
    rank_ref[...] = ranks.T[0:8, :].astype(jnp.int32)
    wts_ref[...] = wts
    cnt_ref[...] = jnp.broadcast_to(newc, cnt_ref.shape)


def _route(logits, bm=512):
    m = logits.shape[0]
    blk = pl.BlockSpec((bm, LANES), lambda i: (i, 0))
    tblk = pl.BlockSpec((8, bm), lambda i: (0, i))
    return pl.pallas_call(
        _route_body,
        grid=(m // bm,),
        in_specs=[blk],
        out_specs=[tblk, tblk, blk, pl.BlockSpec((8, LANES), lambda i: (0, 0))],
        out_shape=[
            jax.ShapeDtypeStruct((8, m), jnp.int32),
            jax.ShapeDtypeStruct((8, m), jnp.int32),
            jax.ShapeDtypeStruct((m, LANES), F32),
            jax.ShapeDtypeStruct((8, LANES), F32),
        ],
        scratch_shapes=[pltpu.VMEM((8, LANES), F32)],
        compiler_params=_cparams(("arbitrary",)),
        name="route",
    )(logits)


def _dispatch_body(pend_ref, dest_ref, h2_ref, xs_ref, zero_ref, sem, zsem, *, bm):
    @pl.when(pl.program_id(0) == 0)
    def _():
        zero_ref[...] = jnp.zeros_like(zero_ref)
        n_sub_total = xs_ref.shape[0] // MOE_SUB
        first_tail = lax.shift_right_logical(pend_ref[N_EXPERTS - 1], MOE_SUB.bit_length() - 1)

        def zcopy(r0):
            return pltpu.make_async_copy(zero_ref, xs_ref.at[pl.ds(pl.multiple_of(r0, MOE_SUB), MOE_SUB), :], zsem)

        def pad_row(e):
            return jnp.maximum(pend_ref[e] - MOE_SUB, 0)

        def nonempty(e):
            return pend_ref[e] > (pend_ref[e - 1] if e else 0)

        for e in range(N_EXPERTS):
            @pl.when(nonempty(e))
            def _():
                zcopy(pad_row(e)).start()

        def tail_start(sb, _):
            zcopy(sb * MOE_SUB).start()
            return 0

        def tail_wait(sb, _):
            zcopy(sb * MOE_SUB).wait()
            return 0

        lax.fori_loop(first_tail, n_sub_total, tail_start, 0)
        for e in range(N_EXPERTS):
            @pl.when(nonempty(e))
            def _():
                zcopy(pad_row(e)).wait()
        lax.fori_loop(first_tail, n_sub_total, tail_wait, 0)

    def copy(t, k):
        return pltpu.make_async_copy(h2_ref.at[pl.ds(t, 1), :],
                                     xs_ref.at[pl.ds(dest_ref[0, 0, k * bm + t], 1), :], sem)

    def start(t, _):
        for k in range(TOP_K):
            copy(t, k).start(priority=k % 2)
        return 0

    def wait(t, _):
        for k in range(TOP_K):
            copy(t, k).wait()
        return 0

    lax.fori_loop(0, bm, start, 0, unroll=8)
    lax.fori_loop(0, bm, wait, 0, unroll=8)


def _dispatch(pad_end, dest3, h2, n_rows, bm):
    m, width = h2.shape
    grid_spec = pltpu.PrefetchScalarGridSpec(
        num_scalar_prefetch=1,
        grid=(m // bm,),
        in_specs=[
            pl.BlockSpec((1, 1, bm * TOP_K), lambda i, p: (i, 0, 0), memory_space=pltpu.SMEM),
            pl.BlockSpec((bm, width), lambda i, p: (i, 0)),
        ],
        out_specs=pl.BlockSpec(memory_space=pl.ANY),
        scratch_shapes=[
            pltpu.VMEM((MOE_SUB, width), h2.dtype),
            pltpu.SemaphoreType.DMA(()),
            pltpu.SemaphoreType.DMA(()),
        ],
    )
    return pl.pallas_call(
        functools.partial(_dispatch_body, bm=bm),
        grid_spec=grid_spec,
        out_shape=jax.ShapeDtypeStruct((n_rows, width), h2.dtype),
        compiler_params=_cparams(("arbitrary",)),
        name="dispatch",
    )(pad_end, dest3, h2)


def _expert_body(iexp_ref, istart_ref, insub_ref, itail_ref, xs_ref, *rest):
    wgu_refs = rest[:MOE_WSPLIT]
    bgu_ref = rest[MOE_WSPLIT]
    wdn_refs = rest[MOE_WSPLIT + 1:2 * MOE_WSPLIT + 1]
    bdn_ref, ys_ref, xb_ref, acc_ref, wgb_ref, wdb_ref, sem_in, sem_out = rest[2 * MOE_WSPLIT + 1:]
    del iexp_ref
    i = pl.program_id(0)
    f = pl.program_id(1)
    nf = pl.num_programs(1)
    nsub = insub_ref[i]
    start = pl.multiple_of(istart_ref[i], MOE_SUB)
    d = acc_ref.shape[1]
    half = d // 2

    def sub_rows(sb):
        return pl.ds(pl.multiple_of(sb * MOE_SUB, MOE_SUB), MOE_SUB)

    @pl.when((i == 0) & (f == 0))
    def _():
        xb_ref[0:MOE_BIG, :] = jnp.zeros((MOE_BIG, d), BF16)

    @pl.when((f == 0) & (nsub > 0))
    def _():
        def x_copy(sb):
            r0 = pl.multiple_of(sb * MOE_SUB, MOE_SUB)
            return pltpu.make_async_copy(xs_ref.at[pl.ds(start + r0, MOE_SUB), :],
                                         acc_ref.at[pl.ds(r0, MOE_SUB), 0:half], sem_in)

        def x_start(sb, _):
            x_copy(sb).start()
            return 0

        def x_wait(sb, _):
            x_copy(sb).wait()
            return 0

        def unpack(sb, _):
            lo, hi = _unpack_halves(acc_ref[sub_rows(sb), 0:half])
            xb_ref[sub_rows(sb), 0:half] = lo.astype(BF16)
            xb_ref[sub_rows(sb), half:d] = hi.astype(BF16)
            return 0

        def init(sb, _):
            acc_ref[sub_rows(sb), :] = jnp.broadcast_to(bdn_ref[...], (MOE_SUB, d))
            return 0

        lax.fori_loop(0, nsub, x_start, 0)
        lax.fori_loop(0, nsub, x_wait, 0)
        lax.fori_loop(0, nsub, unpack, 0)
        lax.fori_loop(0, jnp.maximum(nsub, MOE_BIG // MOE_SUB), init, 0)

    @pl.when(nsub > 0)
    def _():
        def activation(gu):
            lane = lax.broadcasted_iota(jnp.int32, (gu.shape[0], LANES), 1)
            low = lane < LANES // 2
            idx = jnp.where(low, 2 * lane, 2 * lane - (LANES - 1))
            glus, lins = [], []
            for j in range(0, 2 * MOE_FC, 2 * LANES):
                pa = jnp.take_along_axis(gu[:, j:j + LANES], idx, axis=1)
                pb = jnp.take_along_axis(gu[:, j + LANES:j + 2 * LANES], idx, axis=1)
                glus.append(jnp.where(low, pa, pltpu.roll(pb, LANES // 2, 1)))
                lins.append(jnp.where(low, pltpu.roll(pa, LANES // 2, 1), pb))
            glu = jnp.minimum(jnp.concatenate(glus, axis=1), SWIGLU_LIMIT)
            lin = jnp.clip(jnp.concatenate(lins, axis=1), -SWIGLU_LIMIT, SWIGLU_LIMIT)
            return (glu * jax.nn.sigmoid(SWIGLU_ALPHA * glu) * (lin + 1.0)).astype(BF16)

        def first_up(r):
            g = None
            for k0 in range(0, d, MOE_KC):
                slab, off = divmod(k0, d // MOE_WSPLIT)
                wgb_ref[k0:k0 + MOE_KC, :] = wgu_refs[slab][off:off + MOE_KC, :].astype(BF16)
                part = jnp.dot(xb_ref[r, k0:k0 + MOE_KC], wgb_ref[k0:k0 + MOE_KC, :], preferred_element_type=F32)
                g = part if g is None else g + part
            rows_dn = MOE_FC // MOE_WSPLIT
            for q, ref in enumerate(wdn_refs):
                wdb_ref[q * rows_dn:(q + 1) * rows_dn, :] = ref[...].astype(BF16)
            return g

        def block(r0, rows, cast_first=False):
            tile = min(MOE_TILE, rows)
            tiles = [pl.ds(r0 + t, tile) for t in range(0, rows, tile)]
            gus = [(first_up(r) if cast_first and t == 0
                    else jnp.dot(xb_ref[r, :], wgb_ref[...], preferred_element_type=F32)) + bgu_ref[...]
                   for t, r in enumerate(tiles)]
            acts = [activation(gu) for gu in gus]
            for r, act in zip(tiles, acts):
                for n0 in range(0, d, MOE_NC):
                    acc_ref[r, n0:n0 + MOE_NC] += jnp.dot(
                        act, wdb_ref[:, n0:n0 + MOE_NC], preferred_element_type=F32)

        block(0, MOE_BIG, cast_first=True)
        per_big = MOE_BIG // MOE_SUB
        nbig = lax.shift_right_logical(jnp.maximum(nsub, per_big), per_big.bit_length() - 1)

        def big(b, _):
            block(pl.multiple_of(b * MOE_BIG, MOE_BIG), MOE_BIG)
            return 0

        def small(sb, _):
            block(pl.multiple_of(sb * MOE_SUB, MOE_SUB), MOE_SUB)
            return 0

        lax.fori_loop(1, nbig, big, 0)
        lax.fori_loop(nbig * per_big, nsub, small, 0)

    @pl.when(f == nf - 1)
    def _():
        def out_copy(sb):
            r0 = pl.multiple_of(sb * MOE_SUB, MOE_SUB)
            return pltpu.make_async_copy(acc_ref.at[pl.ds(r0, MOE_SUB), :],
                                         ys_ref.at[pl.ds(start + r0, MOE_SUB), :], sem_out)

        def out_start(sb, _):
            out_copy(sb).start()
            return 0

        def out_wait(sb, _):
            out_copy(sb).wait()
            return 0

        lax.fori_loop(0, nsub, out_start, 0)
        lax.fori_loop(0, nsub, out_wait, 0)

    @pl.when((i == pl.num_programs(0) - 1) & (f == nf - 1))
    def _():
        first = lax.shift_right_logical(itail_ref[0], MOE_SUB.bit_length() - 1)
        last = ys_ref.shape[0] // MOE_SUB
        acc_ref[0:MOE_SUB, :] = jnp.zeros((MOE_SUB, d), F32)

        def tail_copy(sb):
            r0 = pl.multiple_of(sb * MOE_SUB, MOE_SUB)
            return pltpu.make_async_copy(acc_ref.at[0:MOE_SUB, :], ys_ref.at[pl.ds(r0, MOE_SUB), :], sem_out)

        def tail_start(sb, _):
            tail_copy(sb).start()
            return 0

        def tail_wait(sb, _):
            tail_copy(sb).wait()
            return 0

        lax.fori_loop(first, last, tail_start, 0)
        lax.fori_loop(first, last, tail_wait, 0)


def _experts(item_exp, item_start, item_nsub, item_tail, xs, w_gu, b_gu, w_dn, b_dn, n_rows):
    n_items = item_exp.shape[0]
    d = D_MODEL
    nf = D_EXPERT // MOE_FC
    grid_spec = pltpu.PrefetchScalarGridSpec(
        num_scalar_prefetch=4,
        grid=(n_items, nf),
        in_specs=(
            [pl.BlockSpec(memory_space=pl.ANY)]
            + [pl.BlockSpec((None, d // MOE_WSPLIT, 2 * MOE_FC),
                            functools.partial(lambda i, f, e, s, n, t, q: (e[i], q, f), q=q))
               for q in range(MOE_WSPLIT)]
            + [pl.BlockSpec((None, 1, 2 * MOE_FC), lambda i, f, e, s, n, t: (e[i], 0, f))]
            + [pl.BlockSpec((None, MOE_FC // MOE_WSPLIT, d),
                            functools.partial(lambda i, f, e, s, n, t, q: (e[i], f * MOE_WSPLIT + q, 0), q=q))
               for q in range(MOE_WSPLIT)]
            + [pl.BlockSpec((None, 1, d), lambda i, f, e, s, n, t: (e[i], 0, 0))]
        ),
        out_specs=pl.BlockSpec(memory_space=pl.ANY),
        scratch_shapes=[
            pltpu.VMEM((MOE_RMAX, d), BF16),
            pltpu.VMEM((MOE_RMAX, d), F32),
            pltpu.VMEM((d, 2 * MOE_FC), BF16),
            pltpu.VMEM((MOE_FC, d), BF16),
            pltpu.SemaphoreType.DMA(()),
            pltpu.SemaphoreType.DMA(()),
        ],
    )
    return pl.pallas_call(
        _expert_body,
        grid_spec=grid_spec,
        out_shape=jax.ShapeDtypeStruct((n_rows, d), F32),
        compiler_params=_cparams(("arbitrary", "arbitrary"), vmem=58 * 1024 * 1024),
        name="experts",
    )(item_exp, item_start, item_nsub, item_tail, xs, *([w_gu] * MOE_WSPLIT), b_gu, *([w_dn] * MOE_WSPLIT), b_dn)


def _combine_body(dest_ref, ys_ref, x1_ref, wts_ref, o_ref, buf_ref, sem, *, bm):
    def copy(t, k):
        return pltpu.make_async_copy(ys_ref.at[pl.ds(dest_ref[0, 0, k * bm + t], 1), :],
                                     buf_ref.at[k, pl.ds(t, 1), :], sem)

    def start(t, _):
        for k in range(TOP_K):
            copy(t, k).start(priority=k % 2)
        return 0

    def wait(t, _):
        for k in range(TOP_K):
            copy(t, k).wait()
        return 0

    lax.fori_loop(0, bm, start, 0, unroll=8)
    lax.fori_loop(0, bm, wait, 0, unroll=8)
    w = wts_ref[...]
    acc = x1_ref[...]
    for k in range(TOP_K):
        acc = acc + w[:, k:k + 1] * buf_ref[k]
    o_ref[...] = acc


def _combine(dest3, ys, x1, wts, bm):
    m, d = x1.shape
    return pl.pallas_call(
        functools.partial(_combine_body, bm=bm),
        grid=(m // bm,),
        in_specs=[
            pl.BlockSpec((1, 1, bm * TOP_K), lambda i: (i, 0, 0), memory_space=pltpu.SMEM),
            pl.BlockSpec(memory_space=pl.ANY),
            pl.BlockSpec((bm, d), lambda i: (i, 0)),
            pl.BlockSpec((bm, LANES), lambda i: (i, 0)),
        ],
        out_specs=pl.BlockSpec((bm, d), lambda i: (i, 0)),
        out_shape=jax.ShapeDtypeStruct((m, d), F32),
        scratch_shapes=[pltpu.VMEM((TOP_K, bm, d), F32), pltpu.SemaphoreType.DMA(())],
        compiler_params=_cparams(("arbitrary",)),
        name="combine",
    )(dest3, ys, x1, wts)


def _rope_tables(seq):
    half = DA_DQK // 2
    inv = ROPE_THETA ** (-jnp.arange(0, DA_DQK, 2, dtype=F32) / DA_DQK)
    ang = jnp.arange(seq, dtype=F32)[:, None] * inv[None, :]
    cos, sin = jnp.cos(ang), jnp.sin(ang)
    reps = LANES // DA_DQK
    cosf = jnp.tile(jnp.concatenate([cos, cos], axis=1), (1, reps))
    sins = jnp.tile(jnp.concatenate([-sin, sin], axis=1), (1, reps))
    del half
    return cosf, sins


def _moe_tables(ids, ranks, cnt_row):
    t = ids.shape[1]
    counts = cnt_row.astype(jnp.int32)
    nsb = (counts + MOE_SUB - 1) // MOE_SUB
    padded = nsb * MOE_SUB
    pad_end = jnp.cumsum(padded)
    pad_start = pad_end - padded
    experts = jnp.arange(N_EXPERTS, dtype=jnp.int32)[:, None, None]
    dest = ranks + jnp.sum(jnp.where(ids[None] == experts, pad_start[:, None, None], 0), axis=0)
    per_item = MOE_RMAX // MOE_SUB
    items_e = (nsb + per_item - 1) // per_item
    item_end = jnp.cumsum(items_e)
    n_items = (t * TOP_K // MOE_SUB + N_EXPERTS * per_item) // per_item
    idx = jnp.arange(n_items, dtype=jnp.int32)
    e_of = jnp.minimum(jnp.searchsorted(item_end, idx, side="right"), N_EXPERTS - 1).astype(jnp.int32)
    local = idx - (item_end[e_of] - items_e[e_of])
    valid = idx < item_end[-1]
    nsub = jnp.where(valid, jnp.clip(nsb[e_of] - local * per_item, 0, per_item), 0).astype(jnp.int32)
    last_e = e_of[jnp.maximum(item_end[-1] - 1, 0)]
    item_exp = jnp.where(valid, e_of, last_e).astype(jnp.int32)
    item_start = jnp.where(valid, pad_start[e_of] + local * MOE_RMAX, 0).astype(jnp.int32)
    return dest.astype(jnp.int32), item_exp, item_start, nsub, pad_end.astype(jnp.int32)


def kernel(x, norm1_g, w_in, ml_gate_bias, ml_norm_g, w_ml_out, da_q_norm_g, da_k_norm_g, da_lambda, da_norm_g,
           w_da_out, w_o, norm2_g, w_router, b_router, w_gate_up, b_gate_up, w_down, b_down):
    batch, seq, d = x.shape
    depth = norm1_g.shape[0]
    tokens = batch * seq
    cosf, sins = _rope_tables(seq)
    gmat = jnp.kron(jnp.eye(LANES // DA_DQK, dtype=F32), jnp.full((DA_DQK, DA_DQK), 1.0 / DA_DQK, F32)).astype(BF16)
    n_rows = tokens * TOP_K + N_EXPERTS * MOE_SUB
    bm_tok = 256

    x2 = x.reshape(tokens, d)
    for l in range(depth):
        lam_init = 0.8 - 0.6 * math.exp(-0.3 * l)
        w = w_in[l]
        w_lo = w[:, :_OFF_MG].astype(BF16)
        w_hi = w[:, _N_MG:].astype(BF16)
        w_gate = w[:, _OFF_MG:_OFF_MG + LANES].astype(BF16)
        gbias = jnp.pad(ml_gate_bias[l].reshape(1, _N_MG), ((0, 0), (0, LANES - _N_MG)))
        proj, gates = _inproj(x2, norm1_g[l].reshape(1, d), w_lo, w_hi, w_gate, gbias)

        gp = _gateprep(gates, ML_CHUNK)
        grow = gp.reshape(4, ML_HEADS, 1, tokens)
        gcol = gp.reshape(4, ML_HEADS, batch, seq).transpose(2, 1, 3, 0)
        hm = _mlstm(proj, grow, gcol, ml_norm_g[l].reshape(1, ML_DV), batch, seq, ML_CHUNK)

        gq = jnp.tile(da_q_norm_g[l], LANES // DA_DQK).reshape(1, LANES)
        gk = jnp.tile(da_k_norm_g[l], LANES // DA_DQK).reshape(1, LANES)
        qr, kr = _qkprep(proj, cosf, sins, gq, gk, gmat, seq)
        hd = _attn(qr, kr, proj, da_lambda[l], da_norm_g[l].reshape(DA_DV, 1), batch, seq, lam_init)

        wr = jnp.pad(w_router[l], ((0, 0), (0, LANES - N_EXPERTS)))
        br = jnp.pad(b_router[l].reshape(1, N_EXPERTS), ((0, 0), (0, LANES - N_EXPERTS)))
        x1, h2, logits = _post(hm, hd, proj, x2, w_ml_out[l].astype(BF16), w_da_out[l].astype(BF16),
                               w_o[l].astype(BF16), norm2_g[l].reshape(1, d), wr, br)

        ids, ranks, wts, cnt = _route(logits)
        dest, item_exp, item_start, item_nsub, pad_end = _moe_tables(ids[:TOP_K], ranks[:TOP_K], cnt[0, :N_EXPERTS])
        dest3 = dest.reshape(TOP_K, tokens // bm_tok, bm_tok).transpose(1, 0, 2).reshape(
            tokens // bm_tok, 1, bm_tok * TOP_K)
        xs = _dispatch(pad_end, dest3, h2, n_rows, bm_tok)
        ys = _experts(item_exp, item_start, item_nsub, pad_end[-1:], xs, w_gate_up[l],
                      b_gate_up[l].reshape(N_EXPERTS, 1, 2 * D_EXPERT), w_down[l],
                      b_down[l].reshape(N_EXPERTS, 1, d), n_rows)
        x2 = _combine(dest3, ys, x1, wts, bm_tok)
    return x2.reshape(batch, seq, d)
```

```python
import functools
import math

import jax
import jax.numpy as jnp
from jax import lax
from jax.experimental import pallas as pl
from jax.experimental.pallas import tpu as pltpu

F32 = jnp.float32
BF16 = jnp.bfloat16

D_MODEL = 2048
ML_HEADS = 4
ML_DQK = 128
ML_DV = 256
ML_W = ML_HEADS * ML_DV
DA_HEADS = 8
DA_DQK = 64
DA_DV = 2 * DA_DQK
DA_W = DA_HEADS * DA_DV
ROPE_THETA = 10000.0
N_EXPERTS = 32
TOP_K = 4
D_EXPERT = D_MODEL
SWIGLU_LIMIT = 7.0
SWIGLU_ALPHA = 1.702
EPS = 1e-6
LOG2E = 1.4426950408889634

LANES = 128
VMEM_LIMIT = 48 * 1024 * 1024

_OFF_MG = 2 * ML_HEADS * ML_DQK + 2 * ML_W
_N_MG = 4 * ML_HEADS
_C_MQ, _C_MK, _C_MV, _C_MO = 0, 512, 1024, 2048
_C_DQ, _C_DK, _C_DV, _C_GM, _C_GD = 3072, 4096, 5120, 6144, 8192
_N_MAIN = 10240

ML_CHUNK = 256
AT_QB = 512
AT_KB = 256
AT_SAFE_BOUND = 60.0
MOE_SUB = 256
MOE_RMAX = 2048
MOE_BIG = 1024
MOE_TILE = 256
MOE_FC = 256
MOE_NC = 512
MOE_KC = 256


def _cparams(sem, vmem=VMEM_LIMIT):
    return pltpu.CompilerParams(dimension_semantics=sem, vmem_limit_bytes=vmem)


def _pack_halves(x):
    n = x.shape[1] // 2
    lo = lax.bitcast_convert_type(x[:, :n].astype(BF16).astype(F32), jnp.uint32)
    hi = lax.bitcast_convert_type(x[:, n:].astype(BF16).astype(F32), jnp.uint32)
    return lax.bitcast_convert_type(lax.shift_right_logical(lo, jnp.uint32(16)) | hi, F32)


def _unpack_halves(w):
    u = lax.bitcast_convert_type(w, jnp.uint32)
    lo = lax.bitcast_convert_type(lax.shift_left(u, jnp.uint32(16)), F32)
    hi = lax.bitcast_convert_type(u & jnp.uint32(0xFFFF0000), F32)
    return lo, hi


def _inproj_body(x_ref, g_ref, wlo_ref, whi_ref, wg_ref, gb_ref, o_ref, og_ref, xn_ref, *, nlo):
    j = pl.program_id(1)

    @pl.when(j == 0)
    def _():
        x = x_ref[...]
        ms = jnp.mean(x * x, axis=-1, keepdims=True)
        xn = (x * lax.rsqrt(ms + EPS) * g_ref[...]).astype(BF16)
        xn_ref[...] = xn
        gates = jnp.dot(xn, wg_ref[...], preferred_element_type=F32) + gb_ref[...]
        og_ref[...] = gates.T[0:_N_MG, :]

    @pl.when(j < nlo)
    def _():
        o_ref[...] = jnp.dot(xn_ref[...], wlo_ref[...], preferred_element_type=F32).astype(o_ref.dtype)

    @pl.when(j >= nlo)
    def _():
        o_ref[...] = jnp.dot(xn_ref[...], whi_ref[...], preferred_element_type=F32).astype(o_ref.dtype)


def _inproj(x2, g1, w_lo, w_hi, w_gate, gate_bias, bm=1024, bn=1024):
    m, d = x2.shape
    n = w_hi.shape[1]
    nlo = w_lo.shape[1] // bn
    return pl.pallas_call(
        functools.partial(_inproj_body, nlo=nlo),
        grid=(m // bm, n // bn),
        in_specs=[
            pl.BlockSpec((bm, d), lambda i, j: (i, 0)),
            pl.BlockSpec((1, d), lambda i, j: (0, 0)),
            pl.BlockSpec((d, bn), lambda i, j: (0, jnp.minimum(j, nlo - 1))),
            pl.BlockSpec((d, bn), lambda i, j: (0, jnp.maximum(j, nlo))),
            pl.BlockSpec((d, LANES), lambda i, j: (0, 0)),
            pl.BlockSpec((1, LANES), lambda i, j: (0, 0)),
        ],
        out_specs=[
            pl.BlockSpec((bm, bn), lambda i, j: (i, j)),
            pl.BlockSpec((_N_MG, bm), lambda i, j: (0, i)),
        ],
        out_shape=[
            jax.ShapeDtypeStruct((m, n), BF16),
            jax.ShapeDtypeStruct((_N_MG, m), F32),
        ],
        scratch_shapes=[pltpu.VMEM((bm, d), BF16)],
        compiler_params=_cparams(("parallel", "arbitrary")),
        name="inproj",
    )(x2, g1, w_lo, w_hi, w_gate, gate_bias)


def _gateprep_body(g_ref, o_ref):
    x = g_ref[...]
    c = x.shape[1]
    lf = jnp.minimum(x, 0.0) - jnp.log1p(jnp.exp(-jnp.abs(x)))
    r = lax.broadcasted_iota(jnp.int32, (c, c), 0)
    s = lax.broadcasted_iota(jnp.int32, (c, c), 1)
    upper = (r <= s).astype(F32)
    lower = (r >= s).astype(F32)
    pre = jnp.dot(lf, upper, preferred_element_type=F32, precision=lax.Precision.HIGHEST)
    suf = jnp.dot(lf, lower, preferred_element_type=F32, precision=lax.Precision.HIGHEST)
    row = lax.broadcasted_iota(jnp.int32, x.shape, 0)
    h = ML_HEADS
    out = jnp.where((row >= h) & (row < 2 * h), pre, x)
    out = jnp.where(row >= 3 * h, suf, out)
    o_ref[...] = out


def _gateprep(gt, chunk):
    r, n = gt.shape
    return pl.pallas_call(
        _gateprep_body,
        grid=(n // chunk,),
        in_specs=[pl.BlockSpec((r, chunk), lambda i: (0, i))],
        out_specs=pl.BlockSpec((r, chunk), lambda i: (0, i)),
        out_shape=jax.ShapeDtypeStruct((r, n), F32),
        compiler_params=_cparams(("parallel",)),
        name="gateprep",
    )(gt)


def _mlstm_body(q_ref, k_ref, v_ref, mo_ref, grow_ref, gcol_ref, ng_ref, o_ref,
                qt_ref, vat_ref, hf_ref, hb_ref, *, chunk):
    s_len = q_ref.shape[0]
    nc = s_len // chunk
    dv = ML_DV
    wide = vat_ref.shape[0]
    inv_scale = float(ML_DQK) ** 0.5

    def transpose_in(c, _):
        r0 = pl.multiple_of(c * chunk, chunk)
        qt_ref[:, pl.ds(r0, chunk)] = q_ref[pl.ds(r0, chunk), :].astype(F32).T.astype(BF16)
        vat_ref[0:dv, pl.ds(r0, chunk)] = v_ref[pl.ds(r0, chunk), :].astype(F32).T.astype(BF16)
        return 0

    lax.fori_loop(0, nc, transpose_in, 0)
    extra = lax.broadcasted_iota(jnp.int32, (wide - dv, s_len), 0)
    vat_ref[dv:wide, :] = jnp.where(extra == 0, 1.0, 0.0).astype(BF16)

    ss = lax.broadcasted_iota(jnp.int32, (chunk, chunk), 0)
    tt = lax.broadcasted_iota(jnp.int32, (chunk, chunk), 1)

    def chunk_step(c, state, m_prev, reverse):
        r0 = pl.multiple_of(c * chunk, chunk)
        gi, gb = (2, 3) if reverse else (0, 1)
        qtc = qt_ref[:, pl.ds(r0, chunk)]
        kc = k_ref[pl.ds(r0, chunk), :]
        vatc = vat_ref[:, pl.ds(r0, chunk)]
        i_row = grow_ref[gi, :, pl.ds(r0, chunk)]
        b_row = grow_ref[gb, :, pl.ds(r0, chunk)]
        gcol = gcol_ref[0, pl.ds(r0, chunk), :]
        c_col = gcol[:, gb:gb + 1] - gcol[:, gi:gi + 1]
        mask = (ss >= tt) if reverse else (ss <= tt)
        log_d = jnp.where(mask, b_row - c_col, -jnp.inf)
        inter = b_row + m_prev
        m_t = jnp.maximum(inter, jnp.max(log_d, axis=0, keepdims=True))
        s_inter = jnp.exp(inter - m_t)
        dm = jnp.exp(log_d - m_t)
        skq = jnp.dot(kc, qtc, preferred_element_type=F32)
        p = (dm * skq).astype(BF16)
        nd = (jnp.dot(vatc, p, preferred_element_type=F32)
              + s_inter * jnp.dot(state.astype(BF16), qtc, preferred_element_type=F32))
        den = nd[dv:dv + 1, :]
        floor = jnp.exp(-m_t) * inv_scale
        h = (nd[:dv, :] / jnp.maximum(jnp.abs(den), floor)).T
        g = b_row[:, 0:1] if reverse else b_row[:, chunk - 1:chunk]
        a_row = g - b_row + i_row
        m_new = jnp.maximum(g + m_prev, jnp.max(a_row, axis=1, keepdims=True))
        decay = jnp.exp(g + m_prev - m_new)
        w_row = jnp.exp(a_row - m_new)
        wv = (vatc.astype(F32) * w_row).astype(BF16)
        new_state = decay * state + jnp.dot(wv, kc, preferred_element_type=F32)
        return r0, h, new_state, m_new

    def body(c, carry):
        sf, mf, sb, mb = carry
        r0, h, sf, mf = chunk_step(c, sf, mf, False)
        hf_ref[pl.ds(r0, chunk), :] = h
        r1, h2, sb, mb = chunk_step(nc - 1 - c, sb, mb, True)
        hb_ref[pl.ds(r1, chunk), :] = h2
        return sf, mf, sb, mb

    z = jnp.zeros((wide, ML_DQK), F32)
    m0 = jnp.zeros((1, 1), F32)
    lax.fori_loop(0, nc, body, (z, m0, z, m0), unroll=4)

    def epilogue(c, _):
        r0 = pl.multiple_of(c * chunk, chunk)
        h = hf_ref[pl.ds(r0, chunk), :] + hb_ref[pl.ds(r0, chunk), :]
        ms = jnp.mean(h * h, axis=-1, keepdims=True)
        hn = h * lax.rsqrt(ms + EPS) * ng_ref[...]
        gate = jax.nn.sigmoid(mo_ref[pl.ds(r0, chunk), :].astype(F32))
        o_ref[pl.ds(r0, chunk), :] = (hn * gate).astype(o_ref.dtype)
        return 0

    lax.fori_loop(0, nc, epilogue, 0)


def _mlstm(proj, grow, gcol, ng, batch, seq, chunk):
    m = proj.shape[0]
    return pl.pallas_call(
        functools.partial(_mlstm_body, chunk=chunk),
        grid=(batch, ML_HEADS),
        in_specs=[
            pl.BlockSpec((seq, ML_DQK), lambda b, h: (b, _C_MQ // ML_DQK + h)),
            pl.BlockSpec((seq, ML_DQK), lambda b, h: (b, _C_MK // ML_DQK + h)),
            pl.BlockSpec((seq, ML_DV), lambda b, h: (b, _C_MV // ML_DV + h)),
            pl.BlockSpec((seq, ML_DV), lambda b, h: (b, _C_MO // ML_DV + h)),
            pl.BlockSpec((4, None, 1, seq), lambda b, h: (0, h, 0, b)),
            pl.BlockSpec((None, 1, seq, 4), lambda b, h: (b, h, 0, 0)),
            pl.BlockSpec((1, ML_DV), lambda b, h: (0, 0)),
        ],
        out_specs=pl.BlockSpec((seq, ML_DV), lambda b, h: (b, h)),
        out_shape=jax.ShapeDtypeStruct((m, ML_W), BF16),
        scratch_shapes=[
            pltpu.VMEM((ML_DQK, seq), BF16),
            pltpu.VMEM((ML_DV + 16, seq), BF16),
            pltpu.VMEM((seq, ML_DV), F32),
            pltpu.VMEM((seq, ML_DV), F32),
        ],
        compiler_params=_cparams(("parallel", "parallel")),
        name="mlstm",
    )(proj, proj, proj, proj, grow, gcol, ng)


def _qkprep_body(q_ref, k_ref, cos_ref, sin_ref, gq_ref, gk_ref, gm_ref, oq_ref, ok_ref, *, qscale):
    cosf = cos_ref[...]
    sins = sin_ref[...]
    lane = lax.broadcasted_iota(jnp.int32, cosf.shape, 1)
    first_half = (lane % DA_DQK) < (DA_DQK // 2)
    gmat = gm_ref[...]

    def one(x_ref, g_ref, o_ref, scale):
        for j in range(x_ref.shape[1] // LANES):
            x = x_ref[:, j * LANES:(j + 1) * LANES].astype(F32)
            ms = jnp.dot((x * x).astype(BF16), gmat, preferred_element_type=F32)
            y = x * lax.rsqrt(ms + EPS) * g_ref[...]
            rot = jnp.where(first_half, pltpu.roll(y, LANES - DA_DQK // 2, 1), pltpu.roll(y, DA_DQK // 2, 1))
            o = y * cosf + rot * sins
            if scale != 1.0:
                o = o * scale
            o_ref[:, j * LANES:(j + 1) * LANES] = o.astype(o_ref.dtype)

    one(q_ref, gq_ref, oq_ref, qscale)
    one(k_ref, gk_ref, ok_ref, 1.0)


def _qkprep(proj, cosf, sins, gq, gk, gmat, seq, bm=512):
    m = proj.shape[0]
    nsb = seq // bm
    return pl.pallas_call(
        functools.partial(_qkprep_body, qscale=float(DA_DQK) ** -0.5 * LOG2E),
        grid=(m // bm,),
        in_specs=[
            pl.BlockSpec((bm, DA_W), lambda i: (i, _C_DQ // DA_W)),
            pl.BlockSpec((bm, DA_W), lambda i: (i, _C_DK // DA_W)),
            pl.BlockSpec((bm, LANES), lambda i: (i % nsb, 0)),
            pl.BlockSpec((bm, LANES), lambda i: (i % nsb, 0)),
            pl.BlockSpec((1, LANES), lambda i: (0, 0)),
            pl.BlockSpec((1, LANES), lambda i: (0, 0)),
            pl.BlockSpec((LANES, LANES), lambda i: (0, 0)),
        ],
        out_specs=[
            pl.BlockSpec((bm, DA_W), lambda i: (i, 0)),
            pl.BlockSpec((bm, DA_W), lambda i: (i, 0)),
        ],
        out_shape=[jax.ShapeDtypeStruct((m, DA_W), BF16)] * 2,
        compiler_params=_cparams(("parallel",)),
        name="qkprep",
    )(proj, proj, cosf, sins, gq, gk, gmat)


def _attn_body(q_ref, k_ref, v_ref, lam_ref, ng_ref, o_ref, vt_ref, kmax_ref, *, kb, lam_init):
    s_len = k_ref.shape[0]

    @pl.when(pl.program_id(2) == 0)
    def _():
        def transpose_v(c, _):
            r0 = pl.multiple_of(c * kb, kb)
            vt_ref[0:DA_DV, pl.ds(r0, kb)] = v_ref[pl.ds(r0, kb), :].astype(F32).T.astype(BF16)
            return 0

        lax.fori_loop(0, s_len // kb, transpose_v, 0)
        vt_ref[DA_DV:, :] = jnp.ones((vt_ref.shape[0] - DA_DV, s_len), BF16)

        gr = lax.broadcasted_iota(jnp.int32, (DA_DV, DA_DV), 0) // DA_DQK
        gc = lax.broadcasted_iota(jnp.int32, (DA_DV, DA_DV), 1) // DA_DQK
        group_sum = jnp.where(gr == gc, 1.0, 0.0).astype(BF16)

        def key_norms(c, mx):
            r0 = pl.multiple_of(c * kb, kb)
            kk = k_ref[pl.ds(r0, kb), :].astype(F32)
            n2 = jnp.dot((kk * kk).astype(BF16), group_sum, preferred_element_type=F32)
            return jnp.maximum(mx, jnp.max(n2, axis=0, keepdims=True))

        mx = lax.fori_loop(0, s_len // kb, key_norms, jnp.zeros((1, DA_DV), F32))
        kmax_ref[...] = jnp.broadcast_to(jnp.sqrt(mx), kmax_ref.shape)

    qt = q_ref[...].astype(F32).T.astype(BF16)
    qb = qt.shape[1]
    row = lax.broadcasted_iota(jnp.int32, qt.shape, 0)
    zero = jnp.zeros_like(qt)
    q1 = jnp.where(row < DA_DQK, qt, zero)
    q2 = jnp.where(row >= DA_DQK, qt, zero)
    nv = vt_ref.shape[0]

    def scores(j):
        kblk = k_ref[j * kb:(j + 1) * kb, :]
        return (jnp.dot(kblk, q1, preferred_element_type=F32),
                jnp.dot(kblk, q2, preferred_element_type=F32))

    def attend(upd, init):
        st1 = st2 = init
        s_cur = scores(0)
        for j in range(s_len // kb):
            s_nxt = scores(j + 1) if (j + 1) * kb < s_len else None
            vb = vt_ref[:, j * kb:(j + 1) * kb]
            st1 = upd(s_cur[0], st1, vb, 0)
            st2 = upd(s_cur[1], st2, vb, 1)
            s_cur = s_nxt
        return st1, st2

    def finish(a1, a2):
        lp = lam_ref[...]
        lam = (jnp.exp(jnp.sum(lp[0:1, :] * lp[1:2, :], axis=1, keepdims=True))
               - jnp.exp(jnp.sum(lp[2:3, :] * lp[3:4, :], axis=1, keepdims=True)) + lam_init)
        o = a1[:DA_DV, :] / a1[DA_DV:DA_DV + 1, :] - lam * (a2[:DA_DV, :] / a2[DA_DV:DA_DV + 1, :])
        ms = jnp.mean(o * o, axis=0, keepdims=True)
        on = o * lax.rsqrt(ms + EPS) * ng_ref[...] * (1.0 - lam_init)
        o_ref[...] = on.T.astype(o_ref.dtype)

    sq = qt.astype(F32) * qt.astype(F32)
    inflate = 1.0 + 2.0 ** -6
    bounds = (jnp.sqrt(jnp.sum(jnp.where(row < DA_DQK, sq, 0.0), axis=0, keepdims=True))
              * kmax_ref[0:1, 0:1] * inflate,
              jnp.sqrt(jnp.sum(jnp.where(row >= DA_DQK, sq, 0.0), axis=0, keepdims=True))
              * kmax_ref[0:1, DA_DQK:DA_DQK + 1] * inflate)
    safe = jnp.max(jnp.maximum(bounds[0], bounds[1])) < AT_SAFE_BOUND

    @pl.when(safe)
    def _():
        def upd(s, acc, vb, which):
            p = jnp.exp2(s - bounds[which]).astype(BF16)
            return acc + jnp.dot(vb, p, preferred_element_type=F32)

        finish(*attend(upd, jnp.zeros((nv, qb), F32)))

    @pl.when(jnp.logical_not(safe))
    def _():
        def upd(s, state, vb, which):
            m_old, acc = state
            m_new = jnp.maximum(m_old, jnp.max(s, axis=0, keepdims=True))
            alpha = jnp.exp2(m_old - m_new)
            p = jnp.exp2(s - m_new).astype(BF16)
            return m_new, alpha * acc + jnp.dot(vb, p, preferred_element_type=F32)

        init = (jnp.full((1, qb), -jnp.inf, F32), jnp.zeros((nv, qb), F32))
        (_, a1), (_, a2) = attend(upd, init)
        finish(a1, a2)


def _attn(qr, kr, proj, lam_params, ng_col, batch, seq, lam_init, qb=AT_QB, kb=AT_KB):
    nq = seq // qb
    return pl.pallas_call(
        functools.partial(_attn_body, kb=kb, lam_init=lam_init),
        grid=(batch, DA_HEADS, nq),
        in_specs=[
            pl.BlockSpec((qb, DA_DV), lambda b, h, i: (b * nq + i, h)),
            pl.BlockSpec((seq, DA_DV), lambda b, h, i: (b, h)),
            pl.BlockSpec((seq, DA_DV), lambda b, h, i: (b, _C_DV // DA_DV + h)),
            pl.BlockSpec((4, DA_DQK), lambda b, h, i: (0, 0)),
            pl.BlockSpec((DA_DV, 1), lambda b, h, i: (0, 0)),
        ],
        out_specs=pl.BlockSpec((qb, DA_DV), lambda b, h, i: (b * nq + i, h)),
        out_shape=jax.ShapeDtypeStruct((batch * seq, DA_W), BF16),
        scratch_shapes=[pltpu.VMEM((DA_DV + 8, seq), BF16), pltpu.VMEM((8, DA_DV), F32)],
        compiler_params=_cparams(("parallel", "parallel", "arbitrary")),
        name="diffattn",
    )(qr, kr, proj, lam_params, ng_col)


def _post_body(hm_ref, hd_ref, gm_ref, gd_ref, x_ref, wm_ref, wd_ref, wo_ref, g2_ref, wrh_ref, wrl_ref, br_ref,
               x1_ref, h2_ref, lg_ref, *, tile):
    rows = [slice(t * tile, (t + 1) * tile) for t in range(x_ref.shape[0] // tile)]
    dot = functools.partial(jnp.dot, preferred_element_type=F32)
    ys = [(dot(hm_ref[r, :], wm_ref[...]), dot(hd_ref[r, :], wd_ref[...])) for r in rows]
    mixes = [(jax.nn.sigmoid(gm_ref[r, :].astype(F32)) * ym
              + jax.nn.sigmoid(gd_ref[r, :].astype(F32)) * yd).astype(BF16) for r, (ym, yd) in zip(rows, ys)]
    x1s = [x_ref[r, :] + dot(mix, wo_ref[...]) for r, mix in zip(rows, mixes)]
    for r, x1 in zip(rows, x1s):
        x1_ref[r, :] = x1
        ms = jnp.mean(x1 * x1, axis=-1, keepdims=True)
        h2 = x1 * lax.rsqrt(ms + EPS) * g2_ref[...]
        h2_ref[r, :] = _pack_halves(h2)
        hi = h2.astype(BF16)
        lo = (h2 - hi.astype(F32)).astype(BF16)
        lg_ref[r, :] = (dot(hi, wrh_ref[...]) + dot(hi, wrl_ref[...]) + dot(lo, wrh_ref[...])) + br_ref[...]


def _post(hm, hd, proj, x2, wm, wd, wo, g2, wr, br, bm=512, tile=256):
    m, d = x2.shape
    const = lambda i: (0, 0)
    wr_hi = wr.astype(BF16)
    wr_lo = (wr - wr_hi.astype(F32)).astype(BF16)
    return pl.pallas_call(
        functools.partial(_post_body, tile=tile),
        grid=(m // bm,),
        in_specs=[
            pl.BlockSpec((bm, ML_W), lambda i: (i, 0)),
            pl.BlockSpec((bm, DA_W), lambda i: (i, 0)),
            pl.BlockSpec((bm, d), lambda i: (i, _C_GM // D_MODEL)),
            pl.BlockSpec((bm, d), lambda i: (i, _C_GD // D_MODEL)),
            pl.BlockSpec((bm, d), lambda i: (i, 0)),
            pl.BlockSpec((ML_W, d), const, pipeline_mode=pl.Buffered(1)),
            pl.BlockSpec((DA_W, d), const, pipeline_mode=pl.Buffered(1)),
            pl.BlockSpec((d, d), const, pipeline_mode=pl.Buffered(1)),
            pl.BlockSpec((1, d), const),
            pl.BlockSpec((d, LANES), const),
            pl.BlockSpec((d, LANES), const),
            pl.BlockSpec((1, LANES), const),
        ],
        out_specs=[
            pl.BlockSpec((bm, d), lambda i: (i, 0)),
            pl.BlockSpec((bm, d // 2), lambda i: (i, 0)),
            pl.BlockSpec((bm, LANES), lambda i: (i, 0)),
        ],
        out_shape=[
            jax.ShapeDtypeStruct((m, d), F32),
            jax.ShapeDtypeStruct((m, d // 2), F32),
            jax.ShapeDtypeStruct((m, LANES), F32),
        ],
        compiler_params=_cparams(("parallel",), vmem=62 * 1024 * 1024),
        name="postmix",
    )(hm, hd, proj, proj, x2, wm, wd, wo, g2, wr_hi, wr_lo, br)


def _route_body(lg_ref, ids_ref, rank_ref, wts_ref, cnt_ref, carry_ref):
    @pl.when(pl.program_id(0) == 0)
    def _():
        carry_ref[...] = jnp.zeros_like(carry_ref)

    lg = lg_ref[...]
    bm = lg.shape[0]
    lane = lax.broadcasted_iota(jnp.int32, lg.shape, 1)
    lanef = lane.astype(F32)
    work = jnp.where(lane < N_EXPERTS, lg, -jnp.inf)
    vals, hots = [], []
    ids = jnp.zeros(lg.shape, F32)
    for k in range(TOP_K):
        mx = jnp.max(work, axis=1, keepdims=True)
        idx = jnp.min(jnp.where(work == mx, lanef, float(LANES)), axis=1, keepdims=True)
        hot = lanef == idx
        work = jnp.where(hot, -jnp.inf, work)
        vals.append(mx)
        hots.append(hot)
        ids = jnp.where(lane == k, idx, ids)
    exps = [jnp.exp(v - vals[0]) for v in vals]
    tot = exps[0] + exps[1] + exps[2] + exps[3]
    sel = jnp.zeros(lg.shape, F32)
    for hot in hots:
        sel = jnp.where(hot, 1.0, sel)
    r = lax.broadcasted_iota(jnp.int32, (bm, bm), 0)
    c = lax.broadcasted_iota(jnp.int32, (bm, bm), 1)
    strict = jnp.where(c < r, 1.0, 0.0).astype(BF16)
    cum = jnp.dot(strict, sel.astype(BF16), preferred_element_type=F32) + carry_ref[0:1, :]
    ranks = jnp.zeros(lg.shape, F32)
    wts = jnp.zeros(lg.shape, F32)
    for k in range(TOP_K):
        rk = jnp.sum(jnp.where(hots[k], cum, 0.0), axis=1, keepdims=True)
        ranks = jnp.where(lane == k, rk, ranks)
        wts = jnp.where(lane == k, exps[k] / tot, wts)
    newc = carry_ref[0:1, :] + jnp.sum(sel, axis=0, keepdims=True)
    carry_ref[...] = jnp.broadcast_to(newc, carry_ref.shape)
    ids_ref[...] = ids.T[0:8, :].astype(jnp.int32)
    rank_ref[...] = ranks.T[0:8, :].astype(jnp.int32)
    wts_ref[...] = wts
    cnt_ref[...] = jnp.broadcast_to(newc, cnt_ref.shape)


def _route(logits, bm=512):
    m = logits.shape[0]
    blk = pl.BlockSpec((bm, LANES), lambda i: (i, 0))
    tblk = pl.BlockSpec((8, bm), lambda i: (0, i))
    return pl.pallas_call(
        _route_body,
        grid=(m // bm,),
        in_specs=[blk],
        out_specs=[tblk, tblk, blk, pl.BlockSpec((8, LANES), lambda i: (0, 0))],
        out_shape=[
            jax.ShapeDtypeStruct((8, m), jnp.int32),
            jax.ShapeDtypeStruct((8, m), jnp.int32),
            jax.ShapeDtypeStruct((m, LANES), F32),
            jax.ShapeDtypeStruct((8, LANES), F32),
        ],
        scratch_shapes=[pltpu.VMEM((8, LANES), F32)],
        compiler_params=_cparams(("arbitrary",)),
        name="route",
    )(logits)


def _dispatch_body(pend_ref, dest_ref, h2_ref, xs_ref, zero_ref, sem, zsem, *, bm):
    @pl.when(pl.program_id(0) == 0)
    def _():
        zero_ref[...] = jnp.zeros_like(zero_ref)
        n_sub_total = xs_ref.shape[0] // MOE_SUB
        first_tail = lax.shift_right_logical(pend_ref[N_EXPERTS - 1], MOE_SUB.bit_length() - 1)

        def zcopy(r0):
            return pltpu.make_async_copy(zero_ref, xs_ref.at[pl.ds(pl.multiple_of(r0, MOE_SUB), MOE_SUB), :], zsem)

        def pad_row(e):
            return jnp.maximum(pend_ref[e] - MOE_SUB, 0)

        def nonempty(e):
            return pend_ref[e] > (pend_ref[e - 1] if e else 0)

        for e in range(N_EXPERTS):
            @pl.when(nonempty(e))
            def _():
                zcopy(pad_row(e)).start()

        def tail_start(sb, _):
            zcopy(sb * MOE_SUB).start()
            return 0

        def tail_wait(sb, _):
            zcopy(sb * MOE_SUB).wait()
            return 0

        lax.fori_loop(first_tail, n_sub_total, tail_start, 0)
        for e in range(N_EXPERTS):
            @pl.when(nonempty(e))
            def _():
                zcopy(pad_row(e)).wait()
        lax.fori_loop(first_tail, n_sub_total, tail_wait, 0)

    def copy(t, k):
        return pltpu.make_async_copy(h2_ref.at[pl.ds(t, 1), :],
                                     xs_ref.at[pl.ds(dest_ref[0, 0, k * bm + t], 1), :], sem)

    def start(t, _):
        for k in range(TOP_K):
            copy(t, k).start(priority=k % 2)
        return 0

    def wait(t, _):
        for k in range(TOP_K):
            copy(t, k).wait()
        return 0

    lax.fori_loop(0, bm, start, 0, unroll=8)
    lax.fori_loop(0, bm, wait, 0, unroll=8)


def _dispatch(pad_end, dest3, h2, n_rows, bm):
    m, width = h2.shape
    grid_spec = pltpu.PrefetchScalarGridSpec(
        num_scalar_prefetch=1,
        grid=(m // bm,),
        in_specs=[
            pl.BlockSpec((1, 1, bm * TOP_K), lambda i, p: (i, 0, 0), memory_space=pltpu.SMEM),
            pl.BlockSpec((bm, width), lambda i, p: (i, 0)),
        ],
        out_specs=pl.BlockSpec(memory_space=pl.ANY),
        scratch_shapes=[
            pltpu.VMEM((MOE_SUB, width), h2.dtype),
            pltpu.SemaphoreType.DMA(()),
            pltpu.SemaphoreType.DMA(()),
        ],
    )
    return pl.pallas_call(
        functools.partial(_dispatch_body, bm=bm),
        grid_spec=grid_spec,
        out_shape=jax.ShapeDtypeStruct((n_rows, width), h2.dtype),
        compiler_params=_cparams(("arbitrary",)),
        name="dispatch",
    )(pad_end, dest3, h2)


def _expert_body(iexp_ref, istart_ref, insub_ref, itail_ref, xs_ref, wgu_ref, bgu_ref, wdn_ref, bdn_ref,
                 ys_ref, xb_ref, acc_ref, wgb_ref, wdb_ref, sem_in, sem_out):
    del iexp_ref
    i = pl.program_id(0)
    f = pl.program_id(1)
    nf = pl.num_programs(1)
    nsub = insub_ref[i]
    start = pl.multiple_of(istart_ref[i], MOE_SUB)
    d = acc_ref.shape[1]
    half = d // 2

    def sub_rows(sb):
        return pl.ds(pl.multiple_of(sb * MOE_SUB, MOE_SUB), MOE_SUB)

    @pl.when((i == 0) & (f == 0))
    def _():
        xb_ref[0:MOE_BIG, :] = jnp.zeros((MOE_BIG, d), BF16)

    @pl.when((f == 0) & (nsub > 0))
    def _():
        def x_copy(sb):
            r0 = pl.multiple_of(sb * MOE_SUB, MOE_SUB)
            return pltpu.make_async_copy(xs_ref.at[pl.ds(start + r0, MOE_SUB), :],
                                         acc_ref.at[pl.ds(r0, MOE_SUB), 0:half], sem_in)

        def x_start(sb, _):
            x_copy(sb).start()
            return 0

        def x_wait(sb, _):
            x_copy(sb).wait()
            return 0

        def unpack(sb, _):
            lo, hi = _unpack_halves(acc_ref[sub_rows(sb), 0:half])
            xb_ref[sub_rows(sb), 0:half] = lo.astype(BF16)
            xb_ref[sub_rows(sb), half:d] = hi.astype(BF16)
            return 0

        def init(sb, _):
            acc_ref[sub_rows(sb), :] = jnp.broadcast_to(bdn_ref[...], (MOE_SUB, d))
            return 0

        lax.fori_loop(0, nsub, x_start, 0)
        lax.fori_loop(0, nsub, x_wait, 0)
        lax.fori_loop(0, nsub, unpack, 0)
        lax.fori_loop(0, jnp.maximum(nsub, MOE_BIG // MOE_SUB), init, 0)

    @pl.when(nsub > 0)
    def _():
        def activation(gu):
            lane = lax.broadcasted_iota(jnp.int32, (gu.shape[0], LANES), 1)
            low = lane < LANES // 2
            idx = jnp.where(low, 2 * lane, 2 * lane - (LANES - 1))
            glus, lins = [], []
            for j in range(0, 2 * MOE_FC, 2 * LANES):
                pa = jnp.take_along_axis(gu[:, j:j + LANES], idx, axis=1)
                pb = jnp.take_along_axis(gu[:, j + LANES:j + 2 * LANES], idx, axis=1)
                glus.append(jnp.where(low, pa, pltpu.roll(pb, LANES // 2, 1)))
                lins.append(jnp.where(low, pltpu.roll(pa, LANES // 2, 1), pb))
            glu = jnp.minimum(jnp.concatenate(glus, axis=1), SWIGLU_LIMIT)
            lin = jnp.clip(jnp.concatenate(lins, axis=1), -SWIGLU_LIMIT, SWIGLU_LIMIT)
            return (glu * jax.nn.sigmoid(SWIGLU_ALPHA * glu) * (lin + 1.0)).astype(BF16)

        def first_up(r):
            g = None
            for k0 in range(0, d, MOE_KC):
                wgb_ref[k0:k0 + MOE_KC, :] = wgu_ref[k0:k0 + MOE_KC, :].astype(BF16)
                part = jnp.dot(xb_ref[r, k0:k0 + MOE_KC], wgb_ref[k0:k0 + MOE_KC, :], preferred_element_type=F32)
                g = part if g is None else g + part
            wdb_ref[...] = wdn_ref[...].astype(BF16)
            return g

        def block(r0, rows, cast_first=False):
            tile = min(MOE_TILE, rows)
            tiles = [pl.ds(r0 + t, tile) for t in range(0, rows, tile)]
            gus = [(first_up(r) if cast_first and t == 0
                    else jnp.dot(xb_ref[r, :], wgb_ref[...], preferred_element_type=F32)) + bgu_ref[...]
                   for t, r in enumerate(tiles)]
            acts = [activation(gu) for gu in gus]
            for r, act in zip(tiles, acts):
                for n0 in range(0, d, MOE_NC):
                    acc_ref[r, n0:n0 + MOE_NC] += jnp.dot(
                        act, wdb_ref[:, n0:n0 + MOE_NC], preferred_element_type=F32)

        block(0, MOE_BIG, cast_first=True)
        per_big = MOE_BIG // MOE_SUB
        nbig = lax.shift_right_logical(jnp.maximum(nsub, per_big), per_big.bit_length() - 1)

        def big(b, _):
            block(pl.multiple_of(b * MOE_BIG, MOE_BIG), MOE_BIG)
            return 0

        def small(sb, _):
            block(pl.multiple_of(sb * MOE_SUB, MOE_SUB), MOE_SUB)
            return 0

        lax.fori_loop(1, nbig, big, 0)
        lax.fori_loop(nbig * per_big, nsub, small, 0)

    @pl.when(f == nf - 1)
    def _():
        def out_copy(sb):
            r0 = pl.multiple_of(sb * MOE_SUB, MOE_SUB)
            return pltpu.make_async_copy(acc_ref.at[pl.ds(r0, MOE_SUB), :],
                                         ys_ref.at[pl.ds(start + r0, MOE_SUB), :], sem_out)

        def out_start(sb, _):
            out_copy(sb).start()
            return 0

        def out_wait(sb, _):
            out_copy(sb).wait()
            return 0

        lax.fori_loop(0, nsub, out_start, 0)
        lax.fori_loop(0, nsub, out_wait, 0)

    @pl.when((i == pl.num_programs(0) - 1) & (f == nf - 1))
    def _():
        first = lax.shift_right_logical(itail_ref[0], MOE_SUB.bit_length() - 1)
        last = ys_ref.shape[0] // MOE_SUB
        acc_ref[0:MOE_SUB, :] = jnp.zeros((MOE_SUB, d), F32)

        def tail_copy(sb):
            r0 = pl.multiple_of(sb * MOE_SUB, MOE_SUB)
            return pltpu.make_async_copy(acc_ref.at[0:MOE_SUB, :], ys_ref.at[pl.ds(r0, MOE_SUB), :], sem_out)

        def tail_start(sb, _):
            tail_copy(sb).start()
            return 0

        def tail_wait(sb, _):
            tail_copy(sb).wait()
            return 0

        lax.fori_loop(first, last, tail_start, 0)
        lax.fori_loop(first, last, tail_wait, 0)


def _experts(item_exp, item_start, item_nsub, item_tail, xs, w_gu, b_gu, w_dn, b_dn, n_rows):
    n_items = item_exp.shape[0]
    d = D_MODEL
    nf = D_EXPERT // MOE_FC
    grid_spec = pltpu.PrefetchScalarGridSpec(
        num_scalar_prefetch=4,
        grid=(n_items, nf),
        in_specs=[
            pl.BlockSpec(memory_space=pl.ANY),
            pl.BlockSpec((None, d, 2 * MOE_FC), lambda i, f, e, s, n, t: (e[i], 0, jnp.where(n[i] > 0, f, nf - 1))),
            pl.BlockSpec((None, 1, 2 * MOE_FC), lambda i, f, e, s, n, t: (e[i], 0, jnp.where(n[i] > 0, f, nf - 1))),
            pl.BlockSpec((None, MOE_FC, d), lambda i, f, e, s, n, t: (e[i], jnp.where(n[i] > 0, f, nf - 1), 0)),
            pl.BlockSpec((None, 1, d), lambda i, f, e, s, n, t: (e[i], 0, 0)),
        ],
        out_specs=pl.BlockSpec(memory_space=pl.ANY),
        scratch_shapes=[
            pltpu.VMEM((MOE_RMAX, d), BF16),
            pltpu.VMEM((MOE_RMAX, d), F32),
            pltpu.VMEM((d, 2 * MOE_FC), BF16),
            pltpu.VMEM((MOE_FC, d), BF16),
            pltpu.SemaphoreType.DMA(()),
            pltpu.SemaphoreType.DMA(()),
        ],
    )
    return pl.pallas_call(
        _expert_body,
        grid_spec=grid_spec,
        out_shape=jax.ShapeDtypeStruct((n_rows, d), F32),
        compiler_params=_cparams(("arbitrary", "arbitrary"), vmem=58 * 1024 * 1024),
        name="experts",
    )(item_exp, item_start, item_nsub, item_tail, xs, w_gu, b_gu, w_dn, b_dn)


def _combine_body(dest_ref, ys_ref, x1_ref, wts_ref, o_ref, buf_ref, sem, *, bm):
    def copy(t, k):
        return pltpu.make_async_copy(ys_ref.at[pl.ds(dest_ref[0, 0, k * bm + t], 1), :],
                                     buf_ref.at[k, pl.ds(t, 1), :], sem)

    def start(t, _):
        for k in range(TOP_K):
            copy(t, k).start(priority=k % 2)
        return 0

    def wait(t, _):
        for k in range(TOP_K):
            copy(t, k).wait()
        return 0

    lax.fori_loop(0, bm, start, 0, unroll=8)
    lax.fori_loop(0, bm, wait, 0, unroll=8)
    w = wts_ref[...]
    acc = x1_ref[...]
    for k in range(TOP_K):
        acc = acc + w[:, k:k + 1] * buf_ref[k]
    o_ref[...] = acc


def _combine(dest3, ys, x1, wts, bm):
    m, d = x1.shape
    return pl.pallas_call(
        functools.partial(_combine_body, bm=bm),
        grid=(m // bm,),
        in_specs=[
            pl.BlockSpec((1, 1, bm * TOP_K), lambda i: (i, 0, 0), memory_space=pltpu.SMEM),
            pl.BlockSpec(memory_space=pl.ANY),
            pl.BlockSpec((bm, d), lambda i: (i, 0)),
            pl.BlockSpec((bm, LANES), lambda i: (i, 0)),
        ],
        out_specs=pl.BlockSpec((bm, d), lambda i: (i, 0)),
        out_shape=jax.ShapeDtypeStruct((m, d), F32),
        scratch_shapes=[pltpu.VMEM((TOP_K, bm, d), F32), pltpu.SemaphoreType.DMA(())],
        compiler_params=_cparams(("arbitrary",)),
        name="combine",
    )(dest3, ys, x1, wts)


def _rope_tables(seq):
    half = DA_DQK // 2
    inv = ROPE_THETA ** (-jnp.arange(0, DA_DQK, 2, dtype=F32) / DA_DQK)
    ang = jnp.arange(seq, dtype=F32)[:, None] * inv[None, :]
    cos, sin = jnp.cos(ang), jnp.sin(ang)
    reps = LANES // DA_DQK
    cosf = jnp.tile(jnp.concatenate([cos, cos], axis=1), (1, reps))
    sins = jnp.tile(jnp.concatenate([-sin, sin], axis=1), (1, reps))
    del half
    return cosf, sins


def _moe_tables(ids, ranks, cnt_row):
    t = ids.shape[1]
    counts = cnt_row.astype(jnp.int32)
    nsb = (counts + MOE_SUB - 1) // MOE_SUB
    padded = nsb * MOE_SUB
    pad_end = jnp.cumsum(padded)
    pad_start = pad_end - padded
    experts = jnp.arange(N_EXPERTS, dtype=jnp.int32)[:, None, None]
    dest = ranks + jnp.sum(jnp.where(ids[None] == experts, pad_start[:, None, None], 0), axis=0)
    per_item = MOE_RMAX // MOE_SUB
    items_e = (nsb + per_item - 1) // per_item
    item_end = jnp.cumsum(items_e)
    n_items = (t * TOP_K // MOE_SUB + N_EXPERTS * per_item) // per_item
    idx = jnp.arange(n_items, dtype=jnp.int32)
    e_of = jnp.minimum(jnp.searchsorted(item_end, idx, side="right"), N_EXPERTS - 1).astype(jnp.int32)
    local = idx - (item_end[e_of] - items_e[e_of])
    valid = idx < item_end[-1]
    nsub = jnp.where(valid, jnp.clip(nsb[e_of] - local * per_item, 0, per_item), 0).astype(jnp.int32)
    last_e = e_of[jnp.maximum(item_end[-1] - 1, 0)]
    item_exp = jnp.where(valid, e_of, last_e).astype(jnp.int32)
    item_start = jnp.where(valid, pad_start[e_of] + local * MOE_RMAX, 0).astype(jnp.int32)
    return dest.astype(jnp.int32), item_exp, item_start, nsub, pad_end.astype(jnp.int32)


def kernel(x, norm1_g, w_in, ml_gate_bias, ml_norm_g, w_ml_out, da_q_norm_g, da_k_norm_g, da_lambda, da_norm_g,
           w_da_out, w_o, norm2_g, w_router, b_router, w_gate_up, b_gate_up, w_down, b_down):
    batch, seq, d = x.shape
    depth = norm1_g.shape[0]
    tokens = batch * seq
    cosf, sins = _rope_tables(seq)
    gmat = jnp.kron(jnp.eye(LANES // DA_DQK, dtype=F32), jnp.full((DA_DQK, DA_DQK), 1.0 / DA_DQK, F32)).astype(BF16)
    n_rows = tokens * TOP_K + N_EXPERTS * MOE_SUB
    bm_tok = 256

    x2 = x.reshape(tokens, d)
    for l in range(depth):
        lam_init = 0.8 - 0.6 * math.exp(-0.3 * l)
        w = w_in[l]
        w_lo = w[:, :_OFF_MG].astype(BF16)
        w_hi = w[:, _N_MG:].astype(BF16)
        w_gate = w[:, _OFF_MG:_OFF_MG + LANES].astype(BF16)
        gbias = jnp.pad(ml_gate_bias[l].reshape(1, _N_MG), ((0, 0), (0, LANES - _N_MG)))
        proj, gates = _inproj(x2, norm1_g[l].reshape(1, d), w_lo, w_hi, w_gate, gbias)

        gp = _gateprep(gates, ML_CHUNK)
        grow = gp.reshape(4, ML_HEADS, 1, tokens)
        gcol = gp.reshape(4, ML_HEADS, batch, seq).transpose(2, 1, 3, 0)
        hm = _mlstm(proj, grow, gcol, ml_norm_g[l].reshape(1, ML_DV), batch, seq, ML_CHUNK)

        gq = jnp.tile(da_q_norm_g[l], LANES // DA_DQK).reshape(1, LANES)
        gk = jnp.tile(da_k_norm_g[l], LANES // DA_DQK).reshape(1, LANES)
        qr, kr = _qkprep(proj, cosf, sins, gq, gk, gmat, seq)
        hd = _attn(qr, kr, proj, da_lambda[l], da_norm_g[l].reshape(DA_DV, 1), batch, seq, lam_init)

        wr = jnp.pad(w_router[l], ((0, 0), (0, LANES - N_EXPERTS)))
        br = jnp.pad(b_router[l].reshape(1, N_EXPERTS), ((0, 0), (0, LANES - N_EXPERTS)))
        x1, h2, logits = _post(hm, hd, proj, x2, w_ml_out[l].astype(BF16), w_da_out[l].astype(BF16),
                               w_o[l].astype(BF16), norm2_g[l].reshape(1, d), wr, br)

        ids, ranks, wts, cnt = _route(logits)
        dest, item_exp, item_start, item_nsub, pad_end = _moe_tables(ids[:TOP_K], ranks[:TOP_K], cnt[0, :N_EXPERTS])
        dest3 = dest.reshape(TOP_K, tokens // bm_tok, bm_tok).transpose(1, 0, 2).reshape(
            tokens // bm_tok, 1, bm_tok * TOP_K)
        xs = _dispatch(pad_end, dest3, h2, n_rows, bm_tok)
        ys = _experts(item_exp, item_start, item_nsub, pad_end[-1:], xs, w_gate_up[l],
                      b_gate_up[l].reshape(N_EXPERTS, 1, 2 * D_EXPERT), w_down[l],
                      b_down[l].reshape(N_EXPERTS, 1, d), n_rows)
        x2 = _combine(dest3, ys, x1, wts, bm_tok)
    return x2.reshape(batch, seq, d)
```

```python
import functools
import math

import jax
import jax.numpy as jnp
from jax import lax
from jax.experimental import pallas as pl
from jax.experimental.pallas import tpu as pltpu

F32 = jnp.float32
BF16 = jnp.bfloat16

D_MODEL = 2048
ML_HEADS = 4
ML_DQK = 128
ML_DV = 256
ML_W = ML_HEADS * ML_DV
DA_HEADS = 8
DA_DQK = 64
DA_DV = 2 * DA_DQK
DA_W = DA_HEADS * DA_DV
ROPE_THETA = 10000.0
N_EXPERTS = 32
TOP_K = 4
D_EXPERT = D_MODEL
SWIGLU_LIMIT = 7.0
SWIGLU_ALPHA = 1.702
EPS = 1e-6
LOG2E = 1.4426950408889634

LANES = 128
VMEM_LIMIT = 48 * 1024 * 1024

_OFF_MG = 2 * ML_HEADS * ML_DQK + 2 * ML_W
_N_MG = 4 * ML_HEADS
_C_MQ, _C_MK, _C_MV, _C_MO = 0, 512, 1024, 2048
_C_DQ, _C_DK, _C_DV, _C_GM, _C_GD = 3072, 4096, 5120, 6144, 8192
_N_MAIN = 10240

ML_CHUNK = 256
AT_QB = 1024
AT_KB = 256
AT_SAFE_BOUND = 60.0
MOE_SUB = 256
MOE_RMAX = 2048
MOE_BIG = 1024
MOE_TILE = 256
MOE_FC = 256
MOE_NC = 512
MOE_KC = 256


def _cparams(sem, vmem=VMEM_LIMIT):
    return pltpu.CompilerParams(dimension_semantics=sem, vmem_limit_bytes=vmem)


def _pack_halves(x):
    n = x.shape[1] // 2
    lo = lax.bitcast_convert_type(x[:, :n].astype(BF16).astype(F32), jnp.uint32)
    hi = lax.bitcast_convert_type(x[:, n:].astype(BF16).astype(F32), jnp.uint32)
    return lax.bitcast_convert_type(lax.shift_right_logical(lo, jnp.uint32(16)) | hi, F32)


def _unpack_halves(w):
    u = lax.bitcast_convert_type(w, jnp.uint32)
    lo = lax.bitcast_convert_type(lax.shift_left(u, jnp.uint32(16)), F32)
    hi = lax.bitcast_convert_type(u & jnp.uint32(0xFFFF0000), F32)
    return lo, hi


def _inproj_body(x_ref, g_ref, wlo_ref, whi_ref, wg_ref, gb_ref, o_ref, og_ref, xn_ref, *, nlo):
    j = pl.program_id(1)

    @pl.when(j == 0)
    def _():
        x = x_ref[...]
        ms = jnp.mean(x * x, axis=-1, keepdims=True)
        xn = (x * lax.rsqrt(ms + EPS) * g_ref[...]).astype(BF16)
        xn_ref[...] = xn
        gates = jnp.dot(xn, wg_ref[...], preferred_element_type=F32) + gb_ref[...]
        og_ref[...] = gates.T[0:_N_MG, :]

    @pl.when(j < nlo)
    def _():
        o_ref[...] = jnp.dot(xn_ref[...], wlo_ref[...], preferred_element_type=F32).astype(o_ref.dtype)

    @pl.when(j >= nlo)
    def _():
        o_ref[...] = jnp.dot(xn_ref[...], whi_ref[...], preferred_element_type=F32).astype(o_ref.dtype)


def _inproj(x2, g1, w_lo, w_hi, w_gate, gate_bias, bm=1024, bn=1024):
    m, d = x2.shape
    n = w_hi.shape[1]
    nlo = w_lo.shape[1] // bn
    return pl.pallas_call(
        functools.partial(_inproj_body, nlo=nlo),
        grid=(m // bm, n // bn),
        in_specs=[
            pl.BlockSpec((bm, d), lambda i, j: (i, 0)),
            pl.BlockSpec((1, d), lambda i, j: (0, 0)),
            pl.BlockSpec((d, bn), lambda i, j: (0, jnp.minimum(j, nlo - 1))),
            pl.BlockSpec((d, bn), lambda i, j: (0, jnp.maximum(j, nlo))),
            pl.BlockSpec((d, LANES), lambda i, j: (0, 0)),
            pl.BlockSpec((1, LANES), lambda i, j: (0, 0)),
        ],
        out_specs=[
            pl.BlockSpec((bm, bn), lambda i, j: (i, j)),
            pl.BlockSpec((_N_MG, bm), lambda i, j: (0, i)),
        ],
        out_shape=[
            jax.ShapeDtypeStruct((m, n), BF16),
            jax.ShapeDtypeStruct((_N_MG, m), F32),
        ],
        scratch_shapes=[pltpu.VMEM((bm, d), BF16)],
        compiler_params=_cparams(("parallel", "arbitrary")),
        name="inproj",
    )(x2, g1, w_lo, w_hi, w_gate, gate_bias)


def _gateprep_body(g_ref, o_ref):
    x = g_ref[...]
    c = x.shape[1]
    lf = jnp.minimum(x, 0.0) - jnp.log1p(jnp.exp(-jnp.abs(x)))
    r = lax.broadcasted_iota(jnp.int32, (c, c), 0)
    s = lax.broadcasted_iota(jnp.int32, (c, c), 1)
    upper = (r <= s).astype(F32)
    lower = (r >= s).astype(F32)
    pre = jnp.dot(lf, upper, preferred_element_type=F32, precision=lax.Precision.HIGHEST)
    suf = jnp.dot(lf, lower, preferred_element_type=F32, precision=lax.Precision.HIGHEST)
    row = lax.broadcasted_iota(jnp.int32, x.shape, 0)
    h = ML_HEADS
    out = jnp.where((row >= h) & (row < 2 * h), pre, x)
    out = jnp.where(row >= 3 * h, suf, out)
    o_ref[...] = out


def _gateprep(gt, chunk):
    r, n = gt.shape
    return pl.pallas_call(
        _gateprep_body,
        grid=(n // chunk,),
        in_specs=[pl.BlockSpec((r, chunk), lambda i: (0, i))],
        out_specs=pl.BlockSpec((r, chunk), lambda i: (0, i)),
        out_shape=jax.ShapeDtypeStruct((r, n), F32),
        compiler_params=_cparams(("parallel",)),
        name="gateprep",
    )(gt)


def _mlstm_body(q_ref, k_ref, v_ref, mo_ref, grow_ref, gcol_ref, ng_ref, o_ref,
                qt_ref, vat_ref, hf_ref, hb_ref, *, chunk):
    s_len = q_ref.shape[0]
    nc = s_len // chunk
    dv = ML_DV
    wide = vat_ref.shape[0]
    inv_scale = float(ML_DQK) ** 0.5

    def transpose_in(c, _):
        r0 = pl.multiple_of(c * chunk, chunk)
        qt_ref[:, pl.ds(r0, chunk)] = q_ref[pl.ds(r0, chunk), :].astype(F32).T.astype(BF16)
        vat_ref[0:dv, pl.ds(r0, chunk)] = v_ref[pl.ds(r0, chunk), :].astype(F32).T.astype(BF16)
        return 0

    lax.fori_loop(0, nc, transpose_in, 0)
    extra = lax.broadcasted_iota(jnp.int32, (wide - dv, s_len), 0)
    vat_ref[dv:wide, :] = jnp.where(extra == 0, 1.0, 0.0).astype(BF16)

    ss = lax.broadcasted_iota(jnp.int32, (chunk, chunk), 0)
    tt = lax.broadcasted_iota(jnp.int32, (chunk, chunk), 1)

    def chunk_step(c, state, m_prev, reverse):
        r0 = pl.multiple_of(c * chunk, chunk)
        gi, gb = (2, 3) if reverse else (0, 1)
        qtc = qt_ref[:, pl.ds(r0, chunk)]
        kc = k_ref[pl.ds(r0, chunk), :]
        vatc = vat_ref[:, pl.ds(r0, chunk)]
        i_row = grow_ref[gi, :, pl.ds(r0, chunk)]
        b_row = grow_ref[gb, :, pl.ds(r0, chunk)]
        gcol = gcol_ref[0, pl.ds(r0, chunk), :]
        c_col = gcol[:, gb:gb + 1] - gcol[:, gi:gi + 1]
        mask = (ss >= tt) if reverse else (ss <= tt)
        log_d = jnp.where(mask, b_row - c_col, -jnp.inf)
        inter = b_row + m_prev
        m_t = jnp.maximum(inter, jnp.max(log_d, axis=0, keepdims=True))
        s_inter = jnp.exp(inter - m_t)
        dm = jnp.exp(log_d - m_t)
        skq = jnp.dot(kc, qtc, preferred_element_type=F32)
        p = (dm * skq).astype(BF16)
        nd = (jnp.dot(vatc, p, preferred_element_type=F32)
              + s_inter * jnp.dot(state.astype(BF16), qtc, preferred_element_type=F32))
        den = nd[dv:dv + 1, :]
        floor = jnp.exp(-m_t) * inv_scale
        h = (nd[:dv, :] / jnp.maximum(jnp.abs(den), floor)).T
        g = b_row[:, 0:1] if reverse else b_row[:, chunk - 1:chunk]
        a_row = g - b_row + i_row
        m_new = jnp.maximum(g + m_prev, jnp.max(a_row, axis=1, keepdims=True))
        decay = jnp.exp(g + m_prev - m_new)
        w_row = jnp.exp(a_row - m_new)
        wv = (vatc.astype(F32) * w_row).astype(BF16)
        new_state = decay * state + jnp.dot(wv, kc, preferred_element_type=F32)
        return r0, h, new_state, m_new

    def body(c, carry):
        sf, mf, sb, mb = carry
        r0, h, sf, mf = chunk_step(c, sf, mf, False)
        hf_ref[pl.ds(r0, chunk), :] = h
        r1, h2, sb, mb = chunk_step(nc - 1 - c, sb, mb, True)
        hb_ref[pl.ds(r1, chunk), :] = h2
        return sf, mf, sb, mb

    z = jnp.zeros((wide, ML_DQK), F32)
    m0 = jnp.zeros((1, 1), F32)
    lax.fori_loop(0, nc, body, (z, m0, z, m0), unroll=4)

    def epilogue(c, _):
        r0 = pl.multiple_of(c * chunk, chunk)
        h = hf_ref[pl.ds(r0, chunk), :] + hb_ref[pl.ds(r0, chunk), :]
        ms = jnp.mean(h * h, axis=-1, keepdims=True)
        hn = h * lax.rsqrt(ms + EPS) * ng_ref[...]
        gate = jax.nn.sigmoid(mo_ref[pl.ds(r0, chunk), :].astype(F32))
        o_ref[pl.ds(r0, chunk), :] = (hn * gate).astype(o_ref.dtype)
        return 0

    lax.fori_loop(0, nc, epilogue, 0)


def _mlstm(proj, grow, gcol, ng, batch, seq, chunk):
    m = proj.shape[0]
    return pl.pallas_call(
        functools.partial(_mlstm_body, chunk=chunk),
        grid=(batch, ML_HEADS),
        in_specs=[
            pl.BlockSpec((seq, ML_DQK), lambda b, h: (b, _C_MQ // ML_DQK + h)),
            pl.BlockSpec((seq, ML_DQK), lambda b, h: (b, _C_MK // ML_DQK + h)),
            pl.BlockSpec((seq, ML_DV), lambda b, h: (b, _C_MV // ML_DV + h)),
            pl.BlockSpec((seq, ML_DV), lambda b, h: (b, _C_MO // ML_DV + h)),
            pl.BlockSpec((4, None, 1, seq), lambda b, h: (0, h, 0, b)),
            pl.BlockSpec((None, 1, seq, 4), lambda b, h: (b, h, 0, 0)),
            pl.BlockSpec((1, ML_DV), lambda b, h: (0, 0)),
        ],
        out_specs=pl.BlockSpec((seq, ML_DV), lambda b, h: (b, h)),
        out_shape=jax.ShapeDtypeStruct((m, ML_W), BF16),
        scratch_shapes=[
            pltpu.VMEM((ML_DQK, seq), BF16),
            pltpu.VMEM((ML_DV + 16, seq), BF16),
            pltpu.VMEM((seq, ML_DV), F32),
            pltpu.VMEM((seq, ML_DV), F32),
        ],
        compiler_params=_cparams(("parallel", "parallel")),
        name="mlstm",
    )(proj, proj, proj, proj, grow, gcol, ng)


def _qkprep_body(q_ref, k_ref, cos_ref, sin_ref, gq_ref, gk_ref, gm_ref, oq_ref, ok_ref, *, qscale):
    cosf = cos_ref[...]
    sins = sin_ref[...]
    lane = lax.broadcasted_iota(jnp.int32, cosf.shape, 1)
    first_half = (lane % DA_DQK) < (DA_DQK // 2)
    gmat = gm_ref[...]

    def one(x_ref, g_ref, o_ref, scale):
        for j in range(x_ref.shape[1] // LANES):
            x = x_ref[:, j * LANES:(j + 1) * LANES].astype(F32)
            ms = jnp.dot((x * x).astype(BF16), gmat, preferred_element_type=F32)
            y = x * lax.rsqrt(ms + EPS) * g_ref[...]
            rot = jnp.where(first_half, pltpu.roll(y, LANES - DA_DQK // 2, 1), pltpu.roll(y, DA_DQK // 2, 1))
            o = y * cosf + rot * sins
            if scale != 1.0:
                o = o * scale
            o_ref[:, j * LANES:(j + 1) * LANES] = o.astype(o_ref.dtype)

    one(q_ref, gq_ref, oq_ref, qscale)
    one(k_ref, gk_ref, ok_ref, 1.0)


def _qkprep(proj, cosf, sins, gq, gk, gmat, seq, bm=512):
    m = proj.shape[0]
    nsb = seq // bm
    return pl.pallas_call(
        functools.partial(_qkprep_body, qscale=float(DA_DQK) ** -0.5 * LOG2E),
        grid=(m // bm,),
        in_specs=[
            pl.BlockSpec((bm, DA_W), lambda i: (i, _C_DQ // DA_W)),
            pl.BlockSpec((bm, DA_W), lambda i: (i, _C_DK // DA_W)),
            pl.BlockSpec((bm, LANES), lambda i: (i % nsb, 0)),
            pl.BlockSpec((bm, LANES), lambda i: (i % nsb, 0)),
            pl.BlockSpec((1, LANES), lambda i: (0, 0)),
            pl.BlockSpec((1, LANES), lambda i: (0, 0)),
            pl.BlockSpec((LANES, LANES), lambda i: (0, 0)),
        ],
        out_specs=[
            pl.BlockSpec((bm, DA_W), lambda i: (i, 0)),
            pl.BlockSpec((bm, DA_W), lambda i: (i, 0)),
        ],
        out_shape=[jax.ShapeDtypeStruct((m, DA_W), BF16)] * 2,
        compiler_params=_cparams(("parallel",)),
        name="qkprep",
    )(proj, proj, cosf, sins, gq, gk, gmat)


def _attn_body(q_ref, k_ref, v_ref, lam_ref, ng_ref, o_ref, vt_ref, kmax_ref, *, kb, lam_init):
    s_len = k_ref.shape[0]

    @pl.when(pl.program_id(2) == 0)
    def _():
        def transpose_v(c, _):
            r0 = pl.multiple_of(c * kb, kb)
            vt_ref[0:DA_DV, pl.ds(r0, kb)] = v_ref[pl.ds(r0, kb), :].astype(F32).T.astype(BF16)
            return 0

        lax.fori_loop(0, s_len // kb, transpose_v, 0)
        vt_ref[DA_DV:, :] = jnp.ones((vt_ref.shape[0] - DA_DV, s_len), BF16)

        gr = lax.broadcasted_iota(jnp.int32, (DA_DV, DA_DV), 0) // DA_DQK
        gc = lax.broadcasted_iota(jnp.int32, (DA_DV, DA_DV), 1) // DA_DQK
        group_sum = jnp.where(gr == gc, 1.0, 0.0).astype(BF16)

        def key_norms(c, mx):
            r0 = pl.multiple_of(c * kb, kb)
            kk = k_ref[pl.ds(r0, kb), :].astype(F32)
            n2 = jnp.dot((kk * kk).astype(BF16), group_sum, preferred_element_type=F32)
            return jnp.maximum(mx, jnp.max(n2, axis=0, keepdims=True))

        mx = lax.fori_loop(0, s_len // kb, key_norms, jnp.zeros((1, DA_DV), F32))
        kmax_ref[...] = jnp.broadcast_to(jnp.sqrt(mx), kmax_ref.shape)

    qt = q_ref[...].astype(F32).T.astype(BF16)
    qb = qt.shape[1]
    row = lax.broadcasted_iota(jnp.int32, qt.shape, 0)
    zero = jnp.zeros_like(qt)
    q1 = jnp.where(row < DA_DQK, qt, zero)
    q2 = jnp.where(row >= DA_DQK, qt, zero)
    nv = vt_ref.shape[0]

    def scores(j):
        kblk = k_ref[j * kb:(j + 1) * kb, :]
        return (jnp.dot(kblk, q1, preferred_element_type=F32),
                jnp.dot(kblk, q2, preferred_element_type=F32))

    def attend(upd, init):
        st1 = st2 = init
        s_cur = scores(0)
        for j in range(s_len // kb):
            s_nxt = scores(j + 1) if (j + 1) * kb < s_len else None
            vb = vt_ref[:, j * kb:(j + 1) * kb]
            st1 = upd(s_cur[0], st1, vb, 0)
            st2 = upd(s_cur[1], st2, vb, 1)
            s_cur = s_nxt
        return st1, st2

    def finish(a1, a2):
        lp = lam_ref[...]
        lam = (jnp.exp(jnp.sum(lp[0:1, :] * lp[1:2, :], axis=1, keepdims=True))
               - jnp.exp(jnp.sum(lp[2:3, :] * lp[3:4, :], axis=1, keepdims=True)) + lam_init)
        o = a1[:DA_DV, :] / a1[DA_DV:DA_DV + 1, :] - lam * (a2[:DA_DV, :] / a2[DA_DV:DA_DV + 1, :])
        ms = jnp.mean(o * o, axis=0, keepdims=True)
        on = o * lax.rsqrt(ms + EPS) * ng_ref[...] * (1.0 - lam_init)
        o_ref[...] = on.T.astype(o_ref.dtype)

    sq = qt.astype(F32) * qt.astype(F32)
    inflate = 1.0 + 2.0 ** -6
    bounds = (jnp.sqrt(jnp.sum(jnp.where(row < DA_DQK, sq, 0.0), axis=0, keepdims=True))
              * kmax_ref[0:1, 0:1] * inflate,
              jnp.sqrt(jnp.sum(jnp.where(row >= DA_DQK, sq, 0.0), axis=0, keepdims=True))
              * kmax_ref[0:1, DA_DQK:DA_DQK + 1] * inflate)
    safe = jnp.max(jnp.maximum(bounds[0], bounds[1])) < AT_SAFE_BOUND

    @pl.when(safe)
    def _():
        def upd(s, acc, vb, which):
            p = jnp.exp2(s - bounds[which]).astype(BF16)
            return acc + jnp.dot(vb, p, preferred_element_type=F32)

        finish(*attend(upd, jnp.zeros((nv, qb), F32)))

    @pl.when(jnp.logical_not(safe))
    def _():
        def upd(s, state, vb, which):
            m_old, acc = state
            m_new = jnp.maximum(m_old, jnp.max(s, axis=0, keepdims=True))
            alpha = jnp.exp2(m_old - m_new)
            p = jnp.exp2(s - m_new).astype(BF16)
            return m_new, alpha * acc + jnp.dot(vb, p, preferred_element_type=F32)

        init = (jnp.full((1, qb), -jnp.inf, F32), jnp.zeros((nv, qb), F32))
        (_, a1), (_, a2) = attend(upd, init)
        finish(a1, a2)


def _attn(qr, kr, proj, lam_params, ng_col, batch, seq, lam_init, qb=AT_QB, kb=AT_KB):
    nq = seq // qb
    return pl.pallas_call(
        functools.partial(_attn_body, kb=kb, lam_init=lam_init),
        grid=(batch, DA_HEADS, nq),
        in_specs=[
            pl.BlockSpec((qb, DA_DV), lambda b, h, i: (b * nq + i, h)),
            pl.BlockSpec((seq, DA_DV), lambda b, h, i: (b, h)),
            pl.BlockSpec((seq, DA_DV), lambda b, h, i: (b, _C_DV // DA_DV + h)),
            pl.BlockSpec((4, DA_DQK), lambda b, h, i: (0, 0)),
            pl.BlockSpec((DA_DV, 1), lambda b, h, i: (0, 0)),
        ],
        out_specs=pl.BlockSpec((qb, DA_DV), lambda b, h, i: (b * nq + i, h)),
        out_shape=jax.ShapeDtypeStruct((batch * seq, DA_W), BF16),
        scratch_shapes=[pltpu.VMEM((DA_DV + 8, seq), BF16), pltpu.VMEM((8, DA_DV), F32)],
        compiler_params=_cparams(("parallel", "parallel", "arbitrary")),
        name="diffattn",
    )(qr, kr, proj, lam_params, ng_col)


def _post_body(hm_ref, hd_ref, gm_ref, gd_ref, x_ref, wm_ref, wd_ref, wo_ref, g2_ref, wrh_ref, wrl_ref, br_ref,
               x1_ref, h2_ref, lg_ref, *, tile):
    rows = [slice(t * tile, (t + 1) * tile) for t in range(x_ref.shape[0] // tile)]
    dot = functools.partial(jnp.dot, preferred_element_type=F32)
    ys = [(dot(hm_ref[r, :], wm_ref[...]), dot(hd_ref[r, :], wd_ref[...])) for r in rows]
    mixes = [(jax.nn.sigmoid(gm_ref[r, :].astype(F32)) * ym
              + jax.nn.sigmoid(gd_ref[r, :].astype(F32)) * yd).astype(BF16) for r, (ym, yd) in zip(rows, ys)]
    x1s = [x_ref[r, :] + dot(mix, wo_ref[...]) for r, mix in zip(rows, mixes)]
    for r, x1 in zip(rows, x1s):
        x1_ref[r, :] = x1
        ms = jnp.mean(x1 * x1, axis=-1, keepdims=True)
        h2 = x1 * lax.rsqrt(ms + EPS) * g2_ref[...]
        h2_ref[r, :] = _pack_halves(h2)
        hi = h2.astype(BF16)
        lo = (h2 - hi.astype(F32)).astype(BF16)
        lg_ref[r, :] = (dot(hi, wrh_ref[...]) + dot(hi, wrl_ref[...]) + dot(lo, wrh_ref[...])) + br_ref[...]


def _post(hm, hd, proj, x2, wm, wd, wo, g2, wr, br, bm=512, tile=256):
    m, d = x2.shape
    const = lambda i: (0, 0)
    wr_hi = wr.astype(BF16)
    wr_lo = (wr - wr_hi.astype(F32)).astype(BF16)
    return pl.pallas_call(
        functools.partial(_post_body, tile=tile),
        grid=(m // bm,),
        in_specs=[
            pl.BlockSpec((bm, ML_W), lambda i: (i, 0)),
            pl.BlockSpec((bm, DA_W), lambda i: (i, 0)),
            pl.BlockSpec((bm, d), lambda i: (i, _C_GM // D_MODEL)),
            pl.BlockSpec((bm, d), lambda i: (i, _C_GD // D_MODEL)),
            pl.BlockSpec((bm, d), lambda i: (i, 0)),
            pl.BlockSpec((ML_W, d), const, pipeline_mode=pl.Buffered(1)),
            pl.BlockSpec((DA_W, d), const, pipeline_mode=pl.Buffered(1)),
            pl.BlockSpec((d, d), const, pipeline_mode=pl.Buffered(1)),
            pl.BlockSpec((1, d), const),
            pl.BlockSpec((d, LANES), const),
            pl.BlockSpec((d, LANES), const),
            pl.BlockSpec((1, LANES), const),
        ],
        out_specs=[
            pl.BlockSpec((bm, d), lambda i: (i, 0)),
            pl.BlockSpec((bm, d // 2), lambda i: (i, 0)),
            pl.BlockSpec((bm, LANES), lambda i: (i, 0)),
        ],
        out_shape=[
            jax.ShapeDtypeStruct((m, d), F32),
            jax.ShapeDtypeStruct((m, d // 2), F32),
            jax.ShapeDtypeStruct((m, LANES), F32),
        ],
        compiler_params=_cparams(("parallel",), vmem=62 * 1024 * 1024),
        name="postmix",
    )(hm, hd, proj, proj, x2, wm, wd, wo, g2, wr_hi, wr_lo, br)


def _route_body(lg_ref, ids_ref, rank_ref, wts_ref, cnt_ref, carry_ref):
    @pl.when(pl.program_id(0) == 0)
    def _():
        carry_ref[...] = jnp.zeros_like(carry_ref)

    lg = lg_ref[...]
    bm = lg.shape[0]
    lane = lax.broadcasted_iota(jnp.int32, lg.shape, 1)
    lanef = lane.astype(F32)
    work = jnp.where(lane < N_EXPERTS, lg, -jnp.inf)
    vals, hots = [], []
    ids = jnp.zeros(lg.shape, F32)
    for k in range(TOP_K):
        mx = jnp.max(work, axis=1, keepdims=True)
        idx = jnp.min(jnp.where(work == mx, lanef, float(LANES)), axis=1, keepdims=True)
        hot = lanef == idx
        work = jnp.where(hot, -jnp.inf, work)
        vals.append(mx)
        hots.append(hot)
        ids = jnp.where(lane == k, idx, ids)
    exps = [jnp.exp(v - vals[0]) for v in vals]
    tot = exps[0] + exps[1] + exps[2] + exps[3]
    sel = jnp.zeros(lg.shape, F32)
    for hot in hots:
        sel = jnp.where(hot, 1.0, sel)
    r = lax.broadcasted_iota(jnp.int32, (bm, bm), 0)
    c = lax.broadcasted_iota(jnp.int32, (bm, bm), 1)
    strict = jnp.where(c < r, 1.0, 0.0).astype(BF16)
    cum = jnp.dot(strict, sel.astype(BF16), preferred_element_type=F32) + carry_ref[0:1, :]
    ranks = jnp.zeros(lg.shape, F32)
    wts = jnp.zeros(lg.shape, F32)
    for k in range(TOP_K):
        rk = jnp.sum(jnp.where(hots[k], cum, 0.0), axis=1, keepdims=True)
        ranks = jnp.where(lane == k, rk, ranks)
        wts = jnp.where(lane == k, exps[k] / tot, wts)
    newc = carry_ref[0:1, :] + jnp.sum(sel, axis=0, keepdims=True)
    carry_ref[...] = jnp.broadcast_to(newc, carry_ref.shape)
    ids_ref[...] = ids.T[0:8, :].astype(jnp.int32)
    rank_ref[...] = ranks.T[0:8, :].astype(jnp.int32)
    wts_ref[...] = wts
    cnt_ref[...] = jnp.broadcast_to(newc, cnt_ref.shape)


def _route(logits, bm=512):
    m = logits.shape[0]
    blk = pl.BlockSpec((bm, LANES), lambda i: (i, 0))
    tblk = pl.BlockSpec((8, bm), lambda i: (0, i))
    return pl.pallas_call(
        _route_body,
        grid=(m // bm,),
        in_specs=[blk],
        out_specs=[tblk, tblk, blk, pl.BlockSpec((8, LANES), lambda i: (0, 0))],
        out_shape=[
            jax.ShapeDtypeStruct((8, m), jnp.int32),
            jax.ShapeDtypeStruct((8, m), jnp.int32),
            jax.ShapeDtypeStruct((m, LANES), F32),
            jax.ShapeDtypeStruct((8, LANES), F32),
        ],
        scratch_shapes=[pltpu.VMEM((8, LANES), F32)],
        compiler_params=_cparams(("arbitrary",)),
        name="route",
    )(logits)


def _dispatch_body(pend_ref, dest_ref, h2_ref, xs_ref, zero_ref, sem, zsem, *, bm):
    @pl.when(pl.program_id(0) == 0)
    def _():
        zero_ref[...] = jnp.zeros_like(zero_ref)
        n_sub_total = xs_ref.shape[0] // MOE_SUB
        first_tail = lax.shift_right_logical(pend_ref[N_EXPERTS - 1], MOE_SUB.bit_length() - 1)

        def zcopy(r0):
            return pltpu.make_async_copy(zero_ref, xs_ref.at[pl.ds(pl.multiple_of(r0, MOE_SUB), MOE_SUB), :], zsem)

        def pad_row(e):
            return jnp.maximum(pend_ref[e] - MOE_SUB, 0)

        def nonempty(e):
            return pend_ref[e] > (pend_ref[e - 1] if e else 0)

        for e in range(N_EXPERTS):
            @pl.when(nonempty(e))
            def _():
                zcopy(pad_row(e)).start()

        def tail_start(sb, _):
            zcopy(sb * MOE_SUB).start()
            return 0

        def tail_wait(sb, _):
            zcopy(sb * MOE_SUB).wait()
            return 0

        lax.fori_loop(first_tail, n_sub_total, tail_start, 0)
        for e in range(N_EXPERTS):
            @pl.when(nonempty(e))
            def _():
                zcopy(pad_row(e)).wait()
        lax.fori_loop(first_tail, n_sub_total, tail_wait, 0)

    def copy(t, k):
        return pltpu.make_async_copy(h2_ref.at[pl.ds(t, 1), :],
                                     xs_ref.at[pl.ds(dest_ref[0, 0, k * bm + t], 1), :], sem)

    def start(t, _):
        for k in range(TOP_K):
            copy(t, k).start(priority=k % 2)
        return 0

    def wait(t, _):
        for k in range(TOP_K):
            copy(t, k).wait()
        return 0

    lax.fori_loop(0, bm, start, 0, unroll=8)
    lax.fori_loop(0, bm, wait, 0, unroll=8)


def _dispatch(pad_end, dest3, h2, n_rows, bm):
    m, width = h2.shape
    grid_spec = pltpu.PrefetchScalarGridSpec(
        num_scalar_prefetch=1,
        grid=(m // bm,),
        in_specs=[
            pl.BlockSpec((1, 1, bm * TOP_K), lambda i, p: (i, 0, 0), memory_space=pltpu.SMEM),
            pl.BlockSpec((bm, width), lambda i, p: (i, 0)),
        ],
        out_specs=pl.BlockSpec(memory_space=pl.ANY),
        scratch_shapes=[
            pltpu.VMEM((MOE_SUB, width), h2.dtype),
            pltpu.SemaphoreType.DMA(()),
            pltpu.SemaphoreType.DMA(()),
        ],
    )
    return pl.pallas_call(
        functools.partial(_dispatch_body, bm=bm),
        grid_spec=grid_spec,
        out_shape=jax.ShapeDtypeStruct((n_rows, width), h2.dtype),
        compiler_params=_cparams(("arbitrary",)),
        name="dispatch",
    )(pad_end, dest3, h2)


def _expert_body(iexp_ref, istart_ref, insub_ref, itail_ref, xs_ref, wgu_ref, bgu_ref, wdn_ref, bdn_ref,
                 ys_ref, xstage_ref, xb_ref, acc_ref, wgb_ref, wdb_ref, sem_in, sem_out):
    del iexp_ref
    i = pl.program_id(0)
    f = pl.program_id(1)
    nf = pl.num_programs(1)
    nsub = insub_ref[i]
    start = pl.multiple_of(istart_ref[i], MOE_SUB)
    d = acc_ref.shape[1]
    half = d // 2

    def sub_rows(sb):
        return pl.ds(pl.multiple_of(sb * MOE_SUB, MOE_SUB), MOE_SUB)

    @pl.when((i == 0) & (f == 0))
    def _():
        xb_ref[0:MOE_BIG, :] = jnp.zeros((MOE_BIG, d), BF16)

    def out_copy(first_row, sb):
        r0 = pl.multiple_of(sb * MOE_SUB, MOE_SUB)
        return pltpu.make_async_copy(acc_ref.at[pl.ds(r0, MOE_SUB), :],
                                     ys_ref.at[pl.ds(first_row + r0, MOE_SUB), :], sem_out)

    @pl.when((f == 0) & (nsub > 0))
    def _():
        def x_copy(sb):
            r0 = pl.multiple_of(sb * MOE_SUB, MOE_SUB)
            return pltpu.make_async_copy(xs_ref.at[pl.ds(start + r0, MOE_SUB), :],
                                         xstage_ref.at[pl.ds(r0, MOE_SUB), :], sem_in)

        def x_start(sb, _):
            x_copy(sb).start()
            return 0

        def x_wait(sb, _):
            x_copy(sb).wait()
            return 0

        def unpack(sb, _):
            lo, hi = _unpack_halves(xstage_ref[sub_rows(sb), :])
            xb_ref[sub_rows(sb), 0:half] = lo.astype(BF16)
            xb_ref[sub_rows(sb), half:d] = hi.astype(BF16)
            return 0

        lax.fori_loop(0, nsub, x_start, 0)
        lax.fori_loop(0, nsub, x_wait, 0)
        lax.fori_loop(0, nsub, unpack, 0)

    @pl.when((f == 0) & (i > 0))
    def _():
        prev = jnp.maximum(i - 1, 0)
        prev_start = pl.multiple_of(istart_ref[prev], MOE_SUB)

        def prev_wait(sb, _):
            out_copy(prev_start, sb).wait()
            return 0

        lax.fori_loop(0, insub_ref[prev], prev_wait, 0)

    @pl.when((f == 0) & (nsub > 0))
    def _():
        def init(sb, _):
            acc_ref[sub_rows(sb), :] = jnp.broadcast_to(bdn_ref[...], (MOE_SUB, d))
            return 0

        lax.fori_loop(0, jnp.maximum(nsub, MOE_BIG // MOE_SUB), init, 0)

    @pl.when(nsub > 0)
    def _():
        def activation(gu):
            lane = lax.broadcasted_iota(jnp.int32, (gu.shape[0], LANES), 1)
            low = lane < LANES // 2
            idx = jnp.where(low, 2 * lane, 2 * lane - (LANES - 1))
            glus, lins = [], []
            for j in range(0, 2 * MOE_FC, 2 * LANES):
                pa = jnp.take_along_axis(gu[:, j:j + LANES], idx, axis=1)
                pb = jnp.take_along_axis(gu[:, j + LANES:j + 2 * LANES], idx, axis=1)
                glus.append(jnp.where(low, pa, pltpu.roll(pb, LANES // 2, 1)))
                lins.append(jnp.where(low, pltpu.roll(pa, LANES // 2, 1), pb))
            glu = jnp.minimum(jnp.concatenate(glus, axis=1), SWIGLU_LIMIT)
            lin = jnp.clip(jnp.concatenate(lins, axis=1), -SWIGLU_LIMIT, SWIGLU_LIMIT)
            return (glu * jax.nn.sigmoid(SWIGLU_ALPHA * glu) * (lin + 1.0)).astype(BF16)

        def first_up(r):
            g = None
            for k0 in range(0, d, MOE_KC):
                wgb_ref[k0:k0 + MOE_KC, :] = wgu_ref[k0:k0 + MOE_KC, :].astype(BF16)
                part = jnp.dot(xb_ref[r, k0:k0 + MOE_KC], wgb_ref[k0:k0 + MOE_KC, :], preferred_element_type=F32)
                g = part if g is None else g + part
            wdb_ref[...] = wdn_ref[...].astype(BF16)
            return g

        def block(r0, rows, cast_first=False):
            tile = min(MOE_TILE, rows)
            tiles = [pl.ds(r0 + t, tile) for t in range(0, rows, tile)]
            gus = [(first_up(r) if cast_first and t == 0
                    else jnp.dot(xb_ref[r, :], wgb_ref[...], preferred_element_type=F32)) + bgu_ref[...]
                   for t, r in enumerate(tiles)]
            acts = [activation(gu) for gu in gus]
            for r, act in zip(tiles, acts):
                for n0 in range(0, d, MOE_NC):
                    acc_ref[r, n0:n0 + MOE_NC] += jnp.dot(
                        act, wdb_ref[:, n0:n0 + MOE_NC], preferred_element_type=F32)

        block(0, MOE_BIG, cast_first=True)
        per_big = MOE_BIG // MOE_SUB
        nbig = lax.shift_right_logical(jnp.maximum(nsub, per_big), per_big.bit_length() - 1)

        def big(b, _):
            block(pl.multiple_of(b * MOE_BIG, MOE_BIG), MOE_BIG)
            return 0

        def small(sb, _):
            block(pl.multiple_of(sb * MOE_SUB, MOE_SUB), MOE_SUB)
            return 0

        lax.fori_loop(1, nbig, big, 0)
        lax.fori_loop(nbig * per_big, nsub, small, 0)

    @pl.when(f == nf - 1)
    def _():
        def out_start(sb, _):
            out_copy(start, sb).start()
            return 0

        lax.fori_loop(0, nsub, out_start, 0)

    @pl.when((f == nf - 1) & (i == pl.num_programs(0) - 1))
    def _():
        def out_wait(sb, _):
            out_copy(start, sb).wait()
            return 0

        lax.fori_loop(0, nsub, out_wait, 0)

    @pl.when((i == pl.num_programs(0) - 1) & (f == nf - 1))
    def _():
        first = lax.shift_right_logical(itail_ref[0], MOE_SUB.bit_length() - 1)
        last = ys_ref.shape[0] // MOE_SUB
        acc_ref[0:MOE_SUB, :] = jnp.zeros((MOE_SUB, d), F32)

        def tail_copy(sb):
            r0 = pl.multiple_of(sb * MOE_SUB, MOE_SUB)
            return pltpu.make_async_copy(acc_ref.at[0:MOE_SUB, :], ys_ref.at[pl.ds(r0, MOE_SUB), :], sem_out)

        def tail_start(sb, _):
            tail_copy(sb).start()
            return 0

        def tail_wait(sb, _):
            tail_copy(sb).wait()
            return 0

        lax.fori_loop(first, last, tail_start, 0)
        lax.fori_loop(first, last, tail_wait, 0)


def _experts(item_exp, item_start, item_nsub, item_tail, xs, w_gu, b_gu, w_dn, b_dn, n_rows):
    n_items = item_exp.shape[0]
    d = D_MODEL
    nf = D_EXPERT // MOE_FC
    grid_spec = pltpu.PrefetchScalarGridSpec(
        num_scalar_prefetch=4,
        grid=(n_items, nf),
        in_specs=[
            pl.BlockSpec(memory_space=pl.ANY),
            pl.BlockSpec((None, d, 2 * MOE_FC), lambda i, f, e, s, n, t: (e[i], 0, jnp.where(n[i] > 0, f, nf - 1))),
            pl.BlockSpec((None, 1, 2 * MOE_FC), lambda i, f, e, s, n, t: (e[i], 0, jnp.where(n[i] > 0, f, nf - 1))),
            pl.BlockSpec((None, MOE_FC, d), lambda i, f, e, s, n, t: (e[i], jnp.where(n[i] > 0, f, nf - 1), 0)),
            pl.BlockSpec((None, 1, d), lambda i, f, e, s, n, t: (e[i], 0, 0)),
        ],
        out_specs=pl.BlockSpec(memory_space=pl.ANY),
        scratch_shapes=[
            pltpu.VMEM((MOE_RMAX, d // 2), F32),
            pltpu.VMEM((MOE_RMAX, d), BF16),
            pltpu.VMEM((MOE_RMAX, d), F32),
            pltpu.VMEM((d, 2 * MOE_FC), BF16),
            pltpu.VMEM((MOE_FC, d), BF16),
            pltpu.SemaphoreType.DMA(()),
            pltpu.SemaphoreType.DMA(()),
        ],
    )
    return pl.pallas_call(
        _expert_body,
        grid_spec=grid_spec,
        out_shape=jax.ShapeDtypeStruct((n_rows, d), F32),
        compiler_params=_cparams(("arbitrary", "arbitrary"), vmem=58 * 1024 * 1024),
        name="experts",
    )(item_exp, item_start, item_nsub, item_tail, xs, w_gu, b_gu, w_dn, b_dn)


def _combine_body(dest_ref, ys_ref, x1_ref, wts_ref, o_ref, buf_ref, sem, *, bm):
    def copy(t, k):
        return pltpu.make_async_copy(ys_ref.at[pl.ds(dest_ref[0, 0, k * bm + t], 1), :],
                                     buf_ref.at[k, pl.ds(t, 1), :], sem)

    def start(t, _):
        for k in range(TOP_K):
            copy(t, k).start(priority=k % 2)
        return 0

    def wait(t, _):
        for k in range(TOP_K):
            copy(t, k).wait()
        return 0

    lax.fori_loop(0, bm, start, 0, unroll=8)
    lax.fori_loop(0, bm, wait, 0, unroll=8)
    w = wts_ref[...]
    acc = x1_ref[...]
    for k in range(TOP_K):
        acc = acc + w[:, k:k + 1] * buf_ref[k]
    o_ref[...] = acc


def _combine(dest3, ys, x1, wts, bm):
    m, d = x1.shape
    return pl.pallas_call(
        functools.partial(_combine_body, bm=bm),
        grid=(m // bm,),
        in_specs=[
            pl.BlockSpec((1, 1, bm * TOP_K), lambda i: (i, 0, 0), memory_space=pltpu.SMEM),
            pl.BlockSpec(memory_space=pl.ANY),
            pl.BlockSpec((bm, d), lambda i: (i, 0)),
            pl.BlockSpec((bm, LANES), lambda i: (i, 0)),
        ],
        out_specs=pl.BlockSpec((bm, d), lambda i: (i, 0)),
        out_shape=jax.ShapeDtypeStruct((m, d), F32),
        scratch_shapes=[pltpu.VMEM((TOP_K, bm, d), F32), pltpu.SemaphoreType.DMA(())],
        compiler_params=_cparams(("arbitrary",)),
        name="combine",
    )(dest3, ys, x1, wts)


def _rope_tables(seq):
    half = DA_DQK // 2
    inv = ROPE_THETA ** (-jnp.arange(0, DA_DQK, 2, dtype=F32) / DA_DQK)
    ang = jnp.arange(seq, dtype=F32)[:, None] * inv[None, :]
    cos, sin = jnp.cos(ang), jnp.sin(ang)
    reps = LANES // DA_DQK
    cosf = jnp.tile(jnp.concatenate([cos, cos], axis=1), (1, reps))
    sins = jnp.tile(jnp.concatenate([-sin, sin], axis=1), (1, reps))
    del half
    return cosf, sins


def _moe_tables(ids, ranks, cnt_row):
    t = ids.shape[1]
    counts = cnt_row.astype(jnp.int32)
    nsb = (counts + MOE_SUB - 1) // MOE_SUB
    padded = nsb * MOE_SUB
    pad_end = jnp.cumsum(padded)
    pad_start = pad_end - padded
    experts = jnp.arange(N_EXPERTS, dtype=jnp.int32)[:, None, None]
    dest = ranks + jnp.sum(jnp.where(ids[None] == experts, pad_start[:, None, None], 0), axis=0)
    per_item = MOE_RMAX // MOE_SUB
    items_e = (nsb + per_item - 1) // per_item
    item_end = jnp.cumsum(items_e)
    n_items = (t * TOP_K // MOE_SUB + N_EXPERTS * per_item) // per_item
    idx = jnp.arange(n_items, dtype=jnp.int32)
    e_of = jnp.minimum(jnp.searchsorted(item_end, idx, side="right"), N_EXPERTS - 1).astype(jnp.int32)
    local = idx - (item_end[e_of] - items_e[e_of])
    valid = idx < item_end[-1]
    nsub = jnp.where(valid, jnp.clip(nsb[e_of] - local * per_item, 0, per_item), 0).astype(jnp.int32)
    last_e = e_of[jnp.maximum(item_end[-1] - 1, 0)]
    item_exp = jnp.where(valid, e_of, last_e).astype(jnp.int32)
    item_start = jnp.where(valid, pad_start[e_of] + local * MOE_RMAX, 0).astype(jnp.int32)
    return dest.astype(jnp.int32), item_exp, item_start, nsub, pad_end.astype(jnp.int32)


def kernel(x, norm1_g, w_in, ml_gate_bias, ml_norm_g, w_ml_out, da_q_norm_g, da_k_norm_g, da_lambda, da_norm_g,
           w_da_out, w_o, norm2_g, w_router, b_router, w_gate_up, b_gate_up, w_down, b_down):
    batch, seq, d = x.shape
    depth = norm1_g.shape[0]
    tokens = batch * seq
    cosf, sins = _rope_tables(seq)
    gmat = jnp.kron(jnp.eye(LANES // DA_DQK, dtype=F32), jnp.full((DA_DQK, DA_DQK), 1.0 / DA_DQK, F32)).astype(BF16)
    n_rows = tokens * TOP_K + N_EXPERTS * MOE_SUB
    bm_tok = 256

    x2 = x.reshape(tokens, d)
    for l in range(depth):
        lam_init = 0.8 - 0.6 * math.exp(-0.3 * l)
        w = w_in[l]
        w_lo = w[:, :_OFF_MG].astype(BF16)
        w_hi = w[:, _N_MG:].astype(BF16)
        w_gate = w[:, _OFF_MG:_OFF_MG + LANES].astype(BF16)
        gbias = jnp.pad(ml_gate_bias[l].reshape(1, _N_MG), ((0, 0), (0, LANES - _N_MG)))
        proj, gates = _inproj(x2, norm1_g[l].reshape(1, d), w_lo, w_hi, w_gate, gbias)

        gp = _gateprep(gates, ML_CHUNK)
        grow = gp.reshape(4, ML_HEADS, 1, tokens)
        gcol = gp.reshape(4, ML_HEADS, batch, seq).transpose(2, 1, 3, 0)
        hm = _mlstm(proj, grow, gcol, ml_norm_g[l].reshape(1, ML_DV), batch, seq, ML_CHUNK)

        gq = jnp.tile(da_q_norm_g[l], LANES // DA_DQK).reshape(1, LANES)
        gk = jnp.tile(da_k_norm_g[l], LANES // DA_DQK).reshape(1, LANES)
        qr, kr = _qkprep(proj, cosf, sins, gq, gk, gmat, seq)
        hd = _attn(qr, kr, proj, da_lambda[l], da_norm_g[l].reshape(DA_DV, 1), batch, seq, lam_init)

        wr = jnp.pad(w_router[l], ((0, 0), (0, LANES - N_EXPERTS)))
        br = jnp.pad(b_router[l].reshape(1, N_EXPERTS), ((0, 0), (0, LANES - N_EXPERTS)))
        x1, h2, logits = _post(hm, hd, proj, x2, w_ml_out[l].astype(BF16), w_da_out[l].astype(BF16),
                               w_o[l].astype(BF16), norm2_g[l].reshape(1, d), wr, br)

        ids, ranks, wts, cnt = _route(logits)
        dest, item_exp, item_start, item_nsub, pad_end = _moe_tables(ids[:TOP_K], ranks[:TOP_K], cnt[0, :N_EXPERTS])
        dest3 = dest.reshape(TOP_K, tokens // bm_tok, bm_tok).transpose(1, 0, 2).reshape(
            tokens // bm_tok, 1, bm_tok * TOP_K)
        xs = _dispatch(pad_end, dest3, h2, n_rows, bm_tok)
        ys = _experts(item_exp, item_start, item_nsub, pad_end[-1:], xs, w_gate_up[l],
                      b_gate_up[l].reshape(N_EXPERTS, 1, 2 * D_EXPERT), w_down[l],
                      b_down[l].reshape(N_EXPERTS, 1, d), n_rows)
        x2 = _combine(dest3, ys, x1, wts, bm_tok)
    return x2.reshape(batch, seq, d)
```

```python
import functools
import math

import jax
import jax.numpy as jnp
from jax import lax
from jax.experimental import pallas as pl
from jax.experimental.pallas import tpu as pltpu

F32 = jnp.float32
BF16 = jnp.bfloat16

D_MODEL = 2048
ML_HEADS = 4
ML_DQK = 128
ML_DV = 256
ML_W = ML_HEADS * ML_DV
DA_HEADS = 8
DA_DQK = 64
DA_DV = 2 * DA_DQK
DA_W = DA_HEADS * DA_DV
ROPE_THETA = 10000.0
N_EXPERTS = 32
TOP_K = 4
D_EXPERT = D_MODEL
SWIGLU_LIMIT = 7.0
SWIGLU_ALPHA = 1.702
EPS = 1e-6
LOG2E = 1.4426950408889634

LANES = 128
VMEM_LIMIT = 48 * 1024 * 1024

_OFF_MG = 2 * ML_HEADS * ML_DQK + 2 * ML_W
_N_MG = 4 * ML_HEADS
_C_MQ, _C_MK, _C_MV, _C_MO = 0, 512, 1024, 2048
_C_DQ, _C_DK, _C_DV, _C_GM, _C_GD = 3072, 4096, 5120, 6144, 8192
_N_MAIN = 10240

ML_CHUNK = 512
AT_QB = 1024
AT_KB = 256
AT_SAFE_BOUND = 60.0
MOE_SUB = 256
MOE_RMAX = 2048
MOE_BIG = 1024
MOE_TILE = 256
MOE_FC = 256
MOE_NC = 512
MOE_KC = 256


def _cparams(sem, vmem=VMEM_LIMIT):
    return pltpu.CompilerParams(dimension_semantics=sem, vmem_limit_bytes=vmem)


def _pack_halves(x):
    n = x.shape[1] // 2
    lo = lax.bitcast_convert_type(x[:, :n].astype(BF16).astype(F32), jnp.uint32)
    hi = lax.bitcast_convert_type(x[:, n:].astype(BF16).astype(F32), jnp.uint32)
    return lax.bitcast_convert_type(lax.shift_right_logical(lo, jnp.uint32(16)) | hi, F32)


def _unpack_halves(w):
    u = lax.bitcast_convert_type(w, jnp.uint32)
    lo = lax.bitcast_convert_type(lax.shift_left(u, jnp.uint32(16)), F32)
    hi = lax.bitcast_convert_type(u & jnp.uint32(0xFFFF0000), F32)
    return lo, hi


def _inproj_body(x_ref, g_ref, wlo_ref, whi_ref, wg_ref, gb_ref, o_ref, og_ref, xn_ref, *, nlo):
    j = pl.program_id(1)

    @pl.when(j == 0)
    def _():
        x = x_ref[...]
        ms = jnp.mean(x * x, axis=-1, keepdims=True)
        xn = (x * lax.rsqrt(ms + EPS) * g_ref[...]).astype(BF16)
        xn_ref[...] = xn
        gates = jnp.dot(xn, wg_ref[...], preferred_element_type=F32) + gb_ref[...]
        og_ref[...] = gates.T[0:_N_MG, :]

    @pl.when(j < nlo)
    def _():
        o_ref[...] = jnp.dot(xn_ref[...], wlo_ref[...], preferred_element_type=F32).astype(o_ref.dtype)

    @pl.when(j >= nlo)
    def _():
        o_ref[...] = jnp.dot(xn_ref[...], whi_ref[...], preferred_element_type=F32).astype(o_ref.dtype)


def _inproj(x2, g1, w_lo, w_hi, w_gate, gate_bias, bm=1024, bn=1024):
    m, d = x2.shape
    n = w_hi.shape[1]
    nlo = w_lo.shape[1] // bn
    return pl.pallas_call(
        functools.partial(_inproj_body, nlo=nlo),
        grid=(m // bm, n // bn),
        in_specs=[
            pl.BlockSpec((bm, d), lambda i, j: (i, 0)),
            pl.BlockSpec((1, d), lambda i, j: (0, 0)),
            pl.BlockSpec((d, bn), lambda i, j: (0, jnp.minimum(j, nlo - 1))),
            pl.BlockSpec((d, bn), lambda i, j: (0, jnp.maximum(j, nlo))),
            pl.BlockSpec((d, LANES), lambda i, j: (0, 0)),
            pl.BlockSpec((1, LANES), lambda i, j: (0, 0)),
        ],
        out_specs=[
            pl.BlockSpec((bm, bn), lambda i, j: (i, j)),
            pl.BlockSpec((_N_MG, bm), lambda i, j: (0, i)),
        ],
        out_shape=[
            jax.ShapeDtypeStruct((m, n), BF16),
            jax.ShapeDtypeStruct((_N_MG, m), F32),
        ],
        scratch_shapes=[pltpu.VMEM((bm, d), BF16)],
        compiler_params=_cparams(("parallel", "arbitrary")),
        name="inproj",
    )(x2, g1, w_lo, w_hi, w_gate, gate_bias)


def _gateprep_body(g_ref, o_ref):
    x = g_ref[...]
    c = x.shape[1]
    lf = jnp.minimum(x, 0.0) - jnp.log1p(jnp.exp(-jnp.abs(x)))
    r = lax.broadcasted_iota(jnp.int32, (c, c), 0)
    s = lax.broadcasted_iota(jnp.int32, (c, c), 1)
    upper = (r <= s).astype(F32)
    lower = (r >= s).astype(F32)
    pre = jnp.dot(lf, upper, preferred_element_type=F32, precision=lax.Precision.HIGHEST)
    suf = jnp.dot(lf, lower, preferred_element_type=F32, precision=lax.Precision.HIGHEST)
    row = lax.broadcasted_iota(jnp.int32, x.shape, 0)
    h = ML_HEADS
    out = jnp.where((row >= h) & (row < 2 * h), pre, x)
    out = jnp.where(row >= 3 * h, suf, out)
    o_ref[...] = out


def _gateprep(gt, chunk):
    r, n = gt.shape
    return pl.pallas_call(
        _gateprep_body,
        grid=(n // chunk,),
        in_specs=[pl.BlockSpec((r, chunk), lambda i: (0, i))],
        out_specs=pl.BlockSpec((r, chunk), lambda i: (0, i)),
        out_shape=jax.ShapeDtypeStruct((r, n), F32),
        compiler_params=_cparams(("parallel",)),
        name="gateprep",
    )(gt)


def _mlstm_body(q_ref, k_ref, v_ref, mo_ref, grow_ref, gcol_ref, ng_ref, o_ref,
                qt_ref, vat_ref, hf_ref, hb_ref, *, chunk):
    s_len = q_ref.shape[0]
    nc = s_len // chunk
    dv = ML_DV
    wide = vat_ref.shape[0]
    inv_scale = float(ML_DQK) ** 0.5

    def transpose_in(c, _):
        r0 = pl.multiple_of(c * chunk, chunk)
        qt_ref[:, pl.ds(r0, chunk)] = q_ref[pl.ds(r0, chunk), :].astype(F32).T.astype(BF16)
        vat_ref[0:dv, pl.ds(r0, chunk)] = v_ref[pl.ds(r0, chunk), :].astype(F32).T.astype(BF16)
        return 0

    lax.fori_loop(0, nc, transpose_in, 0)
    extra = lax.broadcasted_iota(jnp.int32, (wide - dv, s_len), 0)
    vat_ref[dv:wide, :] = jnp.where(extra == 0, 1.0, 0.0).astype(BF16)

    ss = lax.broadcasted_iota(jnp.int32, (chunk, chunk), 0)
    tt = lax.broadcasted_iota(jnp.int32, (chunk, chunk), 1)

    def chunk_step(c, state, m_prev, reverse):
        r0 = pl.multiple_of(c * chunk, chunk)
        gi, gb = (2, 3) if reverse else (0, 1)
        qtc = qt_ref[:, pl.ds(r0, chunk)]
        kc = k_ref[pl.ds(r0, chunk), :]
        vatc = vat_ref[:, pl.ds(r0, chunk)]
        i_row = grow_ref[gi, :, pl.ds(r0, chunk)]
        b_row = grow_ref[gb, :, pl.ds(r0, chunk)]
        gcol = gcol_ref[0, pl.ds(r0, chunk), :]
        c_col = gcol[:, gb:gb + 1] - gcol[:, gi:gi + 1]
        mask = (ss >= tt) if reverse else (ss <= tt)
        log_d = jnp.where(mask, b_row - c_col, -jnp.inf)
        inter = b_row + m_prev
        m_t = jnp.maximum(inter, jnp.max(log_d, axis=0, keepdims=True))
        s_inter = jnp.exp(inter - m_t)
        dm = jnp.exp(log_d - m_t)
        skq = jnp.dot(kc, qtc, preferred_element_type=F32)
        p = (dm * skq).astype(BF16)
        nd = (jnp.dot(vatc, p, preferred_element_type=F32)
              + s_inter * jnp.dot(state.astype(BF16), qtc, preferred_element_type=F32))
        den = nd[dv:dv + 1, :]
        floor = jnp.exp(-m_t) * inv_scale
        h = (nd[:dv, :] / jnp.maximum(jnp.abs(den), floor)).T
        g = b_row[:, 0:1] if reverse else b_row[:, chunk - 1:chunk]
        a_row = g - b_row + i_row
        m_new = jnp.maximum(g + m_prev, jnp.max(a_row, axis=1, keepdims=True))
        decay = jnp.exp(g + m_prev - m_new)
        w_row = jnp.exp(a_row - m_new)
        wv = (vatc.astype(F32) * w_row).astype(BF16)
        new_state = decay * state + jnp.dot(wv, kc, preferred_element_type=F32)
        return r0, h, new_state, m_new

    def body(c, carry):
        sf, mf, sb, mb = carry
        r0, h, sf, mf = chunk_step(c, sf, mf, False)
        hf_ref[pl.ds(r0, chunk), :] = h
        r1, h2, sb, mb = chunk_step(nc - 1 - c, sb, mb, True)
        hb_ref[pl.ds(r1, chunk), :] = h2
        return sf, mf, sb, mb

    z = jnp.zeros((wide, ML_DQK), F32)
    m0 = jnp.zeros((1, 1), F32)
    lax.fori_loop(0, nc, body, (z, m0, z, m0), unroll=4)

    def epilogue(c, _):
        r0 = pl.multiple_of(c * chunk, chunk)
        h = hf_ref[pl.ds(r0, chunk), :] + hb_ref[pl.ds(r0, chunk), :]
        ms = jnp.mean(h * h, axis=-1, keepdims=True)
        hn = h * lax.rsqrt(ms + EPS) * ng_ref[...]
        gate = jax.nn.sigmoid(mo_ref[pl.ds(r0, chunk), :].astype(F32))
        o_ref[pl.ds(r0, chunk), :] = (hn * gate).astype(o_ref.dtype)
        return 0

    lax.fori_loop(0, nc, epilogue, 0)


def _mlstm(proj, grow, gcol, ng, batch, seq, chunk):
    m = proj.shape[0]
    return pl.pallas_call(
        functools.partial(_mlstm_body, chunk=chunk),
        grid=(batch, ML_HEADS),
        in_specs=[
            pl.BlockSpec((seq, ML_DQK), lambda b, h: (b, _C_MQ // ML_DQK + h)),
            pl.BlockSpec((seq, ML_DQK), lambda b, h: (b, _C_MK // ML_DQK + h)),
            pl.BlockSpec((seq, ML_DV), lambda b, h: (b, _C_MV // ML_DV + h)),
            pl.BlockSpec((seq, ML_DV), lambda b, h: (b, _C_MO // ML_DV + h)),
            pl.BlockSpec((4, None, 1, seq), lambda b, h: (0, h, 0, b)),
            pl.BlockSpec((None, 1, seq, 4), lambda b, h: (b, h, 0, 0)),
            pl.BlockSpec((1, ML_DV), lambda b, h: (0, 0)),
        ],
        out_specs=pl.BlockSpec((seq, ML_DV), lambda b, h: (b, h)),
        out_shape=jax.ShapeDtypeStruct((m, ML_W), BF16),
        scratch_shapes=[
            pltpu.VMEM((ML_DQK, seq), BF16),
            pltpu.VMEM((ML_DV + 16, seq), BF16),
            pltpu.VMEM((seq, ML_DV), F32),
            pltpu.VMEM((seq, ML_DV), F32),
        ],
        compiler_params=_cparams(("parallel", "parallel")),
        name="mlstm",
    )(proj, proj, proj, proj, grow, gcol, ng)


def _qkprep_body(q_ref, k_ref, cos_ref, sin_ref, gq_ref, gk_ref, gm_ref, oq_ref, ok_ref, *, qscale):
    cosf = cos_ref[...]
    sins = sin_ref[...]
    lane = lax.broadcasted_iota(jnp.int32, cosf.shape, 1)
    first_half = (lane % DA_DQK) < (DA_DQK // 2)
    gmat = gm_ref[...]

    def one(x_ref, g_ref, o_ref, scale):
        for j in range(x_ref.shape[1] // LANES):
            x = x_ref[:, j * LANES:(j + 1) * LANES].astype(F32)
            ms = jnp.dot((x * x).astype(BF16), gmat, preferred_element_type=F32)
            y = x * lax.rsqrt(ms + EPS) * g_ref[...]
            rot = jnp.where(first_half, pltpu.roll(y, LANES - DA_DQK // 2, 1), pltpu.roll(y, DA_DQK // 2, 1))
            o = y * cosf + rot * sins
            if scale != 1.0:
                o = o * scale
            o_ref[:, j * LANES:(j + 1) * LANES] = o.astype(o_ref.dtype)

    one(q_ref, gq_ref, oq_ref, qscale)
    one(k_ref, gk_ref, ok_ref, 1.0)


def _qkprep(proj, cosf, sins, gq, gk, gmat, seq, bm=512):
    m = proj.shape[0]
    nsb = seq // bm
    return pl.pallas_call(
        functools.partial(_qkprep_body, qscale=float(DA_DQK) ** -0.5 * LOG2E),
        grid=(m // bm,),
        in_specs=[
            pl.BlockSpec((bm, DA_W), lambda i: (i, _C_DQ // DA_W)),
            pl.BlockSpec((bm, DA_W), lambda i: (i, _C_DK // DA_W)),
            pl.BlockSpec((bm, LANES), lambda i: (i % nsb, 0)),
            pl.BlockSpec((bm, LANES), lambda i: (i % nsb, 0)),
            pl.BlockSpec((1, LANES), lambda i: (0, 0)),
            pl.BlockSpec((1, LANES), lambda i: (0, 0)),
            pl.BlockSpec((LANES, LANES), lambda i: (0, 0)),
        ],
        out_specs=[
            pl.BlockSpec((bm, DA_W), lambda i: (i, 0)),
            pl.BlockSpec((bm, DA_W), lambda i: (i, 0)),
        ],
        out_shape=[jax.ShapeDtypeStruct((m, DA_W), BF16)] * 2,
        compiler_params=_cparams(("parallel",)),
        name="qkprep",
    )(proj, proj, cosf, sins, gq, gk, gmat)


def _attn_body(q_ref, k_ref, v_ref, lam_ref, ng_ref, o_ref, vt_ref, kmax_ref, *, kb, lam_init):
    s_len = k_ref.shape[0]

    @pl.when(pl.program_id(2) == 0)
    def _():
        def transpose_v(c, _):
            r0 = pl.multiple_of(c * kb, kb)
            vt_ref[0:DA_DV, pl.ds(r0, kb)] = v_ref[pl.ds(r0, kb), :].astype(F32).T.astype(BF16)
            return 0

        lax.fori_loop(0, s_len // kb, transpose_v, 0)
        vt_ref[DA_DV:, :] = jnp.ones((vt_ref.shape[0] - DA_DV, s_len), BF16)

        gr = lax.broadcasted_iota(jnp.int32, (DA_DV, DA_DV), 0) // DA_DQK
        gc = lax.broadcasted_iota(jnp.int32, (DA_DV, DA_DV), 1) // DA_DQK
        group_sum = jnp.where(gr == gc, 1.0, 0.0).astype(BF16)

        def key_norms(c, mx):
            r0 = pl.multiple_of(c * kb, kb)
            kk = k_ref[pl.ds(r0, kb), :].astype(F32)
            n2 = jnp.dot((kk * kk).astype(BF16), group_sum, preferred_element_type=F32)
            return jnp.maximum(mx, jnp.max(n2, axis=0, keepdims=True))

        mx = lax.fori_loop(0, s_len // kb, key_norms, jnp.zeros((1, DA_DV), F32))
        kmax_ref[...] = jnp.broadcast_to(jnp.sqrt(mx), kmax_ref.shape)

    qt = q_ref[...].astype(F32).T.astype(BF16)
    qb = qt.shape[1]
    row = lax.broadcasted_iota(jnp.int32, qt.shape, 0)
    zero = jnp.zeros_like(qt)
    q1 = jnp.where(row < DA_DQK, qt, zero)
    q2 = jnp.where(row >= DA_DQK, qt, zero)
    nv = vt_ref.shape[0]

    def scores(j):
        kblk = k_ref[j * kb:(j + 1) * kb, :]
        return (jnp.dot(kblk, q1, preferred_element_type=F32),
                jnp.dot(kblk, q2, preferred_element_type=F32))

    def attend(upd, init):
        st1 = st2 = init
        s_cur = scores(0)
        for j in range(s_len // kb):
            s_nxt = scores(j + 1) if (j + 1) * kb < s_len else None
            vb = vt_ref[:, j * kb:(j + 1) * kb]
            st1 = upd(s_cur[0], st1, vb, 0)
            st2 = upd(s_cur[1], st2, vb, 1)
            s_cur = s_nxt
        return st1, st2

    def finish(a1, a2):
        lp = lam_ref[...]
        lam = (jnp.exp(jnp.sum(lp[0:1, :] * lp[1:2, :], axis=1, keepdims=True))
               - jnp.exp(jnp.sum(lp[2:3, :] * lp[3:4, :], axis=1, keepdims=True)) + lam_init)
        o = a1[:DA_DV, :] / a1[DA_DV:DA_DV + 1, :] - lam * (a2[:DA_DV, :] / a2[DA_DV:DA_DV + 1, :])
        ms = jnp.mean(o * o, axis=0, keepdims=True)
        on = o * lax.rsqrt(ms + EPS) * ng_ref[...] * (1.0 - lam_init)
        o_ref[...] = on.T.astype(o_ref.dtype)

    sq = qt.astype(F32) * qt.astype(F32)
    inflate = 1.0 + 2.0 ** -6
    bounds = (jnp.sqrt(jnp.sum(jnp.where(row < DA_DQK, sq, 0.0), axis=0, keepdims=True))
              * kmax_ref[0:1, 0:1] * inflate,
              jnp.sqrt(jnp.sum(jnp.where(row >= DA_DQK, sq, 0.0), axis=0, keepdims=True))
              * kmax_ref[0:1, DA_DQK:DA_DQK + 1] * inflate)
    safe = jnp.max(jnp.maximum(bounds[0], bounds[1])) < AT_SAFE_BOUND

    @pl.when(safe)
    def _():
        def upd(s, acc, vb, which):
            p = jnp.exp2(s - bounds[which]).astype(BF16)
            return acc + jnp.dot(vb, p, preferred_element_type=F32)

        finish(*attend(upd, jnp.zeros((nv, qb), F32)))

    @pl.when(jnp.logical_not(safe))
    def _():
        def upd(s, state, vb, which):
            m_old, acc = state
            m_new = jnp.maximum(m_old, jnp.max(s, axis=0, keepdims=True))
            alpha = jnp.exp2(m_old - m_new)
            p = jnp.exp2(s - m_new).astype(BF16)
            return m_new, alpha * acc + jnp.dot(vb, p, preferred_element_type=F32)

        init = (jnp.full((1, qb), -jnp.inf, F32), jnp.zeros((nv, qb), F32))
        (_, a1), (_, a2) = attend(upd, init)
        finish(a1, a2)


def _attn(qr, kr, proj, lam_params, ng_col, batch, seq, lam_init, qb=AT_QB, kb=AT_KB):
    nq = seq // qb
    return pl.pallas_call(
        functools.partial(_attn_body, kb=kb, lam_init=lam_init),
        grid=(batch, DA_HEADS, nq),
        in_specs=[
            pl.BlockSpec((qb, DA_DV), lambda b, h, i: (b * nq + i, h)),
            pl.BlockSpec((seq, DA_DV), lambda b, h, i: (b, h)),
            pl.BlockSpec((seq, DA_DV), lambda b, h, i: (b, _C_DV // DA_DV + h)),
            pl.BlockSpec((4, DA_DQK), lambda b, h, i: (0, 0)),
            pl.BlockSpec((DA_DV, 1), lambda b, h, i: (0, 0)),
        ],
        out_specs=pl.BlockSpec((qb, DA_DV), lambda b, h, i: (b * nq + i, h)),
        out_shape=jax.ShapeDtypeStruct((batch * seq, DA_W), BF16),
        scratch_shapes=[pltpu.VMEM((DA_DV + 8, seq), BF16), pltpu.VMEM((8, DA_DV), F32)],
        compiler_params=_cparams(("parallel", "parallel", "arbitrary")),
        name="diffattn",
    )(qr, kr, proj, lam_params, ng_col)


def _post_body(hm_ref, hd_ref, gm_ref, gd_ref, x_ref, wm_ref, wd_ref, wo_ref, g2_ref, wrh_ref, wrl_ref, br_ref,
               x1_ref, h2_ref, lg_ref, *, tile):
    rows = [slice(t * tile, (t + 1) * tile) for t in range(x_ref.shape[0] // tile)]
    dot = functools.partial(jnp.dot, preferred_element_type=F32)
    ys = [(dot(hm_ref[r, :], wm_ref[...]), dot(hd_ref[r, :], wd_ref[...])) for r in rows]
    mixes = [(jax.nn.sigmoid(gm_ref[r, :].astype(F32)) * ym
              + jax.nn.sigmoid(gd_ref[r, :].astype(F32)) * yd).astype(BF16) for r, (ym, yd) in zip(rows, ys)]
    x1s = [x_ref[r, :] + dot(mix, wo_ref[...]) for r, mix in zip(rows, mixes)]
    for r, x1 in zip(rows, x1s):
        x1_ref[r, :] = x1
        ms = jnp.mean(x1 * x1, axis=-1, keepdims=True)
        h2 = x1 * lax.rsqrt(ms + EPS) * g2_ref[...]
        h2_ref[r, :] = _pack_halves(h2)
        hi = h2.astype(BF16)
        lo = (h2 - hi.astype(F32)).astype(BF16)
        lg_ref[r, :] = (dot(hi, wrh_ref[...]) + dot(hi, wrl_ref[...]) + dot(lo, wrh_ref[...])) + br_ref[...]


def _post(hm, hd, proj, x2, wm, wd, wo, g2, wr, br, bm=512, tile=256):
    m, d = x2.shape
    const = lambda i: (0, 0)
    wr_hi = wr.astype(BF16)
    wr_lo = (wr - wr_hi.astype(F32)).astype(BF16)
    return pl.pallas_call(
        functools.partial(_post_body, tile=tile),
        grid=(m // bm,),
        in_specs=[
            pl.BlockSpec((bm, ML_W), lambda i: (i, 0)),
            pl.BlockSpec((bm, DA_W), lambda i: (i, 0)),
            pl.BlockSpec((bm, d), lambda i: (i, _C_GM // D_MODEL)),
            pl.BlockSpec((bm, d), lambda i: (i, _C_GD // D_MODEL)),
            pl.BlockSpec((bm, d), lambda i: (i, 0)),
            pl.BlockSpec((ML_W, d), const, pipeline_mode=pl.Buffered(1)),
            pl.BlockSpec((DA_W, d), const, pipeline_mode=pl.Buffered(1)),
            pl.BlockSpec((d, d), const, pipeline_mode=pl.Buffered(1)),
            pl.BlockSpec((1, d), const),
            pl.BlockSpec((d, LANES), const),
            pl.BlockSpec((d, LANES), const),
            pl.BlockSpec((1, LANES), const),
        ],
        out_specs=[
            pl.BlockSpec((bm, d), lambda i: (i, 0)),
            pl.BlockSpec((bm, d // 2), lambda i: (i, 0)),
            pl.BlockSpec((bm, LANES), lambda i: (i, 0)),
        ],
        out_shape=[
            jax.ShapeDtypeStruct((m, d), F32),
            jax.ShapeDtypeStruct((m, d // 2), F32),
            jax.ShapeDtypeStruct((m, LANES), F32),
        ],
        compiler_params=_cparams(("parallel",), vmem=62 * 1024 * 1024),
        name="postmix",
    )(hm, hd, proj, proj, x2, wm, wd, wo, g2, wr_hi, wr_lo, br)


def _route_body(lg_ref, ids_ref, rank_ref, wts_ref, cnt_ref, carry_ref):
    @pl.when(pl.program_id(0) == 0)
    def _():
        carry_ref[...] = jnp.zeros_like(carry_ref)

    lg = lg_ref[...]
    bm = lg.shape[0]
    lane = lax.broadcasted_iota(jnp.int32, lg.shape, 1)
    lanef = lane.astype(F32)
    work = jnp.where(lane < N_EXPERTS, lg, -jnp.inf)
    vals, hots = [], []
    ids = jnp.zeros(lg.shape, F32)
    for k in range(TOP_K):
        mx = jnp.max(work, axis=1, keepdims=True)
        idx = jnp.min(jnp.where(work == mx, lanef, float(LANES)), axis=1, keepdims=True)
        hot = lanef == idx
        work = jnp.where(hot, -jnp.inf, work)
        vals.append(mx)
        hots.append(hot)
        ids = jnp.where(lane == k, idx, ids)
    exps = [jnp.exp(v - vals[0]) for v in vals]
    tot = exps[0] + exps[1] + exps[2] + exps[3]
    sel = jnp.zeros(lg.shape, F32)
    for hot in hots:
        sel = jnp.where(hot, 1.0, sel)
    r = lax.broadcasted_iota(jnp.int32, (bm, bm), 0)
    c = lax.broadcasted_iota(jnp.int32, (bm, bm), 1)
    strict = jnp.where(c < r, 1.0, 0.0).astype(BF16)
    cum = jnp.dot(strict, sel.astype(BF16), preferred_element_type=F32) + carry_ref[0:1, :]
    ranks = jnp.zeros(lg.shape, F32)
    wts = jnp.zeros(lg.shape, F32)
    for k in range(TOP_K):
        rk = jnp.sum(jnp.where(hots[k], cum, 0.0), axis=1, keepdims=True)
        ranks = jnp.where(lane == k, rk, ranks)
        wts = jnp.where(lane == k, exps[k] / tot, wts)
    newc = carry_ref[0:1, :] + jnp.sum(sel, axis=0, keepdims=True)
    carry_ref[...] = jnp.broadcast_to(newc, carry_ref.shape)
    ids_ref[...] = ids.T[0:8, :].astype(jnp.int32)
    rank_ref[...] = ranks.T[0:8, :].astype(jnp.int32)
    wts_ref[...] = wts
    cnt_ref[...] = jnp.broadcast_to(newc, cnt_ref.shape)


def _route(logits, bm=512):
    m = logits.shape[0]
    blk = pl.BlockSpec((bm, LANES), lambda i: (i, 0))
    tblk = pl.BlockSpec((8, bm), lambda i: (0, i))
    return pl.pallas_call(
        _route_body,
        grid=(m // bm,),
        in_specs=[blk],
        out_specs=[tblk, tblk, blk, pl.BlockSpec((8, LANES), lambda i: (0, 0))],
        out_shape=[
            jax.ShapeDtypeStruct((8, m), jnp.int32),
            jax.ShapeDtypeStruct((8, m), jnp.int32),
            jax.ShapeDtypeStruct((m, LANES), F32),
            jax.ShapeDtypeStruct((8, LANES), F32),
        ],
        scratch_shapes=[pltpu.VMEM((8, LANES), F32)],
        compiler_params=_cparams(("arbitrary",)),
        name="route",
    )(logits)


def _dispatch_body(pend_ref, dest_ref, h2_ref, xs_ref, zero_ref, sem, zsem, *, bm):
    @pl.when(pl.program_id(0) == 0)
    def _():
        zero_ref[...] = jnp.zeros_like(zero_ref)
        n_sub_total = xs_ref.shape[0] // MOE_SUB
        first_tail = lax.shift_right_logical(pend_ref[N_EXPERTS - 1], MOE_SUB.bit_length() - 1)

        def zcopy(r0):
            return pltpu.make_async_copy(zero_ref, xs_ref.at[pl.ds(pl.multiple_of(r0, MOE_SUB), MOE_SUB), :], zsem)

        def pad_row(e):
            return jnp.maximum(pend_ref[e] - MOE_SUB, 0)

        def nonempty(e):
            return pend_ref[e] > (pend_ref[e - 1] if e else 0)

        for e in range(N_EXPERTS):
            @pl.when(nonempty(e))
            def _():
                zcopy(pad_row(e)).start()

        def tail_start(sb, _):
            zcopy(sb * MOE_SUB).start()
            return 0

        def tail_wait(sb, _):
            zcopy(sb * MOE_SUB).wait()
            return 0

        lax.fori_loop(first_tail, n_sub_total, tail_start, 0)
        for e in range(N_EXPERTS):
            @pl.when(nonempty(e))
            def _():
                zcopy(pad_row(e)).wait()
        lax.fori_loop(first_tail, n_sub_total, tail_wait, 0)

    def copy(t, k):
        return pltpu.make_async_copy(h2_ref.at[pl.ds(t, 1), :],
                                     xs_ref.at[pl.ds(dest_ref[0, 0, k * bm + t], 1), :], sem)

    def start(t, _):
        for k in range(TOP_K):
            copy(t, k).start(priority=k % 2)
        return 0

    def wait(t, _):
        for k in range(TOP_K):
            copy(t, k).wait()
        return 0

    lax.fori_loop(0, bm, start, 0, unroll=8)
    lax.fori_loop(0, bm, wait, 0, unroll=8)


def _dispatch(pad_end, dest3, h2, n_rows, bm):
    m, width = h2.shape
    grid_spec = pltpu.PrefetchScalarGridSpec(
        num_scalar_prefetch=1,
        grid=(m // bm,),
        in_specs=[
            pl.BlockSpec((1, 1, bm * TOP_K), lambda i, p: (i, 0, 0), memory_space=pltpu.SMEM),
            pl.BlockSpec((bm, width), lambda i, p: (i, 0)),
        ],
        out_specs=pl.BlockSpec(memory_space=pl.ANY),
        scratch_shapes=[
            pltpu.VMEM((MOE_SUB, width), h2.dtype),
            pltpu.SemaphoreType.DMA(()),
            pltpu.SemaphoreType.DMA(()),
        ],
    )
    return pl.pallas_call(
        functools.partial(_dispatch_body, bm=bm),
        grid_spec=grid_spec,
        out_shape=jax.ShapeDtypeStruct((n_rows, width), h2.dtype),
        compiler_params=_cparams(("arbitrary",)),
        name="dispatch",
    )(pad_end, dest3, h2)


def _expert_body(iexp_ref, istart_ref, insub_ref, itail_ref, xs_ref, wgu_ref, bgu_ref, wdn_ref, bdn_ref,
                 ys_ref, xstage_ref, xb_ref, acc_ref, wgb_ref, wdb_ref, sem_in, sem_out):
    del iexp_ref
    i = pl.program_id(0)
    f = pl.program_id(1)
    nf = pl.num_programs(1)
    nsub = insub_ref[i]
    start = pl.multiple_of(istart_ref[i], MOE_SUB)
    d = acc_ref.shape[1]
    half = d // 2

    def sub_rows(sb):
        return pl.ds(pl.multiple_of(sb * MOE_SUB, MOE_SUB), MOE_SUB)

    @pl.when((i == 0) & (f == 0))
    def _():
        xb_ref[0:MOE_BIG, :] = jnp.zeros((MOE_BIG, d), BF16)

    def out_copy(first_row, sb):
        r0 = pl.multiple_of(sb * MOE_SUB, MOE_SUB)
        return pltpu.make_async_copy(acc_ref.at[pl.ds(r0, MOE_SUB), :],
                                     ys_ref.at[pl.ds(first_row + r0, MOE_SUB), :], sem_out)

    def x_copy(first_row, sb):
        r0 = pl.multiple_of(sb * MOE_SUB, MOE_SUB)
        return pltpu.make_async_copy(xs_ref.at[pl.ds(first_row + r0, MOE_SUB), :],
                                     xstage_ref.at[pl.ds(r0, MOE_SUB), :], sem_in)

    @pl.when((f == 0) & (i == 0))
    def _():
        def x_start(sb, _):
            x_copy(start, sb).start()
            return 0

        lax.fori_loop(0, nsub, x_start, 0)

    @pl.when((f == nf - 1) & (i + 1 < pl.num_programs(0)))
    def _():
        nxt = jnp.minimum(i + 1, pl.num_programs(0) - 1)
        nxt_start = pl.multiple_of(istart_ref[nxt], MOE_SUB)

        def x_start(sb, _):
            x_copy(nxt_start, sb).start()
            return 0

        lax.fori_loop(0, insub_ref[nxt], x_start, 0)

    @pl.when((f == 0) & (nsub > 0))
    def _():
        def x_wait(sb, _):
            x_copy(start, sb).wait()
            return 0

        def unpack(sb, _):
            lo, hi = _unpack_halves(xstage_ref[sub_rows(sb), :])
            xb_ref[sub_rows(sb), 0:half] = lo.astype(BF16)
            xb_ref[sub_rows(sb), half:d] = hi.astype(BF16)
            return 0

        lax.fori_loop(0, nsub, x_wait, 0)
        lax.fori_loop(0, nsub, unpack, 0)

    @pl.when((f == 0) & (i > 0))
    def _():
        prev = jnp.maximum(i - 1, 0)
        prev_start = pl.multiple_of(istart_ref[prev], MOE_SUB)

        def prev_wait(sb, _):
            out_copy(prev_start, sb).wait()
            return 0

        lax.fori_loop(0, insub_ref[prev], prev_wait, 0)

    @pl.when((f == 0) & (nsub > 0))
    def _():
        def init(sb, _):
            acc_ref[sub_rows(sb), :] = jnp.broadcast_to(bdn_ref[...], (MOE_SUB, d))
            return 0

        lax.fori_loop(0, jnp.maximum(nsub, MOE_BIG // MOE_SUB), init, 0)

    @pl.when(nsub > 0)
    def _():
        def activation(gu):
            lane = lax.broadcasted_iota(jnp.int32, (gu.shape[0], LANES), 1)
            low = lane < LANES // 2
            idx = jnp.where(low, 2 * lane, 2 * lane - (LANES - 1))
            glus, lins = [], []
            for j in range(0, 2 * MOE_FC, 2 * LANES):
                pa = jnp.take_along_axis(gu[:, j:j + LANES], idx, axis=1)
                pb = jnp.take_along_axis(gu[:, j + LANES:j + 2 * LANES], idx, axis=1)
                glus.append(jnp.where(low, pa, pltpu.roll(pb, LANES // 2, 1)))
                lins.append(jnp.where(low, pltpu.roll(pa, LANES // 2, 1), pb))
            glu = jnp.minimum(jnp.concatenate(glus, axis=1), SWIGLU_LIMIT)
            lin = jnp.clip(jnp.concatenate(lins, axis=1), -SWIGLU_LIMIT, SWIGLU_LIMIT)
            return (glu * jax.nn.sigmoid(SWIGLU_ALPHA * glu) * (lin + 1.0)).astype(BF16)

        def first_up(r):
            g = None
            for k0 in range(0, d, MOE_KC):
                wgb_ref[k0:k0 + MOE_KC, :] = wgu_ref[k0:k0 + MOE_KC, :].astype(BF16)
                part = jnp.dot(xb_ref[r, k0:k0 + MOE_KC], wgb_ref[k0:k0 + MOE_KC, :], preferred_element_type=F32)
                g = part if g is None else g + part
            wdb_ref[...] = wdn_ref[...].astype(BF16)
            return g

        def block(r0, rows, cast_first=False):
            tile = min(MOE_TILE, rows)
            tiles = [pl.ds(r0 + t, tile) for t in range(0, rows, tile)]
            gus = [(first_up(r) if cast_first and t == 0
                    else jnp.dot(xb_ref[r, :], wgb_ref[...], preferred_element_type=F32)) + bgu_ref[...]
                   for t, r in enumerate(tiles)]
            acts = [activation(gu) for gu in gus]
            for r, act in zip(tiles, acts):
                for n0 in range(0, d, MOE_NC):
                    acc_ref[r, n0:n0 + MOE_NC] += jnp.dot(
                        act, wdb_ref[:, n0:n0 + MOE_NC], preferred_element_type=F32)

        per_big = MOE_BIG // MOE_SUB

        @pl.when(nsub == per_big + 1)
        def _():
            block(0, MOE_BIG + MOE_SUB, cast_first=True)

        @pl.when(nsub != per_big + 1)
        def _():
            block(0, MOE_BIG, cast_first=True)
            nbig = lax.shift_right_logical(jnp.maximum(nsub, per_big), per_big.bit_length() - 1)

            def big(b, _):
                block(pl.multiple_of(b * MOE_BIG, MOE_BIG), MOE_BIG)
                return 0

            def small(sb, _):
                block(pl.multiple_of(sb * MOE_SUB, MOE_SUB), MOE_SUB)
                return 0

            lax.fori_loop(1, nbig, big, 0)
            lax.fori_loop(nbig * per_big, nsub, small, 0)

    @pl.when(f == nf - 1)
    def _():
        def out_start(sb, _):
            out_copy(start, sb).start()
            return 0

        lax.fori_loop(0, nsub, out_start, 0)

    @pl.when((f == nf - 1) & (i == pl.num_programs(0) - 1))
    def _():
        def out_wait(sb, _):
            out_copy(start, sb).wait()
            return 0

        lax.fori_loop(0, nsub, out_wait, 0)

    @pl.when((i == pl.num_programs(0) - 1) & (f == nf - 1))
    def _():
        first = lax.shift_right_logical(itail_ref[0], MOE_SUB.bit_length() - 1)
        last = ys_ref.shape[0] // MOE_SUB
        acc_ref[0:MOE_SUB, :] = jnp.zeros((MOE_SUB, d), F32)

        def tail_copy(sb):
            r0 = pl.multiple_of(sb * MOE_SUB, MOE_SUB)
            return pltpu.make_async_copy(acc_ref.at[0:MOE_SUB, :], ys_ref.at[pl.ds(r0, MOE_SUB), :], sem_out)

        def tail_start(sb, _):
            tail_copy(sb).start()
            return 0

        def tail_wait(sb, _):
            tail_copy(sb).wait()
            return 0

        lax.fori_loop(first, last, tail_start, 0)
        lax.fori_loop(first, last, tail_wait, 0)


def _experts(item_exp, item_start, item_nsub, item_tail, xs, w_gu, b_gu, w_dn, b_dn, n_rows):
    n_items = item_exp.shape[0]
    d = D_MODEL
    nf = D_EXPERT // MOE_FC
    grid_spec = pltpu.PrefetchScalarGridSpec(
        num_scalar_prefetch=4,
        grid=(n_items, nf),
        in_specs=[
            pl.BlockSpec(memory_space=pl.ANY),
            pl.BlockSpec((None, d, 2 * MOE_FC), lambda i, f, e, s, n, t: (e[i], 0, jnp.where(n[i] > 0, f, nf - 1))),
            pl.BlockSpec((None, 1, 2 * MOE_FC), lambda i, f, e, s, n, t: (e[i], 0, jnp.where(n[i] > 0, f, nf - 1))),
            pl.BlockSpec((None, MOE_FC, d), lambda i, f, e, s, n, t: (e[i], jnp.where(n[i] > 0, f, nf - 1), 0)),
            pl.BlockSpec((None, 1, d), lambda i, f, e, s, n, t: (e[i], 0, 0)),
        ],
        out_specs=pl.BlockSpec(memory_space=pl.ANY),
        scratch_shapes=[
            pltpu.VMEM((MOE_RMAX, d // 2), F32),
            pltpu.VMEM((MOE_RMAX, d), BF16),
            pltpu.VMEM((MOE_RMAX, d), F32),
            pltpu.VMEM((d, 2 * MOE_FC), BF16),
            pltpu.VMEM((MOE_FC, d), BF16),
            pltpu.SemaphoreType.DMA(()),
            pltpu.SemaphoreType.DMA(()),
        ],
    )
    return pl.pallas_call(
        _expert_body,
        grid_spec=grid_spec,
        out_shape=jax.ShapeDtypeStruct((n_rows, d), F32),
        compiler_params=_cparams(("arbitrary", "arbitrary"), vmem=58 * 1024 * 1024),
        name="experts",
    )(item_exp, item_start, item_nsub, item_tail, xs, w_gu, b_gu, w_dn, b_dn)


def _combine_body(dest_ref, ys_ref, x1_ref, wts_ref, o_ref, buf_ref, sem, *, bm):
    def copy(t, k):
        return pltpu.make_async_copy(ys_ref.at[pl.ds(dest_ref[0, 0, k * bm + t], 1), :],
                                     buf_ref.at[k, pl.ds(t, 1), :], sem)

    def start(t, _):
        for k in range(TOP_K):
            copy(t, k).start(priority=k % 2)
        return 0

    def wait(t, _):
        for k in range(TOP_K):
            copy(t, k).wait()
        return 0

    lax.fori_loop(0, bm, start, 0, unroll=8)
    lax.fori_loop(0, bm, wait, 0, unroll=8)
    w = wts_ref[...]
    acc = x1_ref[...]
    for k in range(TOP_K):
        acc = acc + w[:, k:k + 1] * buf_ref[k]
    o_ref[...] = acc


def _combine(dest3, ys, x1, wts, bm):
    m, d = x1.shape
    return pl.pallas_call(
        functools.partial(_combine_body, bm=bm),
        grid=(m // bm,),
        in_specs=[
            pl.BlockSpec((1, 1, bm * TOP_K), lambda i: (i, 0, 0), memory_space=pltpu.SMEM),
            pl.BlockSpec(memory_space=pl.ANY),
            pl.BlockSpec((bm, d), lambda i: (i, 0)),
            pl.BlockSpec((bm, LANES), lambda i: (i, 0)),
        ],
        out_specs=pl.BlockSpec((bm, d), lambda i: (i, 0)),
        out_shape=jax.ShapeDtypeStruct((m, d), F32),
        scratch_shapes=[pltpu.VMEM((TOP_K, bm, d), F32), pltpu.SemaphoreType.DMA(())],
        compiler_params=_cparams(("arbitrary",)),
        name="combine",
    )(dest3, ys, x1, wts)


def _rope_tables(seq):
    half = DA_DQK // 2
    inv = ROPE_THETA ** (-jnp.arange(0, DA_DQK, 2, dtype=F32) / DA_DQK)
    ang = jnp.arange(seq, dtype=F32)[:, None] * inv[None, :]
    cos, sin = jnp.cos(ang), jnp.sin(ang)
    reps = LANES // DA_DQK
    cosf = jnp.tile(jnp.concatenate([cos, cos], axis=1), (1, reps))
    sins = jnp.tile(jnp.concatenate([-sin, sin], axis=1), (1, reps))
    del half
    return cosf, sins


def _moe_tables(ids, ranks, cnt_row):
    t = ids.shape[1]
    counts = cnt_row.astype(jnp.int32)
    nsb = (counts + MOE_SUB - 1) // MOE_SUB
    padded = nsb * MOE_SUB
    pad_end = jnp.cumsum(padded)
    pad_start = pad_end - padded
    experts = jnp.arange(N_EXPERTS, dtype=jnp.int32)[:, None, None]
    dest = ranks + jnp.sum(jnp.where(ids[None] == experts, pad_start[:, None, None], 0), axis=0)
    per_item = MOE_RMAX // MOE_SUB
    items_e = (nsb + per_item - 1) // per_item
    item_end = jnp.cumsum(items_e)
    n_items = (t * TOP_K // MOE_SUB + N_EXPERTS * per_item) // per_item
    idx = jnp.arange(n_items, dtype=jnp.int32)
    e_of = jnp.minimum(jnp.searchsorted(item_end, idx, side="right"), N_EXPERTS - 1).astype(jnp.int32)
    local = idx - (item_end[e_of] - items_e[e_of])
    valid = idx < item_end[-1]
    nsub = jnp.where(valid, jnp.clip(nsb[e_of] - local * per_item, 0, per_item), 0).astype(jnp.int32)
    last_e = e_of[jnp.maximum(item_end[-1] - 1, 0)]
    item_exp = jnp.where(valid, e_of, last_e).astype(jnp.int32)
    item_start = jnp.where(valid, pad_start[e_of] + local * MOE_RMAX, 0).astype(jnp.int32)
    return dest.astype(jnp.int32), item_exp, item_start, nsub, pad_end.astype(jnp.int32)


def kernel(x, norm1_g, w_in, ml_gate_bias, ml_norm_g, w_ml_out, da_q_norm_g, da_k_norm_g, da_lambda, da_norm_g,
           w_da_out, w_o, norm2_g, w_router, b_router, w_gate_up, b_gate_up, w_down, b_down):
    batch, seq, d = x.shape
    depth = norm1_g.shape[0]
    tokens = batch * seq
    cosf, sins = _rope_tables(seq)
    gmat = jnp.kron(jnp.eye(LANES // DA_DQK, dtype=F32), jnp.full((DA_DQK, DA_DQK), 1.0 / DA_DQK, F32)).astype(BF16)
    n_rows = tokens * TOP_K + N_EXPERTS * MOE_SUB
    bm_tok = 512

    x2 = x.reshape(tokens, d)
    for l in range(depth):
        lam_init = 0.8 - 0.6 * math.exp(-0.3 * l)
        w = w_in[l]
        w_lo = w[:, :_OFF_MG].astype(BF16)
        w_hi = w[:, _N_MG:].astype(BF16)
        w_gate = w[:, _OFF_MG:_OFF_MG + LANES].astype(BF16)
        gbias = jnp.pad(ml_gate_bias[l].reshape(1, _N_MG), ((0, 0), (0, LANES - _N_MG)))
        proj, gates = _inproj(x2, norm1_g[l].reshape(1, d), w_lo, w_hi, w_gate, gbias)

        gp = _gateprep(gates, ML_CHUNK)
        grow = gp.reshape(4, ML_HEADS, 1, tokens)
        gcol = gp.reshape(4, ML_HEADS, batch, seq).transpose(2, 1, 3, 0)
        hm = _mlstm(proj, grow, gcol, ml_norm_g[l].reshape(1, ML_DV), batch, seq, ML_CHUNK)

        gq = jnp.tile(da_q_norm_g[l], LANES // DA_DQK).reshape(1, LANES)
        gk = jnp.tile(da_k_norm_g[l], LANES // DA_DQK).reshape(1, LANES)
        qr, kr = _qkprep(proj, cosf, sins, gq, gk, gmat, seq)
        hd = _attn(qr, kr, proj, da_lambda[l], da_norm_g[l].reshape(DA_DV, 1), batch, seq, lam_init)

        wr = jnp.pad(w_router[l], ((0, 0), (0, LANES - N_EXPERTS)))
        br = jnp.pad(b_router[l].reshape(1, N_EXPERTS), ((0, 0), (0, LANES - N_EXPERTS)))
        x1, h2, logits = _post(hm, hd, proj, x2, w_ml_out[l].astype(BF16), w_da_out[l].astype(BF16),
                               w_o[l].astype(BF16), norm2_g[l].reshape(1, d), wr, br)

        ids, ranks, wts, cnt = _route(logits)
        dest, item_exp, item_start, item_nsub, pad_end = _moe_tables(ids[:TOP_K], ranks[:TOP_K], cnt[0, :N_EXPERTS])
        dest3 = dest.reshape(TOP_K, tokens // bm_tok, bm_tok).transpose(1, 0, 2).reshape(
            tokens // bm_tok, 1, bm_tok * TOP_K)
        xs = _dispatch(pad_end, dest3, h2, n_rows, bm_tok)
        ys = _experts(item_exp, item_start, item_nsub, pad_end[-1:], xs, w_gate_up[l],
                      b_gate_up[l].reshape(N_EXPERTS, 1, 2 * D_EXPERT), w_down[l],
                      b_down[l].reshape(N_EXPERTS, 1, d), n_rows)
        x2 = _combine(dest3, ys, x1, wts, bm_tok)
    return x2.reshape(batch, seq, d)
```

```python
import functools
import math

import jax
import jax.numpy as jnp
from jax import lax
from jax.experimental import pallas as pl
from jax.experimental.pallas import tpu as pltpu

F32 = jnp.float32
BF16 = jnp.bfloat16

D_MODEL = 2048
ML_HEADS = 4
ML_DQK = 128
ML_DV = 256
ML_W = ML_HEADS * ML_DV
DA_HEADS = 8
DA_DQK = 64
DA_DV = 2 * DA_DQK
DA_W = DA_HEADS * DA_DV
ROPE_THETA = 10000.0
N_EXPERTS = 32
TOP_K = 4
D_EXPERT = D_MODEL
SWIGLU_LIMIT = 7.0
SWIGLU_ALPHA = 1.702
EPS = 1e-6
LOG2E = 1.4426950408889634

LANES = 128
VMEM_LIMIT = 48 * 1024 * 1024
VMEM_LIMIT_EXPERTS = 58 * 1024 * 1024
VMEM_LIMIT_POSTMIX = 62 * 1024 * 1024

_OFF_MG = 2 * ML_HEADS * ML_DQK + 2 * ML_W
_N_MG = 4 * ML_HEADS
_C_MQ, _C_MK, _C_MV, _C_MO = 0, 512, 1024, 2048
_C_DQ, _C_DK, _C_DV, _C_GM, _C_GD = 3072, 4096, 5120, 6144, 8192
_N_MAIN = 10240

ML_CHUNK = 512
AT_QB = 1024
AT_KB = 256
AT_SAFE_BOUND = 60.0
MOE_SUB = 256
MOE_RMAX = 2048
MOE_BIG = 1024
MOE_TILE = 256
MOE_FC = 256
MOE_NC = 512
MOE_KC = 256


def _cparams(sem, vmem=VMEM_LIMIT):
    return pltpu.CompilerParams(dimension_semantics=sem, vmem_limit_bytes=vmem)


def _pack_halves(x):
    n = x.shape[1] // 2
    lo = lax.bitcast_convert_type(x[:, :n].astype(BF16).astype(F32), jnp.uint32)
    hi = lax.bitcast_convert_type(x[:, n:].astype(BF16).astype(F32), jnp.uint32)
    return lax.bitcast_convert_type(lax.shift_right_logical(lo, jnp.uint32(16)) | hi, F32)


def _unpack_halves(w):
    u = lax.bitcast_convert_type(w, jnp.uint32)
    lo = lax.bitcast_convert_type(lax.shift_left(u, jnp.uint32(16)), F32)
    hi = lax.bitcast_convert_type(u & jnp.uint32(0xFFFF0000), F32)
    return lo, hi


def _inproj_body(x_ref, g_ref, wlo_ref, whi_ref, wg_ref, gb_ref, o_ref, og_ref, xn_ref, *, nlo):
    j = pl.program_id(1)

    @pl.when(j == 0)
    def _():
        x = x_ref[...]
        ms = jnp.mean(x * x, axis=-1, keepdims=True)
        xn = (x * lax.rsqrt(ms + EPS) * g_ref[...]).astype(BF16)
        xn_ref[...] = xn
        gates = jnp.dot(xn, wg_ref[...], preferred_element_type=F32) + gb_ref[...]
        og_ref[...] = gates.T[0:_N_MG, :]

    @pl.when(j < nlo)
    def _():
        o_ref[...] = jnp.dot(xn_ref[...], wlo_ref[...], preferred_element_type=F32).astype(o_ref.dtype)

    @pl.when(j >= nlo)
    def _():
        o_ref[...] = jnp.dot(xn_ref[...], whi_ref[...], preferred_element_type=F32).astype(o_ref.dtype)


def _inproj(x2, g1, w_lo, w_hi, w_gate, gate_bias, bm=1024, bn=1024):
    m, d = x2.shape
    n = w_hi.shape[1]
    nlo = w_lo.shape[1] // bn
    return pl.pallas_call(
        functools.partial(_inproj_body, nlo=nlo),
        grid=(m // bm, n // bn),
        in_specs=[
            pl.BlockSpec((bm, d), lambda i, j: (i, 0)),
            pl.BlockSpec((1, d), lambda i, j: (0, 0)),
            pl.BlockSpec((d, bn), lambda i, j: (0, jnp.minimum(j, nlo - 1))),
            pl.BlockSpec((d, bn), lambda i, j: (0, jnp.maximum(j, nlo))),
            pl.BlockSpec((d, LANES), lambda i, j: (0, 0)),
            pl.BlockSpec((1, LANES), lambda i, j: (0, 0)),
        ],
        out_specs=[
            pl.BlockSpec((bm, bn), lambda i, j: (i, j)),
            pl.BlockSpec((_N_MG, bm), lambda i, j: (0, i)),
        ],
        out_shape=[
            jax.ShapeDtypeStruct((m, n), BF16),
            jax.ShapeDtypeStruct((_N_MG, m), F32),
        ],
        scratch_shapes=[pltpu.VMEM((bm, d), BF16)],
        compiler_params=_cparams(("parallel", "arbitrary")),
        name="inproj",
    )(x2, g1, w_lo, w_hi, w_gate, gate_bias)


def _gateprep_body(g_ref, o_ref):
    x = g_ref[...]
    c = x.shape[1]
    lf = jnp.minimum(x, 0.0) - jnp.log1p(jnp.exp(-jnp.abs(x)))
    r = lax.broadcasted_iota(jnp.int32, (c, c), 0)
    s = lax.broadcasted_iota(jnp.int32, (c, c), 1)
    upper = (r <= s).astype(F32)
    lower = (r >= s).astype(F32)
    pre = jnp.dot(lf, upper, preferred_element_type=F32, precision=lax.Precision.HIGHEST)
    suf = jnp.dot(lf, lower, preferred_element_type=F32, precision=lax.Precision.HIGHEST)
    row = lax.broadcasted_iota(jnp.int32, x.shape, 0)
    h = ML_HEADS
    out = jnp.where((row >= h) & (row < 2 * h), pre, x)
    out = jnp.where(row >= 3 * h, suf, out)
    o_ref[...] = out


def _gateprep(gt, chunk):
    r, n = gt.shape
    return pl.pallas_call(
        _gateprep_body,
        grid=(n // chunk,),
        in_specs=[pl.BlockSpec((r, chunk), lambda i: (0, i))],
        out_specs=pl.BlockSpec((r, chunk), lambda i: (0, i)),
        out_shape=jax.ShapeDtypeStruct((r, n), F32),
        compiler_params=_cparams(("parallel",)),
        name="gateprep",
    )(gt)


def _mlstm_body(q_ref, k_ref, v_ref, mo_ref, grow_ref, gcol_ref, ng_ref, o_ref,
                qt_ref, vat_ref, hf_ref, hb_ref, *, chunk):
    s_len = q_ref.shape[0]
    nc = s_len // chunk
    dv = ML_DV
    wide = vat_ref.shape[0]
    inv_scale = float(ML_DQK) ** 0.5

    def transpose_in(c, _):
        r0 = pl.multiple_of(c * chunk, chunk)
        qt_ref[:, pl.ds(r0, chunk)] = q_ref[pl.ds(r0, chunk), :].astype(F32).T.astype(BF16)
        vat_ref[0:dv, pl.ds(r0, chunk)] = v_ref[pl.ds(r0, chunk), :].astype(F32).T.astype(BF16)
        return 0

    lax.fori_loop(0, nc, transpose_in, 0)
    extra = lax.broadcasted_iota(jnp.int32, (wide - dv, s_len), 0)
    vat_ref[dv:wide, :] = jnp.where(extra == 0, 1.0, 0.0).astype(BF16)

    ss = lax.broadcasted_iota(jnp.int32, (chunk, chunk), 0)
    tt = lax.broadcasted_iota(jnp.int32, (chunk, chunk), 1)

    def chunk_step(c, state, m_prev, reverse):
        r0 = pl.multiple_of(c * chunk, chunk)
        gi, gb = (2, 3) if reverse else (0, 1)
        qtc = qt_ref[:, pl.ds(r0, chunk)]
        kc = k_ref[pl.ds(r0, chunk), :]
        vatc = vat_ref[:, pl.ds(r0, chunk)]
        i_row = grow_ref[gi, :, pl.ds(r0, chunk)]
        b_row = grow_ref[gb, :, pl.ds(r0, chunk)]
        gcol = gcol_ref[0, pl.ds(r0, chunk), :]
        c_col = gcol[:, gb:gb + 1] - gcol[:, gi:gi + 1]
        mask = (ss >= tt) if reverse else (ss <= tt)
        log_d = jnp.where(mask, b_row - c_col, -jnp.inf)
        inter = b_row + m_prev
        m_t = jnp.maximum(inter, jnp.max(log_d, axis=0, keepdims=True))
        s_inter = jnp.exp(inter - m_t)
        dm = jnp.exp(log_d - m_t)
        skq = jnp.dot(kc, qtc, preferred_element_type=F32)
        p = (dm * skq).astype(BF16)
        nd = (jnp.dot(vatc, p, preferred_element_type=F32)
              + s_inter * jnp.dot(state.astype(BF16), qtc, preferred_element_type=F32))
        den = nd[dv:dv + 1, :]
        floor = jnp.exp(-m_t) * inv_scale
        h = (nd[:dv, :] / jnp.maximum(jnp.abs(den), floor)).T
        g = b_row[:, 0:1] if reverse else b_row[:, chunk - 1:chunk]
        a_row = g - b_row + i_row
        m_new = jnp.maximum(g + m_prev, jnp.max(a_row, axis=1, keepdims=True))
        decay = jnp.exp(g + m_prev - m_new)
        w_row = jnp.exp(a_row - m_new)
        wv = (vatc.astype(F32) * w_row).astype(BF16)
        new_state = decay * state + jnp.dot(wv, kc, preferred_element_type=F32)
        return r0, h, new_state, m_new

    def body(c, carry):
        sf, mf, sb, mb = carry
        r0, h, sf, mf = chunk_step(c, sf, mf, False)
        hf_ref[pl.ds(r0, chunk), :] = h
        r1, h2, sb, mb = chunk_step(nc - 1 - c, sb, mb, True)
        hb_ref[pl.ds(r1, chunk), :] = h2
        return sf, mf, sb, mb

    z = jnp.zeros((wide, ML_DQK), F32)
    m0 = jnp.zeros((1, 1), F32)
    lax.fori_loop(0, nc, body, (z, m0, z, m0), unroll=4)

    def epilogue(c, _):
        r0 = pl.multiple_of(c * chunk, chunk)
        h = hf_ref[pl.ds(r0, chunk), :] + hb_ref[pl.ds(r0, chunk), :]
        ms = jnp.mean(h * h, axis=-1, keepdims=True)
        hn = h * lax.rsqrt(ms + EPS) * ng_ref[...]
        gate = jax.nn.sigmoid(mo_ref[pl.ds(r0, chunk), :].astype(F32))
        o_ref[pl.ds(r0, chunk), :] = (hn * gate).astype(o_ref.dtype)
        return 0

    lax.fori_loop(0, nc, epilogue, 0)


def _mlstm(proj, grow, gcol, ng, batch, seq, chunk):
    m = proj.shape[0]
    return pl.pallas_call(
        functools.partial(_mlstm_body, chunk=chunk),
        grid=(batch, ML_HEADS),
        in_specs=[
            pl.BlockSpec((seq, ML_DQK), lambda b, h: (b, _C_MQ // ML_DQK + h)),
            pl.BlockSpec((seq, ML_DQK), lambda b, h: (b, _C_MK // ML_DQK + h)),
            pl.BlockSpec((seq, ML_DV), lambda b, h: (b, _C_MV // ML_DV + h)),
            pl.BlockSpec((seq, ML_DV), lambda b, h: (b, _C_MO // ML_DV + h)),
            pl.BlockSpec((4, None, 1, seq), lambda b, h: (0, h, 0, b)),
            pl.BlockSpec((None, 1, seq, 4), lambda b, h: (b, h, 0, 0)),
            pl.BlockSpec((1, ML_DV), lambda b, h: (0, 0)),
        ],
        out_specs=pl.BlockSpec((seq, ML_DV), lambda b, h: (b, h)),
        out_shape=jax.ShapeDtypeStruct((m, ML_W), BF16),
        scratch_shapes=[
            pltpu.VMEM((ML_DQK, seq), BF16),
            pltpu.VMEM((ML_DV + 16, seq), BF16),
            pltpu.VMEM((seq, ML_DV), F32),
            pltpu.VMEM((seq, ML_DV), F32),
        ],
        compiler_params=_cparams(("parallel", "parallel")),
        name="mlstm",
    )(proj, proj, proj, proj, grow, gcol, ng)


def _qkprep_body(q_ref, k_ref, cos_ref, sin_ref, gq_ref, gk_ref, gm_ref, oq_ref, ok_ref, *, qscale):
    cosf = cos_ref[...]
    sins = sin_ref[...]
    lane = lax.broadcasted_iota(jnp.int32, cosf.shape, 1)
    first_half = (lane % DA_DQK) < (DA_DQK // 2)
    gmat = gm_ref[...]

    def one(x_ref, g_ref, o_ref, scale):
        for j in range(x_ref.shape[1] // LANES):
            x = x_ref[:, j * LANES:(j + 1) * LANES].astype(F32)
            ms = jnp.dot((x * x).astype(BF16), gmat, preferred_element_type=F32)
            y = x * lax.rsqrt(ms + EPS) * g_ref[...]
            rot = jnp.where(first_half, pltpu.roll(y, LANES - DA_DQK // 2, 1), pltpu.roll(y, DA_DQK // 2, 1))
            o = y * cosf + rot * sins
            if scale != 1.0:
                o = o * scale
            o_ref[:, j * LANES:(j + 1) * LANES] = o.astype(o_ref.dtype)

    one(q_ref, gq_ref, oq_ref, qscale)
    one(k_ref, gk_ref, ok_ref, 1.0)


def _qkprep(proj, cosf, sins, gq, gk, gmat, seq, bm=512):
    m = proj.shape[0]
    nsb = seq // bm
    return pl.pallas_call(
        functools.partial(_qkprep_body, qscale=float(DA_DQK) ** -0.5 * LOG2E),
        grid=(m // bm,),
        in_specs=[
            pl.BlockSpec((bm, DA_W), lambda i: (i, _C_DQ // DA_W)),
            pl.BlockSpec((bm, DA_W), lambda i: (i, _C_DK // DA_W)),
            pl.BlockSpec((bm, LANES), lambda i: (i % nsb, 0)),
            pl.BlockSpec((bm, LANES), lambda i: (i % nsb, 0)),
            pl.BlockSpec((1, LANES), lambda i: (0, 0)),
            pl.BlockSpec((1, LANES), lambda i: (0, 0)),
            pl.BlockSpec((LANES, LANES), lambda i: (0, 0)),
        ],
        out_specs=[
            pl.BlockSpec((bm, DA_W), lambda i: (i, 0)),
            pl.BlockSpec((bm, DA_W), lambda i: (i, 0)),
        ],
        out_shape=[jax.ShapeDtypeStruct((m, DA_W), BF16)] * 2,
        compiler_params=_cparams(("parallel",)),
        name="qkprep",
    )(proj, proj, cosf, sins, gq, gk, gmat)


def _attn_body(q_ref, k_ref, v_ref, lam_ref, ng_ref, o_ref, vt_ref, kmax_ref, *, kb, lam_init):
    s_len = k_ref.shape[0]

    @pl.when(pl.program_id(2) == 0)
    def _():
        def transpose_v(c, _):
            r0 = pl.multiple_of(c * kb, kb)
            vt_ref[0:DA_DV, pl.ds(r0, kb)] = v_ref[pl.ds(r0, kb), :].astype(F32).T.astype(BF16)
            return 0

        lax.fori_loop(0, s_len // kb, transpose_v, 0)
        vt_ref[DA_DV:, :] = jnp.ones((vt_ref.shape[0] - DA_DV, s_len), BF16)

        gr = lax.broadcasted_iota(jnp.int32, (DA_DV, DA_DV), 0) // DA_DQK
        gc = lax.broadcasted_iota(jnp.int32, (DA_DV, DA_DV), 1) // DA_DQK
        group_sum = jnp.where(gr == gc, 1.0, 0.0).astype(BF16)

        def key_norms(c, mx):
            r0 = pl.multiple_of(c * kb, kb)
            kk = k_ref[pl.ds(r0, kb), :].astype(F32)
            n2 = jnp.dot((kk * kk).astype(BF16), group_sum, preferred_element_type=F32)
            return jnp.maximum(mx, jnp.max(n2, axis=0, keepdims=True))

        mx = lax.fori_loop(0, s_len // kb, key_norms, jnp.zeros((1, DA_DV), F32))
        kmax_ref[...] = jnp.broadcast_to(jnp.sqrt(mx), kmax_ref.shape)

    qt = q_ref[...].astype(F32).T.astype(BF16)
    qb = qt.shape[1]
    row = lax.broadcasted_iota(jnp.int32, qt.shape, 0)
    zero = jnp.zeros_like(qt)
    q1 = jnp.where(row < DA_DQK, qt, zero)
    q2 = jnp.where(row >= DA_DQK, qt, zero)
    nv = vt_ref.shape[0]

    def scores(j):
        kblk = k_ref[j * kb:(j + 1) * kb, :]
        return (jnp.dot(kblk, q1, preferred_element_type=F32),
                jnp.dot(kblk, q2, preferred_element_type=F32))

    def attend(upd, init):
        st1 = st2 = init
        s_cur = scores(0)
        for j in range(s_len // kb):
            s_nxt = scores(j + 1) if (j + 1) * kb < s_len else None
            vb = vt_ref[:, j * kb:(j + 1) * kb]
            st1 = upd(s_cur[0], st1, vb, 0)
            st2 = upd(s_cur[1], st2, vb, 1)
            s_cur = s_nxt
        return st1, st2

    def finish(a1, a2):
        lp = lam_ref[...]
        lam = (jnp.exp(jnp.sum(lp[0:1, :] * lp[1:2, :], axis=1, keepdims=True))
               - jnp.exp(jnp.sum(lp[2:3, :] * lp[3:4, :], axis=1, keepdims=True)) + lam_init)
        o = a1[:DA_DV, :] / a1[DA_DV:DA_DV + 1, :] - lam * (a2[:DA_DV, :] / a2[DA_DV:DA_DV + 1, :])
        ms = jnp.mean(o * o, axis=0, keepdims=True)
        on = o * lax.rsqrt(ms + EPS) * ng_ref[...] * (1.0 - lam_init)
        o_ref[...] = on.T.astype(o_ref.dtype)

    sq = qt.astype(F32) * qt.astype(F32)
    inflate = 1.0 + 2.0 ** -6
    bounds = (jnp.sqrt(jnp.sum(jnp.where(row < DA_DQK, sq, 0.0), axis=0, keepdims=True))
              * kmax_ref[0:1, 0:1] * inflate,
              jnp.sqrt(jnp.sum(jnp.where(row >= DA_DQK, sq, 0.0), axis=0, keepdims=True))
              * kmax_ref[0:1, DA_DQK:DA_DQK + 1] * inflate)
    safe = jnp.max(jnp.maximum(bounds[0], bounds[1])) < AT_SAFE_BOUND

    @pl.when(safe)
    def _():
        def upd(s, acc, vb, which):
            p = jnp.exp2(s - bounds[which]).astype(BF16)
            return acc + jnp.dot(vb, p, preferred_element_type=F32)

        finish(*attend(upd, jnp.zeros((nv, qb), F32)))

    @pl.when(jnp.logical_not(safe))
    def _():
        def upd(s, state, vb, which):
            m_old, acc = state
            m_new = jnp.maximum(m_old, jnp.max(s, axis=0, keepdims=True))
            alpha = jnp.exp2(m_old - m_new)
            p = jnp.exp2(s - m_new).astype(BF16)
            return m_new, alpha * acc + jnp.dot(vb, p, preferred_element_type=F32)

        init = (jnp.full((1, qb), -jnp.inf, F32), jnp.zeros((nv, qb), F32))
        (_, a1), (_, a2) = attend(upd, init)
        finish(a1, a2)


def _attn(qr, kr, proj, lam_params, ng_col, batch, seq, lam_init, qb=AT_QB, kb=AT_KB):
    nq = seq // qb
    return pl.pallas_call(
        functools.partial(_attn_body, kb=kb, lam_init=lam_init),
        grid=(batch, DA_HEADS, nq),
        in_specs=[
            pl.BlockSpec((qb, DA_DV), lambda b, h, i: (b * nq + i, h)),
            pl.BlockSpec((seq, DA_DV), lambda b, h, i: (b, h)),
            pl.BlockSpec((seq, DA_DV), lambda b, h, i: (b, _C_DV // DA_DV + h)),
            pl.BlockSpec((4, DA_DQK), lambda b, h, i: (0, 0)),
            pl.BlockSpec((DA_DV, 1), lambda b, h, i: (0, 0)),
        ],
        out_specs=pl.BlockSpec((qb, DA_DV), lambda b, h, i: (b * nq + i, h)),
        out_shape=jax.ShapeDtypeStruct((batch * seq, DA_W), BF16),
        scratch_shapes=[pltpu.VMEM((DA_DV + 8, seq), BF16), pltpu.VMEM((8, DA_DV), F32)],
        compiler_params=_cparams(("parallel", "parallel", "arbitrary")),
        name="diffattn",
    )(qr, kr, proj, lam_params, ng_col)


def _post_body(hm_ref, hd_ref, gm_ref, gd_ref, x_ref, wm_ref, wd_ref, wo_ref, g2_ref, wrh_ref, wrl_ref, br_ref,
               x1_ref, h2_ref, lg_ref, *, tile):
    rows = [slice(t * tile, (t + 1) * tile) for t in range(x_ref.shape[0] // tile)]
    dot = functools.partial(jnp.dot, preferred_element_type=F32)
    ys = [(dot(hm_ref[r, :], wm_ref[...]), dot(hd_ref[r, :], wd_ref[...])) for r in rows]
    mixes = [(jax.nn.sigmoid(gm_ref[r, :].astype(F32)) * ym
              + jax.nn.sigmoid(gd_ref[r, :].astype(F32)) * yd).astype(BF16) for r, (ym, yd) in zip(rows, ys)]
    x1s = [x_ref[r, :] + dot(mix, wo_ref[...]) for r, mix in zip(rows, mixes)]
    for r, x1 in zip(rows, x1s):
        x1_ref[r, :] = x1
        ms = jnp.mean(x1 * x1, axis=-1, keepdims=True)
        h2 = x1 * lax.rsqrt(ms + EPS) * g2_ref[...]
        h2_ref[r, :] = _pack_halves(h2)
        hi = h2.astype(BF16)
        lo = (h2 - hi.astype(F32)).astype(BF16)
        lg_ref[r, :] = (dot(hi, wrh_ref[...]) + dot(hi, wrl_ref[...]) + dot(lo, wrh_ref[...])) + br_ref[...]


def _post(hm, hd, proj, x2, wm, wd, wo, g2, wr, br, bm=512, tile=256):
    m, d = x2.shape
    const = lambda i: (0, 0)
    wr_hi = wr.astype(BF16)
    wr_lo = (wr - wr_hi.astype(F32)).astype(BF16)
    return pl.pallas_call(
        functools.partial(_post_body, tile=tile),
        grid=(m // bm,),
        in_specs=[
            pl.BlockSpec((bm, ML_W), lambda i: (i, 0)),
            pl.BlockSpec((bm, DA_W), lambda i: (i, 0)),
            pl.BlockSpec((bm, d), lambda i: (i, _C_GM // D_MODEL)),
            pl.BlockSpec((bm, d), lambda i: (i, _C_GD // D_MODEL)),
            pl.BlockSpec((bm, d), lambda i: (i, 0)),
            pl.BlockSpec((ML_W, d), const, pipeline_mode=pl.Buffered(1)),
            pl.BlockSpec((DA_W, d), const, pipeline_mode=pl.Buffered(1)),
            pl.BlockSpec((d, d), const, pipeline_mode=pl.Buffered(1)),
            pl.BlockSpec((1, d), const),
            pl.BlockSpec((d, LANES), const),
            pl.BlockSpec((d, LANES), const),
            pl.BlockSpec((1, LANES), const),
        ],
        out_specs=[
            pl.BlockSpec((bm, d), lambda i: (i, 0)),
            pl.BlockSpec((bm, d // 2), lambda i: (i, 0)),
            pl.BlockSpec((bm, LANES), lambda i: (i, 0)),
        ],
        out_shape=[
            jax.ShapeDtypeStruct((m, d), F32),
            jax.ShapeDtypeStruct((m, d // 2), F32),
            jax.ShapeDtypeStruct((m, LANES), F32),
        ],
        compiler_params=_cparams(("parallel",), vmem=VMEM_LIMIT_POSTMIX),
        name="postmix",
    )(hm, hd, proj, proj, x2, wm, wd, wo, g2, wr_hi, wr_lo, br)


def _route_body(lg_ref, ids_ref, rank_ref, wts_ref, cnt_ref, carry_ref):
    @pl.when(pl.program_id(0) == 0)
    def _():
        carry_ref[...] = jnp.zeros_like(carry_ref)

    lg = lg_ref[...]
    bm = lg.shape[0]
    lane = lax.broadcasted_iota(jnp.int32, lg.shape, 1)
    lanef = lane.astype(F32)
    work = jnp.where(lane < N_EXPERTS, lg, -jnp.inf)
    vals, hots = [], []
    ids = jnp.zeros(lg.shape, F32)
    for k in range(TOP_K):
        mx = jnp.max(work, axis=1, keepdims=True)
        idx = jnp.min(jnp.where(work == mx, lanef, float(LANES)), axis=1, keepdims=True)
        hot = lanef == idx
        work = jnp.where(hot, -jnp.inf, work)
        vals.append(mx)
        hots.append(hot)
        ids = jnp.where(lane == k, idx, ids)
    exps = [jnp.exp(v - vals[0]) for v in vals]
    tot = exps[0] + exps[1] + exps[2] + exps[3]
    sel = jnp.zeros(lg.shape, F32)
    for hot in hots:
        sel = jnp.where(hot, 1.0, sel)
    r = lax.broadcasted_iota(jnp.int32, (bm, bm), 0)
    c = lax.broadcasted_iota(jnp.int32, (bm, bm), 1)
    strict = jnp.where(c < r, 1.0, 0.0).astype(BF16)
    cum = jnp.dot(strict, sel.astype(BF16), preferred_element_type=F32) + carry_ref[0:1, :]
    ranks = jnp.zeros(lg.shape, F32)
    wts = jnp.zeros(lg.shape, F32)
    for k in range(TOP_K):
        rk = jnp.sum(jnp.where(hots[k], cum, 0.0), axis=1, keepdims=True)
        ranks = jnp.where(lane == k, rk, ranks)
        wts = jnp.where(lane == k, exps[k] / tot, wts)
    newc = carry_ref[0:1, :] + jnp.sum(sel, axis=0, keepdims=True)
    carry_ref[...] = jnp.broadcast_to(newc, carry_ref.shape)
    ids_ref[...] = ids.T[0:8, :].astype(jnp.int32)
    rank_ref[...] = ranks.T[0:8, :].astype(jnp.int32)
    wts_ref[...] = wts
    cnt_ref[...] = jnp.broadcast_to(newc, cnt_ref.shape)


def _route(logits, bm=512):
    m = logits.shape[0]
    blk = pl.BlockSpec((bm, LANES), lambda i: (i, 0))
    tblk = pl.BlockSpec((8, bm), lambda i: (0, i))
    return pl.pallas_call(
        _route_body,
        grid=(m // bm,),
        in_specs=[blk],
        out_specs=[tblk, tblk, blk, pl.BlockSpec((8, LANES), lambda i: (0, 0))],
        out_shape=[
            jax.ShapeDtypeStruct((8, m), jnp.int32),
            jax.ShapeDtypeStruct((8, m), jnp.int32),
            jax.ShapeDtypeStruct((m, LANES), F32),
            jax.ShapeDtypeStruct((8, LANES), F32),
        ],
        scratch_shapes=[pltpu.VMEM((8, LANES), F32)],
        compiler_params=_cparams(("arbitrary",)),
        name="route",
    )(logits)


def _dispatch_body(pend_ref, dest_ref, h2_ref, xs_ref, zero_ref, sem, zsem, *, bm):
    @pl.when(pl.program_id(0) == 0)
    def _():
        zero_ref[...] = jnp.zeros_like(zero_ref)
        n_sub_total = xs_ref.shape[0] // MOE_SUB
        first_tail = lax.shift_right_logical(pend_ref[N_EXPERTS - 1], MOE_SUB.bit_length() - 1)

        def zcopy(r0):
            return pltpu.make_async_copy(zero_ref, xs_ref.at[pl.ds(pl.multiple_of(r0, MOE_SUB), MOE_SUB), :], zsem)

        def pad_row(e):
            return jnp.maximum(pend_ref[e] - MOE_SUB, 0)

        def nonempty(e):
            return pend_ref[e] > (pend_ref[e - 1] if e else 0)

        for e in range(N_EXPERTS):
            @pl.when(nonempty(e))
            def _():
                zcopy(pad_row(e)).start()

        def tail_start(sb, _):
            zcopy(sb * MOE_SUB).start()
            return 0

        def tail_wait(sb, _):
            zcopy(sb * MOE_SUB).wait()
            return 0

        lax.fori_loop(first_tail, n_sub_total, tail_start, 0)
        for e in range(N_EXPERTS):
            @pl.when(nonempty(e))
            def _():
                zcopy(pad_row(e)).wait()
        lax.fori_loop(first_tail, n_sub_total, tail_wait, 0)

    def copy(t, k):
        return pltpu.make_async_copy(h2_ref.at[pl.ds(t, 1), :],
                                     xs_ref.at[pl.ds(dest_ref[0, 0, k * bm + t], 1), :], sem)

    def start(t, _):
        for k in range(TOP_K):
            copy(t, k).start(priority=k % 2)
        return 0

    def wait(t, _):
        for k in range(TOP_K):
            copy(t, k).wait()
        return 0

    lax.fori_loop(0, bm, start, 0, unroll=8)
    lax.fori_loop(0, bm, wait, 0, unroll=8)


def _dispatch(pad_end, dest3, h2, n_rows, bm):
    m, width = h2.shape
    grid_spec = pltpu.PrefetchScalarGridSpec(
        num_scalar_prefetch=1,
        grid=(m // bm,),
        in_specs=[
            pl.BlockSpec((1, 1, bm * TOP_K), lambda i, p: (i, 0, 0), memory_space=pltpu.SMEM),
            pl.BlockSpec((bm, width), lambda i, p: (i, 0)),
        ],
        out_specs=pl.BlockSpec(memory_space=pl.ANY),
        scratch_shapes=[
            pltpu.VMEM((MOE_SUB, width), h2.dtype),
            pltpu.SemaphoreType.DMA(()),
            pltpu.SemaphoreType.DMA(()),
        ],
    )
    return pl.pallas_call(
        functools.partial(_dispatch_body, bm=bm),
        grid_spec=grid_spec,
        out_shape=jax.ShapeDtypeStruct((n_rows, width), h2.dtype),
        compiler_params=_cparams(("arbitrary",)),
        name="dispatch",
    )(pad_end, dest3, h2)


def _expert_body(iexp_ref, istart_ref, insub_ref, itail_ref, xs_ref, wgu_ref, bgu_ref, wdn_ref, bdn_ref,
                 ys_ref, xstage_ref, xb_ref, acc_ref, wgb_ref, wdb_ref, sem_in, sem_out):
    del iexp_ref
    i = pl.program_id(0)
    f = pl.program_id(1)
    nf = pl.num_programs(1)
    nsub = insub_ref[i]
    start = pl.multiple_of(istart_ref[i], MOE_SUB)
    d = acc_ref.shape[1]
    half = d // 2

    def sub_rows(sb):
        return pl.ds(pl.multiple_of(sb * MOE_SUB, MOE_SUB), MOE_SUB)

    @pl.when((i == 0) & (f == 0))
    def _():
        xb_ref[0:MOE_BIG, :] = jnp.zeros((MOE_BIG, d), BF16)

    def out_copy(first_row, sb):
        r0 = pl.multiple_of(sb * MOE_SUB, MOE_SUB)
        return pltpu.make_async_copy(acc_ref.at[pl.ds(r0, MOE_SUB), :],
                                     ys_ref.at[pl.ds(first_row + r0, MOE_SUB), :], sem_out)

    def x_copy(first_row, sb):
        r0 = pl.multiple_of(sb * MOE_SUB, MOE_SUB)
        return pltpu.make_async_copy(xs_ref.at[pl.ds(first_row + r0, MOE_SUB), :],
                                     xstage_ref.at[pl.ds(r0, MOE_SUB), :], sem_in)

    @pl.when((f == 0) & (i == 0))
    def _():
        def x_start(sb, _):
            x_copy(start, sb).start()
            return 0

        lax.fori_loop(0, nsub, x_start, 0)

    @pl.when((f == nf - 1) & (i + 1 < pl.num_programs(0)))
    def _():
        nxt = jnp.minimum(i + 1, pl.num_programs(0) - 1)
        nxt_start = pl.multiple_of(istart_ref[nxt], MOE_SUB)

        def x_start(sb, _):
            x_copy(nxt_start, sb).start()
            return 0

        lax.fori_loop(0, insub_ref[nxt], x_start, 0)

    @pl.when((f == 0) & (nsub > 0))
    def _():
        def x_wait(sb, _):
            x_copy(start, sb).wait()
            return 0

        def unpack(sb, _):
            lo, hi = _unpack_halves(xstage_ref[sub_rows(sb), :])
            xb_ref[sub_rows(sb), 0:half] = lo.astype(BF16)
            xb_ref[sub_rows(sb), half:d] = hi.astype(BF16)
            return 0

        lax.fori_loop(0, nsub, x_wait, 0)
        lax.fori_loop(0, nsub, unpack, 0)

    @pl.when((f == 0) & (i > 0))
    def _():
        prev = jnp.maximum(i - 1, 0)
        prev_start = pl.multiple_of(istart_ref[prev], MOE_SUB)

        def prev_wait(sb, _):
            out_copy(prev_start, sb).wait()
            return 0

        lax.fori_loop(0, insub_ref[prev], prev_wait, 0)

    @pl.when((f == 0) & (nsub > 0))
    def _():
        def init(sb, _):
            acc_ref[sub_rows(sb), :] = jnp.broadcast_to(bdn_ref[...], (MOE_SUB, d))
            return 0

        lax.fori_loop(0, jnp.maximum(nsub, MOE_BIG // MOE_SUB), init, 0)

    @pl.when(nsub > 0)
    def _():
        def activation(gu):
            lane = lax.broadcasted_iota(jnp.int32, (gu.shape[0], LANES), 1)
            low = lane < LANES // 2
            idx = jnp.where(low, 2 * lane, 2 * lane - (LANES - 1))
            glus, lins = [], []
            for j in range(0, 2 * MOE_FC, 2 * LANES):
                pa = jnp.take_along_axis(gu[:, j:j + LANES], idx, axis=1)
                pb = jnp.take_along_axis(gu[:, j + LANES:j + 2 * LANES], idx, axis=1)
                glus.append(jnp.where(low, pa, pltpu.roll(pb, LANES // 2, 1)))
                lins.append(jnp.where(low, pltpu.roll(pa, LANES // 2, 1), pb))
            glu = jnp.minimum(jnp.concatenate(glus, axis=1), SWIGLU_LIMIT)
            lin = jnp.clip(jnp.concatenate(lins, axis=1), -SWIGLU_LIMIT, SWIGLU_LIMIT)
            return (glu * jax.nn.sigmoid(SWIGLU_ALPHA * glu) * (lin + 1.0)).astype(BF16)

        def first_up(r):
            g = None
            for k0 in range(0, d, MOE_KC):
                wgb_ref[k0:k0 + MOE_KC, :] = wgu_ref[k0:k0 + MOE_KC, :].astype(BF16)
                part = jnp.dot(xb_ref[r, k0:k0 + MOE_KC], wgb_ref[k0:k0 + MOE_KC, :], preferred_element_type=F32)
                g = part if g is None else g + part
            wdb_ref[...] = wdn_ref[...].astype(BF16)
            return g

        def block(r0, rows, cast_first=False):
            tile = min(MOE_TILE, rows)
            tiles = [pl.ds(r0 + t, tile) for t in range(0, rows, tile)]
            gus = [(first_up(r) if cast_first and t == 0
                    else jnp.dot(xb_ref[r, :], wgb_ref[...], preferred_element_type=F32)) + bgu_ref[...]
                   for t, r in enumerate(tiles)]
            acts = [activation(gu) for gu in gus]
            for r, act in zip(tiles, acts):
                for n0 in range(0, d, MOE_NC):
                    acc_ref[r, n0:n0 + MOE_NC] += jnp.dot(
                        act, wdb_ref[:, n0:n0 + MOE_NC], preferred_element_type=F32)

        per_big = MOE_BIG // MOE_SUB

        @pl.when(nsub == per_big + 1)
        def _():
            block(0, MOE_BIG + MOE_SUB, cast_first=True)

        @pl.when(nsub != per_big + 1)
        def _():
            block(0, MOE_BIG, cast_first=True)
            nbig = lax.shift_right_logical(jnp.maximum(nsub, per_big), per_big.bit_length() - 1)

            def big(b, _):
                block(pl.multiple_of(b * MOE_BIG, MOE_BIG), MOE_BIG)
                return 0

            def small(sb, _):
                block(pl.multiple_of(sb * MOE_SUB, MOE_SUB), MOE_SUB)
                return 0

            lax.fori_loop(1, nbig, big, 0)
            lax.fori_loop(nbig * per_big, nsub, small, 0)

    @pl.when(f == nf - 1)
    def _():
        def out_start(sb, _):
            out_copy(start, sb).start()
            return 0

        lax.fori_loop(0, nsub, out_start, 0)

    @pl.when((f == nf - 1) & (i == pl.num_programs(0) - 1))
    def _():
        def out_wait(sb, _):
            out_copy(start, sb).wait()
            return 0

        lax.fori_loop(0, nsub, out_wait, 0)

    @pl.when((i == pl.num_programs(0) - 1) & (f == nf - 1))
    def _():
        first = lax.shift_right_logical(itail_ref[0], MOE_SUB.bit_length() - 1)
        last = ys_ref.shape[0] // MOE_SUB
        acc_ref[0:MOE_SUB, :] = jnp.zeros((MOE_SUB, d), F32)

        def tail_copy(sb):
            r0 = pl.multiple_of(sb * MOE_SUB, MOE_SUB)
            return pltpu.make_async_copy(acc_ref.at[0:MOE_SUB, :], ys_ref.at[pl.ds(r0, MOE_SUB), :], sem_out)

        def tail_start(sb, _):
            tail_copy(sb).start()
            return 0

        def tail_wait(sb, _):
            tail_copy(sb).wait()
            return 0

        lax.fori_loop(first, last, tail_start, 0)
        lax.fori_loop(first, last, tail_wait, 0)


def _experts(item_exp, item_start, item_nsub, item_tail, xs, w_gu, b_gu, w_dn, b_dn, n_rows):
    n_items = item_exp.shape[0]
    d = D_MODEL
    nf = D_EXPERT // MOE_FC
    grid_spec = pltpu.PrefetchScalarGridSpec(
        num_scalar_prefetch=4,
        grid=(n_items, nf),
        in_specs=[
            pl.BlockSpec(memory_space=pl.ANY),
            pl.BlockSpec((None, d, 2 * MOE_FC), lambda i, f, e, s, n, t: (e[i], 0, jnp.where(n[i] > 0, f, nf - 1))),
            pl.BlockSpec((None, 1, 2 * MOE_FC), lambda i, f, e, s, n, t: (e[i], 0, jnp.where(n[i] > 0, f, nf - 1))),
            pl.BlockSpec((None, MOE_FC, d), lambda i, f, e, s, n, t: (e[i], jnp.where(n[i] > 0, f, nf - 1), 0)),
            pl.BlockSpec((None, 1, d), lambda i, f, e, s, n, t: (e[i], 0, 0)),
        ],
        out_specs=pl.BlockSpec(memory_space=pl.ANY),
        scratch_shapes=[
            pltpu.VMEM((MOE_RMAX, d // 2), F32),
            pltpu.VMEM((MOE_RMAX, d), BF16),
            pltpu.VMEM((MOE_RMAX, d), F32),
            pltpu.VMEM((d, 2 * MOE_FC), BF16),
            pltpu.VMEM((MOE_FC, d), BF16),
            pltpu.SemaphoreType.DMA(()),
            pltpu.SemaphoreType.DMA(()),
        ],
    )
    return pl.pallas_call(
        _expert_body,
        grid_spec=grid_spec,
        out_shape=jax.ShapeDtypeStruct((n_rows, d), F32),
        compiler_params=_cparams(("arbitrary", "arbitrary"), vmem=VMEM_LIMIT_EXPERTS),
        name="experts",
    )(item_exp, item_start, item_nsub, item_tail, xs, w_gu, b_gu, w_dn, b_dn)


def _combine_body(dest_ref, ys_ref, x1_ref, wts_ref, o_ref, buf_ref, sem, *, bm):
    def copy(t, k):
        return pltpu.make_async_copy(ys_ref.at[pl.ds(dest_ref[0, 0, k * bm + t], 1), :],
                                     buf_ref.at[k, pl.ds(t, 1), :], sem)

    def start(t, _):
        for k in range(TOP_K):
            copy(t, k).start(priority=k % 2)
        return 0

    def wait(t, _):
        for k in range(TOP_K):
            copy(t, k).wait()
        return 0

    lax.fori_loop(0, bm, start, 0, unroll=8)
    lax.fori_loop(0, bm, wait, 0, unroll=8)
    w = wts_ref[...]
    acc = x1_ref[...]
    for k in range(TOP_K):
        acc = acc + w[:, k:k + 1] * buf_ref[k]
    o_ref[...] = acc


def _combine(dest3, ys, x1, wts, bm):
    m, d = x1.shape
    return pl.pallas_call(
        functools.partial(_combine_body, bm=bm),
        grid=(m // bm,),
        in_specs=[
            pl.BlockSpec((1, 1, bm * TOP_K), lambda i: (i, 0, 0), memory_space=pltpu.SMEM),
            pl.BlockSpec(memory_space=pl.ANY),
            pl.BlockSpec((bm, d), lambda i: (i, 0)),
            pl.BlockSpec((bm, LANES), lambda i: (i, 0)),
        ],
        out_specs=pl.BlockSpec((bm, d), lambda i: (i, 0)),
        out_shape=jax.ShapeDtypeStruct((m, d), F32),
        scratch_shapes=[pltpu.VMEM((TOP_K, bm, d), F32), pltpu.SemaphoreType.DMA(())],
        compiler_params=_cparams(("arbitrary",)),
        name="combine",
    )(dest3, ys, x1, wts)


def _rope_tables(seq):
    half = DA_DQK // 2
    inv = ROPE_THETA ** (-jnp.arange(0, DA_DQK, 2, dtype=F32) / DA_DQK)
    ang = jnp.arange(seq, dtype=F32)[:, None] * inv[None, :]
    cos, sin = jnp.cos(ang), jnp.sin(ang)
    reps = LANES // DA_DQK
    cosf = jnp.tile(jnp.concatenate([cos, cos], axis=1), (1, reps))
    sins = jnp.tile(jnp.concatenate([-sin, sin], axis=1), (1, reps))
    del half
    return cosf, sins


def _moe_tables(ids, ranks, cnt_row):
    t = ids.shape[1]
    counts = cnt_row.astype(jnp.int32)
    nsb = (counts + MOE_SUB - 1) // MOE_SUB
    padded = nsb * MOE_SUB
    pad_end = jnp.cumsum(padded)
    pad_start = pad_end - padded
    experts = jnp.arange(N_EXPERTS, dtype=jnp.int32)[:, None, None]
    dest = ranks + jnp.sum(jnp.where(ids[None] == experts, pad_start[:, None, None], 0), axis=0)
    per_item = MOE_RMAX // MOE_SUB
    items_e = (nsb + per_item - 1) // per_item
    item_end = jnp.cumsum(items_e)
    n_items = (t * TOP_K // MOE_SUB + N_EXPERTS * per_item) // per_item
    idx = jnp.arange(n_items, dtype=jnp.int32)
    e_of = jnp.minimum(jnp.searchsorted(item_end, idx, side="right"), N_EXPERTS - 1).astype(jnp.int32)
    local = idx - (item_end[e_of] - items_e[e_of])
    valid = idx < item_end[-1]
    nsub = jnp.where(valid, jnp.clip(nsb[e_of] - local * per_item, 0, per_item), 0).astype(jnp.int32)
    last_e = e_of[jnp.maximum(item_end[-1] - 1, 0)]
    item_exp = jnp.where(valid, e_of, last_e).astype(jnp.int32)
    item_start = jnp.where(valid, pad_start[e_of] + local * MOE_RMAX, 0).astype(jnp.int32)
    return dest.astype(jnp.int32), item_exp, item_start, nsub, pad_end.astype(jnp.int32)


def kernel(x, norm1_g, w_in, ml_gate_bias, ml_norm_g, w_ml_out, da_q_norm_g, da_k_norm_g, da_lambda, da_norm_g,
           w_da_out, w_o, norm2_g, w_router, b_router, w_gate_up, b_gate_up, w_down, b_down):
    batch, seq, d = x.shape
    depth = norm1_g.shape[0]
    tokens = batch * seq
    cosf, sins = _rope_tables(seq)
    gmat = jnp.kron(jnp.eye(LANES // DA_DQK, dtype=F32), jnp.full((DA_DQK, DA_DQK), 1.0 / DA_DQK, F32)).astype(BF16)
    n_rows = tokens * TOP_K + N_EXPERTS * MOE_SUB
    bm_tok = 512

    x2 = x.reshape(tokens, d)
    for l in range(depth):
        lam_init = 0.8 - 0.6 * math.exp(-0.3 * l)
        w = w_in[l]
        w_lo = w[:, :_OFF_MG].astype(BF16)
        w_hi = w[:, _N_MG:].astype(BF16)
        w_gate = w[:, _OFF_MG:_OFF_MG + LANES].astype(BF16)
        gbias = jnp.pad(ml_gate_bias[l].reshape(1, _N_MG), ((0, 0), (0, LANES - _N_MG)))
        proj, gates = _inproj(x2, norm1_g[l].reshape(1, d), w_lo, w_hi, w_gate, gbias)

        gp = _gateprep(gates, ML_CHUNK)
        grow = gp.reshape(4, ML_HEADS, 1, tokens)
        gcol = gp.reshape(4, ML_HEADS, batch, seq).transpose(2, 1, 3, 0)
        hm = _mlstm(proj, grow, gcol, ml_norm_g[l].reshape(1, ML_DV), batch, seq, ML_CHUNK)

        gq = jnp.tile(da_q_norm_g[l], LANES // DA_DQK).reshape(1, LANES)
        gk = jnp.tile(da_k_norm_g[l], LANES // DA_DQK).reshape(1, LANES)
        qr, kr = _qkprep(proj, cosf, sins, gq, gk, gmat, seq)
        hd = _attn(qr, kr, proj, da_lambda[l], da_norm_g[l].reshape(DA_DV, 1), batch, seq, lam_init)

        wr = jnp.pad(w_router[l], ((0, 0), (0, LANES - N_EXPERTS)))
        br = jnp.pad(b_router[l].reshape(1, N_EXPERTS), ((0, 0), (0, LANES - N_EXPERTS)))
        x1, h2, logits = _post(hm, hd, proj, x2, w_ml_out[l].astype(BF16), w_da_out[l].astype(BF16),
                               w_o[l].astype(BF16), norm2_g[l].reshape(1, d), wr, br)

        ids, ranks, wts, cnt = _route(logits)
        dest, item_exp, item_start, item_nsub, pad_end = _moe_tables(ids[:TOP_K], ranks[:TOP_K], cnt[0, :N_EXPERTS])
        dest3 = dest.reshape(TOP_K, tokens // bm_tok, bm_tok).transpose(1, 0, 2).reshape(
            tokens // bm_tok, 1, bm_tok * TOP_K)
        xs = _dispatch(pad_end, dest3, h2, n_rows, bm_tok)
        ys = _experts(item_exp, item_start, item_nsub, pad_end[-1:], xs, w_gate_up[l],
                      b_gate_up[l].reshape(N_EXPERTS, 1, 2 * D_EXPERT), w_down[l],
                      b_down[l].reshape(N_EXPERTS, 1, d), n_rows)
        x2 = _combine(dest3, ys, x1, wts, bm_tok)
    return x2.reshape(batch, seq, d)
```
